```python
import jax, jax.numpy as jnp
from jax import lax
import numpy as np

D_MODEL = 2048
BATCH = 32
SEQ = 256
DEPTH = 4
DEC_BATCH = 4
DEC_SEQ = 1024
PAST_LEN = 512

GRID_W = 64
H_RET = 8
DK_RET = 128
DV_RET = 128
D_RET = H_RET * DV_RET
H_ATT = 8
H_KV = 2
G_ATT = H_ATT // H_KV
DH_ATT = 128
D_ATT = H_ATT * DH_ATT
D_MIX = D_RET + D_ATT
D_IN = 2 * H_RET * DK_RET + 2 * H_RET * DV_RET + H_ATT * DH_ATT + 2 * H_KV * DH_ATT
WINDOW = 128
BLOCK = 128
D_FF = 5632
CONV_W = 3
N_MOD = 6
ROPE_BASE = 10000.0
EPS = 1e-6
NEG_INF = -1e30

kernel_name = "hybrid_retention_swa_prefix_dit_step"


def rms_norm(x):
    xf = x.astype(jnp.float32)
    return (xf * lax.rsqrt(jnp.mean(xf * xf, axis=-1, keepdims=True) + EPS)).astype(x.dtype)


def modulation(cond, w_mod_l, b_mod_l):
    m = jax.nn.silu(cond) @ w_mod_l + b_mod_l
    return jnp.split(m[..., None, :], N_MOD, axis=-1)


def modulate(x, shift, scale):
    return rms_norm(x) * (1 + scale) + shift


def project(h, w_in_l):
    B, T, _ = h.shape
    sizes = (H_RET * DK_RET, H_RET * DK_RET, H_RET * DV_RET, H_RET * DV_RET,
             H_ATT * DH_ATT, H_KV * DH_ATT, H_KV * DH_ATT)
    cuts = [int(s) for s in np.cumsum(sizes)[:-1]]
    q_r, k_r, v_r, g_r, q_a, k_a, v_a = jnp.split(h @ w_in_l, cuts, axis=-1)
    heads = lambda t, n: t.reshape(B, T, n, -1).transpose(0, 2, 1, 3)
    q_a = q_a.reshape(B, T, H_KV, G_ATT, DH_ATT).transpose(0, 2, 3, 1, 4)
    return (heads(q_r, H_RET), heads(k_r, H_RET), heads(v_r, H_RET), g_r,
            q_a, heads(k_a, H_KV), heads(v_a, H_KV))


def retention_scan(q, k, v, log_gamma, s0):
    B, H, T, DK = q.shape
    DV = v.shape[-1]
    N = T // BLOCK
    q = q.astype(jnp.float32).reshape(B, H, N, BLOCK, DK)
    k = (k.astype(jnp.float32) * DK ** -0.5).reshape(B, H, N, BLOCK, DK)
    v = v.astype(jnp.float32).reshape(B, H, N, BLOCK, DV)
    lg = log_gamma.astype(jnp.float32)
    idx = jnp.arange(BLOCK, dtype=jnp.float32)
    diff = idx[:, None] - idx[None, :]
    intra_decay = jnp.where(diff >= 0, jnp.exp(lg[:, None, None] * jnp.maximum(diff, 0.0)), 0.0)
    scores = jnp.einsum('bhncd,bhnsd->bhncs', q, k) * intra_decay[:, None]
    o_intra = jnp.einsum('bhncs,bhnse->bhnce', scores, v)
    q_dec = jnp.exp(lg[:, None] * (idx + 1.0))
    k_dec = jnp.exp(lg[:, None] * (BLOCK - 1.0 - idx))
    c_dec = jnp.exp(lg * BLOCK)[None, :, None, None]
    kv = jnp.einsum('bhncd,bhnce->nbhde', k * k_dec[:, None, :, None], v)
    qx = jnp.moveaxis(q * q_dec[:, None, :, None], 2, 0)

    def step(s, inp):
        qn, kvn = inp
        return c_dec * s + kvn, jnp.einsum('bhcd,bhde->bhce', qn, s)

    s_fin, o_cross = lax.scan(step, s0.astype(jnp.float32), (qx, kv))
    o = o_intra + jnp.moveaxis(o_cross, 0, 2)
    return o.reshape(B, H, T, DV), s_fin


def bidir_retention(q, k, v, lg_f, lg_b, s0_f, s0_b):
    o_f, s_f = retention_scan(q, k, v, lg_f, s0_f)
    o_b, s_b = retention_scan(jnp.flip(q, 2), jnp.flip(k, 2), jnp.flip(v, 2), lg_b, s0_b)
    return o_f + jnp.flip(o_b, 2), s_f, s_b


def retention_output(o, g, gn_l):
    B, H, T, DV = o.shape
    o = o.transpose(0, 2, 1, 3)
    mu = jnp.mean(o, axis=-1, keepdims=True)
    var = jnp.mean((o - mu) ** 2, axis=-1, keepdims=True)
    on = ((o - mu) * lax.rsqrt(var + EPS)).reshape(B, T, H * DV) * gn_l.astype(jnp.float32)
    return jax.nn.silu(g) * on.astype(g.dtype)


def sink_attend(q_blk, ks, vs, masks, sink):
    scale = DH_ATT ** -0.5
    scores = []
    for k, m in zip(ks, masks):
        s = jnp.einsum('bhgqd,bhkd->bhgqk', q_blk, k).astype(jnp.float32) * scale
        scores.append(s if m is None else jnp.where(m, s, NEG_INF))
    B, _, _, Q, _ = q_blk.shape
    sink_col = jnp.broadcast_to(sink.astype(jnp.float32)[None, :, :, None, None], (B, H_KV, G_ATT, Q, 1))
    p = jax.nn.softmax(jnp.concatenate([sink_col] + scores, axis=-1), axis=-1)
    out = None
    off = 1
    for v, s in zip(vs, scores):
        n = s.shape[-1]
        term = jnp.einsum('bhgqk,bhkd->bhgqd', p[..., off:off + n].astype(v.dtype), v)
        out = term if out is None else out + term
        off += n
    return out


def context_attention(q, k, v, sink):
    B, _, _, S, _ = q.shape

    def blk(i):
        qb = lax.dynamic_slice_in_dim(q, i * BLOCK, BLOCK, axis=3)
        return sink_attend(qb, [k], [v], [None], sink)

    o = lax.map(blk, jnp.arange(S // BLOCK))
    return jnp.moveaxis(o, 0, 3).reshape(B, H_KV, G_ATT, S, DH_ATT)


def latent_attention(q, k, v, ck, cv, sink):
    B, _, _, T, _ = q.shape
    pad = ((0, 0), (0, 0), (WINDOW, WINDOW), (0, 0))
    kp, vp = jnp.pad(k, pad), jnp.pad(v, pad)
    span = BLOCK + 2 * WINDOW
    a = jnp.arange(BLOCK)
    b = jnp.arange(span)
    rel = b[None, :] - a[:, None]

    def blk(i):
        start = i * BLOCK
        qb = lax.dynamic_slice_in_dim(q, start, BLOCK, axis=3)
        kb = lax.dynamic_slice_in_dim(kp, start, span, axis=2)
        vb = lax.dynamic_slice_in_dim(vp, start, span, axis=2)
        key_pos = start - WINDOW + b
        mask = (rel >= 0) & (rel <= 2 * WINDOW) & ((key_pos >= 0) & (key_pos < T))[None, :]
        return sink_attend(qb, [kb, ck], [vb, cv], [mask, None], sink)

    o = lax.map(blk, jnp.arange(T // BLOCK))
    return jnp.moveaxis(o, 0, 3).reshape(B, H_KV, G_ATT, T, DH_ATT)


def axial_angles(T):
    rows = T // GRID_W
    row = jnp.repeat(jnp.arange(rows), GRID_W)
    col = jnp.tile(jnp.arange(GRID_W), rows)
    n = DH_ATT // 4
    inv = ROPE_BASE ** (-jnp.arange(n, dtype=jnp.float32) / n)
    return row[:, None] * inv, col[:, None] * inv


def apply_axial_rope(x, ang_r, ang_c):
    half = x.shape[-1] // 2

    def rot(xh, ang):
        x1, x2 = jnp.split(xh, 2, axis=-1)
        c, s = jnp.cos(ang).astype(x.dtype), jnp.sin(ang).astype(x.dtype)
        return jnp.concatenate([x1 * c - x2 * s, x2 * c + x1 * s], axis=-1)

    return jnp.concatenate([rot(x[..., :half], ang_r), rot(x[..., half:], ang_c)], axis=-1)


def conv_ffn(h, w_up_l, conv_w_l, conv_b_l, w_down_l):
    T = h.shape[1]
    up = jnp.pad(h @ w_up_l, ((0, 0), ((CONV_W - 1) // 2, (CONV_W - 1) // 2), (0, 0)))
    u = conv_b_l + sum(up[:, j:j + T] * conv_w_l[j] for j in range(CONV_W))
    gate, val = jnp.split(u, 2, axis=-1)
    return (jax.nn.silu(gate) * val) @ w_down_l


def trunk_layer(x, mod, w_in_l, w_out_l, log_decay_l, gn_l, sink_l, w_up_l, conv_w_l, conv_b_l,
                w_down_l, ctx_k=None, ctx_v=None, ctx_state=None, rope=None):
    shift1, scale1, gate1, shift2, scale2, gate2 = mod
    B, T, _ = x.shape
    q_r, k_r, v_r, g_r, q_a, k_a, v_a = project(modulate(x, shift1, scale1), w_in_l)
    sink = sink_l.reshape(H_KV, G_ATT)
    if ctx_state is None:
        s0 = jnp.zeros((2, B, H_RET, DK_RET, DV_RET), jnp.float32)
    else:
        s0 = jnp.moveaxis(ctx_state, 1, 0)
    o_r, s_f, s_b = bidir_retention(q_r, k_r, v_r, log_decay_l[0], log_decay_l[1], s0[0], s0[1])
    y_r = retention_output(o_r, g_r, gn_l)
    if ctx_k is None:
        o_a = context_attention(q_a, k_a, v_a, sink)
    else:
        ang_r, ang_c = rope
        q_a = apply_axial_rope(q_a, ang_r, ang_c)
        k_a = apply_axial_rope(k_a, ang_r, ang_c)
        o_a = latent_attention(q_a, k_a, v_a, ctx_k, ctx_v, sink)
    y_a = o_a.transpose(0, 3, 1, 2, 4).reshape(B, T, D_ATT)
    x = x + gate1 * (jnp.concatenate([y_r, y_a], axis=-1) @ w_out_l)
    x = x + gate2 * conv_ffn(modulate(x, shift2, scale2), w_up_l, conv_w_l, conv_b_l, w_down_l)
    return x, k_a, v_a, jnp.stack([s_f, s_b], axis=1)


def setup_inputs(seed: int = 0) -> dict:
    key = jax.random.key(seed)
    ks = jax.random.split(key, 20)
    f32 = jnp.float32
    nrm = lambda k, shape, s: jax.random.normal(k, shape, f32) * s
    base_decay = jnp.log1p(-(2.0 ** (-5.0 - jnp.arange(H_RET, dtype=f32))))
    return {
        "x_prompt": nrm(ks[0], (BATCH, SEQ, D_MODEL), 1.0),
        "x_sample": nrm(ks[1], (DEC_BATCH, DEC_SEQ, D_MODEL), 1.0),
        "cache_k": nrm(ks[2], (DEC_BATCH, DEPTH, H_KV, PAST_LEN, DH_ATT), 1.0),
        "cache_v": nrm(ks[3], (DEC_BATCH, DEPTH, H_KV, PAST_LEN, DH_ATT), 1.0),
        "state_ret": nrm(ks[4], (DEC_BATCH, DEPTH, 2, H_RET, DK_RET, DV_RET), 0.1),
        "c": nrm(ks[5], (DEC_BATCH, D_MODEL), 1.0),
        "c_ctx": nrm(ks[6], (D_MODEL,), 1.0),
        "w_mod": nrm(ks[7], (DEPTH, D_MODEL, N_MOD * D_MODEL), 0.5 * D_MODEL ** -0.5),
        "b_mod": nrm(ks[8], (DEPTH, N_MOD * D_MODEL), 0.02),
        "w_in": nrm(ks[9], (DEPTH, D_MODEL, D_IN), D_MODEL ** -0.5),
        "w_out": nrm(ks[10], (DEPTH, D_MIX, D_MODEL), D_MIX ** -0.5),
        "ret_log_decay": base_decay * (1.0 + nrm(ks[11], (DEPTH, 2, H_RET), 0.05)),
        "ret_gn": 1.0 + nrm(ks[12], (DEPTH, D_RET), 0.02),
        "att_sink": nrm(ks[13], (DEPTH, H_ATT), 0.5),
        "w_up": nrm(ks[14], (DEPTH, D_MODEL, 2 * D_FF), D_MODEL ** -0.5),
        "conv_w": nrm(ks[15], (DEPTH, CONV_W, 2 * D_FF), CONV_W ** -0.5),
        "conv_b": nrm(ks[16], (DEPTH, 2 * D_FF), 0.02),
        "w_down": nrm(ks[17], (DEPTH, D_FF, D_MODEL), D_FF ** -0.5),
        "final_gain": 1.0 + nrm(ks[18], (D_MODEL,), 0.02),
    }


def reference(x_prompt, x_sample, cache_k, cache_v, state_ret, c, c_ctx, w_mod, b_mod, w_in, w_out,
              ret_log_decay, ret_gn, att_sink, w_up, conv_w, conv_b, w_down, final_gain):
    rope = axial_angles(x_sample.shape[1])
    xp, xs = x_prompt, x_sample
    new_k, new_v, new_s = [], [], []
    for l in range(DEPTH):
        weights = (w_in[l], w_out[l], ret_log_decay[l], ret_gn[l], att_sink[l],
                   w_up[l], conv_w[l], conv_b[l], w_down[l])
        xp, k_l, v_l, s_l = trunk_layer(xp, modulation(c_ctx, w_mod[l], b_mod[l]), *weights)
        new_k.append(k_l)
        new_v.append(v_l)
        new_s.append(s_l)
        xs, _, _, _ = trunk_layer(xs, modulation(c, w_mod[l], b_mod[l]), *weights,
                                  ctx_k=cache_k[:, l], ctx_v=cache_v[:, l],
                                  ctx_state=state_ret[:, l], rope=rope)
    y_prompt = rms_norm(xp) * final_gain
    y_sample = rms_norm(xs) * final_gain
    new_cache_k = jnp.stack(new_k, axis=1)
    new_cache_v = jnp.stack(new_v, axis=1)
    new_state_ret = jnp.stack(new_s, axis=1)
    return (y_prompt, y_sample, new_cache_k, new_cache_v, new_state_ret)
```

```python
import functools

import jax
import jax.numpy as jnp
import numpy as np
from jax import lax
from jax.experimental import pallas as pl
from jax.experimental.pallas import tpu as pltpu

F32 = jnp.float32
BF16 = jnp.bfloat16

GRID_W = 64
H_RET = 8
DK_RET = 128
DV_RET = 128
D_RET = H_RET * DV_RET
H_ATT = 8
H_KV = 2
G_ATT = H_ATT // H_KV
DH_ATT = 128
D_ATT = H_ATT * DH_ATT
WINDOW = 128
BLOCK = 128
CONV_W = 3
N_MOD = 6
ROPE_BASE = 10000.0
EPS = 1e-6
NEG_INF = -1e30

OFF_QR = 0
OFF_KR = OFF_QR + H_RET * DK_RET
OFF_VR = OFF_KR + H_RET * DK_RET
OFF_GR = OFF_VR + H_RET * DV_RET
OFF_QA = OFF_GR + H_RET * DV_RET
OFF_KA = OFF_QA + H_ATT * DH_ATT
OFF_VA = OFF_KA + H_KV * DH_ATT
D_IN = OFF_VA + H_KV * DH_ATT

COND_ROWS = 8
TOKEN_TILE = 1024
NORM_ROWS = 256
RET_CHUNK = 256
V7X_VMEM_LIMIT = 56 * 1024 * 1024


def _params(semantics):
    return pltpu.CompilerParams(dimension_semantics=semantics, vmem_limit_bytes=V7X_VMEM_LIMIT)


def _once(block_shape, index_map):
    return pl.BlockSpec(block_shape, index_map, pipeline_mode=pl.Buffered(1))


def _mod_kernel(cond_ref, w_ref, b_ref, o_ref):
    a = jax.nn.silu(cond_ref[...]).astype(BF16)
    o_ref[...] = jnp.dot(a, w_ref[...].astype(BF16), preferred_element_type=F32) + b_ref[...]


def _modulation(cond, w_mod, b_mod):
    depth, d, n = w_mod.shape
    tn = 1024
    return pl.pallas_call(
        _mod_kernel,
        grid=(depth, n // tn),
        in_specs=[
            pl.BlockSpec((COND_ROWS, d), lambda l, j: (0, 0)),
            pl.BlockSpec((None, d, tn), lambda l, j: (l, 0, j)),
            pl.BlockSpec((None, 1, tn), lambda l, j: (l, 0, j)),
        ],
        out_specs=pl.BlockSpec((None, COND_ROWS, tn), lambda l, j: (l, 0, j)),
        out_shape=jax.ShapeDtypeStruct((depth, COND_ROWS, n), F32),
        compiler_params=_params(("arbitrary", "arbitrary")),
        name="modulation",
    )(cond, w_mod, b_mod.reshape(depth, 1, n))


def _mod_spec(layer, chunk, d, row0, row_step):
    return pl.BlockSpec((None, None, 1, d), lambda i, j: (layer, row0 + row_step * i, 0, chunk))


def _modulated_norm_to(h_ref, x_ref, shift_ref, scale_ref):
    rows = x_ref.shape[0]
    one_plus = 1.0 + scale_ref[...]
    shift = shift_ref[...]

    def body(c, carry):
        r = pl.ds(pl.multiple_of(c * NORM_ROWS, NORM_ROWS), NORM_ROWS)
        x = x_ref[r, :]
        inv = lax.rsqrt(jnp.mean(x * x, axis=-1, keepdims=True) + EPS)
        h_ref[r, :] = ((x * inv) * one_plus + shift).astype(h_ref.dtype)
        return carry

    lax.fori_loop(0, rows // NORM_ROWS, body, 0)


def _proj_kernel(x_ref, shift_ref, scale_ref, w_ref, o_ref, h_ref):
    @pl.when(pl.program_id(1) == 0)
    def _():
        _modulated_norm_to(h_ref, x_ref, shift_ref, scale_ref)

    o_ref[...] = jnp.dot(h_ref[...], w_ref[...].astype(BF16), preferred_element_type=F32)


def _project(x, mod4, w_in, layer, row0, row_step):
    n, d = x.shape
    d_in = w_in.shape[-1]
    tm, tn = TOKEN_TILE, 512
    return pl.pallas_call(
        _proj_kernel,
        grid=(n // tm, d_in // tn),
        in_specs=[
            _once((tm, d), lambda i, j: (i, 0)),
            _mod_spec(layer, 0, d, row0, row_step),
            _mod_spec(layer, 1, d, row0, row_step),
            pl.BlockSpec((None, d, tn), lambda i, j: (layer, 0, j)),
        ],
        out_specs=pl.BlockSpec((tm, tn), lambda i, j: (i, j)),
        out_shape=jax.ShapeDtypeStruct((n, d_in), F32),
        scratch_shapes=[pltpu.VMEM((tm, d), BF16)],
        compiler_params=_params(("arbitrary", "arbitrary")),
        name="in_proj",
    )(x, mod4, mod4, w_in)


def _ret_kernel(*refs, seq_len, chunk, has_state, emit_state):
    it = iter(refs)
    ld_ref = next(it)
    q_ref, k_ref, v_ref, g_ref, gn_ref = (next(it) for _ in range(5))
    s0_ref = next(it) if has_state else None
    y_ref = next(it)
    snew_ref = next(it) if emit_state else None
    o_ref = next(it)

    head = pl.program_id(1)
    lg_f = ld_ref[0, head]
    lg_b = ld_ref[1, head]
    n_chunks = seq_len // chunk
    k_scale = DK_RET ** -0.5

    row = lax.broadcasted_iota(jnp.int32, (chunk, chunk), 0).astype(F32)
    col = lax.broadcasted_iota(jnp.int32, (chunk, chunk), 1).astype(F32)
    diff = row - col
    intra = (jnp.where(diff >= 0, jnp.exp(lg_f * jnp.maximum(diff, 0.0)), 0.0)
             + jnp.where(diff <= 0, jnp.exp(lg_b * jnp.maximum(-diff, 0.0)), 0.0)) * k_scale
    pos = lax.broadcasted_iota(jnp.int32, (chunk, DK_RET), 0).astype(F32)
    q_dec_f = jnp.exp(lg_f * (pos + 1.0))
    k_dec_f = jnp.exp(lg_f * (chunk - 1.0 - pos)) * k_scale
    q_dec_b = jnp.exp(lg_b * (chunk - pos))
    k_dec_b = jnp.exp(lg_b * pos) * k_scale
    c_dec_f = jnp.exp(jnp.full((DK_RET, DV_RET), lg_f * chunk, F32))
    c_dec_b = jnp.exp(jnp.full((DK_RET, DV_RET), lg_b * chunk, F32))

    def rows(n):
        return pl.ds(n * chunk, chunk)

    def kv_update(k, dec, v_b):
        kd_t = (k * dec).T.astype(BF16)
        return jnp.dot(kd_t, v_b, preferred_element_type=F32)

    s_f = s0_ref[0] if has_state else jnp.zeros((DK_RET, DV_RET), F32)
    for n in range(n_chunks):
        q = q_ref[rows(n), :]
        k = k_ref[rows(n), :]
        q_b = q.astype(BF16)
        v_b = v_ref[rows(n), :].astype(BF16)
        sc = lax.dot_general(q_b, k.astype(BF16), (((1,), (1,)), ((), ())),
                             preferred_element_type=F32) * intra
        o = jnp.dot(sc.astype(BF16), v_b, preferred_element_type=F32)
        o = o + jnp.dot(q_b, s_f.astype(BF16), preferred_element_type=F32) * q_dec_f
        o_ref[rows(n), :] = o
        s_f = c_dec_f * s_f + kv_update(k, k_dec_f, v_b)

    s_b = s0_ref[1] if has_state else jnp.zeros((DK_RET, DV_RET), F32)
    for n in reversed(range(n_chunks)):
        q_b = q_ref[rows(n), :].astype(BF16)
        k = k_ref[rows(n), :]
        v_b = v_ref[rows(n), :].astype(BF16)
        o_ref[rows(n), :] += jnp.dot(q_b, s_b.astype(BF16), preferred_element_type=F32) * q_dec_b
        s_b = c_dec_b * s_b + kv_update(k, k_dec_b, v_b)

    if emit_state:
        snew_ref[0] = s_f
        snew_ref[1] = s_b

    gn = gn_ref[...]
    for n in range(n_chunks):
        o = o_ref[rows(n), :]
        mu = jnp.mean(o, axis=-1, keepdims=True)
        cen = o - mu
        var = jnp.mean(cen * cen, axis=-1, keepdims=True)
        on = cen * lax.rsqrt(var + EPS) * gn
        y_ref[rows(n), :] = jax.nn.silu(g_ref[rows(n), :]) * on


def _retention(proj, log_decay, gn, seq_len, state=None, layer=None, emit_state=False):
    n = proj.shape[0]
    batch = n // seq_len
    has_state = state is not None
    blk = lambda off: pl.BlockSpec((seq_len, DK_RET), lambda b, h: (b, off // DK_RET + h))
    in_specs = [
        pl.BlockSpec(memory_space=pltpu.SMEM),
        blk(OFF_QR), blk(OFF_KR), blk(OFF_VR), blk(OFF_GR),
        pl.BlockSpec((1, DV_RET), lambda b, h: (0, h)),
    ]
    args = [log_decay, proj, proj, proj, proj, gn.reshape(1, D_RET)]
    if has_state:
        in_specs.append(pl.BlockSpec((None, None, 2, None, DK_RET, DV_RET),
                                     lambda b, h: (b, layer, 0, h, 0, 0)))
        args.append(state)
    out_specs = [pl.BlockSpec((seq_len, DV_RET), lambda b, h: (b, h))]
    out_shape = [jax.ShapeDtypeStruct((n, D_RET), F32)]
    if emit_state:
        out_specs.append(pl.BlockSpec((None, 2, None, DK_RET, DV_RET), lambda b, h: (b, 0, h, 0, 0)))
        out_shape.append(jax.ShapeDtypeStruct((batch, 2, H_RET, DK_RET, DV_RET), F32))
    kern = functools.partial(_ret_kernel, seq_len=seq_len, chunk=min(RET_CHUNK, seq_len),
                             has_state=has_state, emit_state=emit_state)
    return pl.pallas_call(
        kern,
        grid=(batch, H_RET),
        in_specs=in_specs,
        out_specs=out_specs,
        out_shape=out_shape,
        scratch_shapes=[pltpu.VMEM((seq_len, DV_RET), F32)],
        compiler_params=_params(("arbitrary", "arbitrary")),
        name="retention",
    )(*args)


def _softmax_parts(scores, sink):
    m = jnp.maximum(sink, functools.reduce(
        jnp.maximum, [jnp.max(s, axis=-1, keepdims=True) for s in scores]))
    ps = [jnp.exp(s - m) for s in scores]
    denom = jnp.exp(sink - m) + functools.reduce(
        lambda a, b: a + b, [jnp.sum(p, axis=-1, keepdims=True) for p in ps])
    return ps, denom


def _ctx_attn_kernel(sink_ref, q_ref, k_ref, v_ref, o_ref):
    kv_head = pl.program_id(1)
    scale = DH_ATT ** -0.5
    k_b = k_ref[...].astype(BF16)
    v_b = v_ref[...].astype(BF16)
    for g in range(G_ATT):
        cols = slice(g * DH_ATT, (g + 1) * DH_ATT)
        q_b = q_ref[:, cols].astype(BF16)
        s = lax.dot_general(q_b, k_b, (((1,), (1,)), ((), ())), preferred_element_type=F32) * scale
        (p,), denom = _softmax_parts([s], sink_ref[kv_head * G_ATT + g])
        o = jnp.dot(p.astype(BF16), v_b, preferred_element_type=F32)
        o_ref[:, cols] = o / denom


def _context_attention(proj, sink, seq_len):
    n = proj.shape[0]
    qw = G_ATT * DH_ATT
    return pl.pallas_call(
        _ctx_attn_kernel,
        grid=(n // seq_len, H_KV),
        in_specs=[
            pl.BlockSpec(memory_space=pltpu.SMEM),
            pl.BlockSpec((seq_len, qw), lambda b, h: (b, OFF_QA // qw + h)),
            pl.BlockSpec((seq_len, DH_ATT), lambda b, h: (b, OFF_KA // DH_ATT + h)),
            pl.BlockSpec((seq_len, DH_ATT), lambda b, h: (b, OFF_VA // DH_ATT + h)),
        ],
        out_specs=pl.BlockSpec((seq_len, qw), lambda b, h: (b, h)),
        out_shape=jax.ShapeDtypeStruct((n, D_ATT), F32),
        compiler_params=_params(("arbitrary", "arbitrary")),
        name="context_attention",
    )(sink, proj, proj, proj)


def _rope(x, cos, sin_signed):
    quarter = DH_ATT // 4
    lane = lax.broadcasted_iota(jnp.int32, x.shape, 1)
    first = (lane % (2 * quarter)) < quarter
    partner = jnp.where(first, pltpu.roll(x, DH_ATT - quarter, axis=1), pltpu.roll(x, quarter, axis=1))
    return x * cos + partner * sin_signed


def _lat_attn_kernel(sink_ref, q_ref, k_ref, v_ref, ck_ref, cv_ref, cos_ref, sin_ref, o_ref,
                     kr_ref, *, seq_len):
    kv_head = pl.program_id(1)
    qb = pl.program_id(2)
    scale = DH_ATT ** -0.5
    span = BLOCK + 2 * WINDOW

    @pl.when(qb == 0)
    def _():
        kr_ref[...] = _rope(k_ref[...], cos_ref[...], sin_ref[...]).astype(BF16)

    q_rows = pl.ds(pl.multiple_of(qb * BLOCK, BLOCK), BLOCK)
    cos_q = cos_ref[q_rows, :]
    sin_q = sin_ref[q_rows, :]
    q4 = jnp.concatenate(
        [_rope(q_ref[:, g * DH_ATT:(g + 1) * DH_ATT], cos_q, sin_q) for g in range(G_ATT)],
        axis=0).astype(BF16)

    start = pl.multiple_of(jnp.clip(qb * BLOCK - WINDOW, 0, seq_len - span), BLOCK)
    k_win = kr_ref[pl.ds(start, span), :]
    v_win = v_ref[pl.ds(start, span), :].astype(BF16)
    q_pos = qb * BLOCK + lax.broadcasted_iota(jnp.int32, (G_ATT * BLOCK, span), 0) % BLOCK
    k_pos = start + lax.broadcasted_iota(jnp.int32, (G_ATT * BLOCK, span), 1)
    band4 = jnp.abs(k_pos - q_pos) <= WINDOW

    nt = (((1,), (1,)), ((), ()))
    s_loc = lax.dot_general(q4, k_win, nt, preferred_element_type=F32) * scale
    s_loc = jnp.where(band4, s_loc, NEG_INF)
    s_ctx = lax.dot_general(q4, ck_ref[...].astype(BF16), nt, preferred_element_type=F32) * scale
    row_head = lax.broadcasted_iota(jnp.int32, (G_ATT * BLOCK, 1), 0) // BLOCK
    sink4 = jnp.full((G_ATT * BLOCK, 1), sink_ref[kv_head * G_ATT], F32)
    for g in range(1, G_ATT):
        sink4 = jnp.where(row_head == g, sink_ref[kv_head * G_ATT + g], sink4)
    (p_loc, p_ctx), denom = _softmax_parts([s_loc, s_ctx], sink4)
    o4 = (jnp.dot(p_loc.astype(BF16), v_win, preferred_element_type=F32)
          + jnp.dot(p_ctx.astype(BF16), cv_ref[...].astype(BF16), preferred_element_type=F32)) / denom
    for g in range(G_ATT):
        o_ref[:, g * DH_ATT:(g + 1) * DH_ATT] = o4[g * BLOCK:(g + 1) * BLOCK, :]


def _latent_attention(proj, sink, cache_k, cache_v, layer, seq_len, cos, sin_signed):
    n = proj.shape[0]
    batch = n // seq_len
    nq = seq_len // BLOCK
    qw = G_ATT * DH_ATT
    past = cache_k.shape[3]
    cache_spec = pl.BlockSpec((None, None, None, past, DH_ATT), lambda b, h, i: (b, layer, h, 0, 0))
    return pl.pallas_call(
        functools.partial(_lat_attn_kernel, seq_len=seq_len),
        grid=(batch, H_KV, nq),
        in_specs=[
            pl.BlockSpec(memory_space=pltpu.SMEM),
            pl.BlockSpec((BLOCK, qw), lambda b, h, i: (b * nq + i, OFF_QA // qw + h)),
            pl.BlockSpec((seq_len, DH_ATT), lambda b, h, i: (b, OFF_KA // DH_ATT + h)),
            pl.BlockSpec((seq_len, DH_ATT), lambda b, h, i: (b, OFF_VA // DH_ATT + h)),
            cache_spec, cache_spec,
            pl.BlockSpec((seq_len, DH_ATT), lambda b, h, i: (0, 0)),
            pl.BlockSpec((seq_len, DH_ATT), lambda b, h, i: (0, 0)),
        ],
        out_specs=pl.BlockSpec((BLOCK, qw), lambda b, h, i: (b * nq + i, h)),
        out_shape=jax.ShapeDtypeStruct((n, D_ATT), F32),
        scratch_shapes=[pltpu.VMEM((seq_len, DH_ATT), BF16)],
        compiler_params=_params(("arbitrary", "arbitrary", "arbitrary")),
        name="latent_attention",
    )(sink, proj, proj, proj, cache_k, cache_v, cos, sin_signed)


def _rope_tables(seq_len):
    rows = seq_len // GRID_W
    row = jnp.repeat(jnp.arange(rows), GRID_W)
    col = jnp.tile(jnp.arange(GRID_W), rows)
    quarter = DH_ATT // 4
    inv = ROPE_BASE ** (-jnp.arange(quarter, dtype=F32) / quarter)
    ang_r, ang_c = row[:, None] * inv, col[:, None] * inv
    cos = jnp.concatenate([jnp.cos(ang_r)] * 2 + [jnp.cos(ang_c)] * 2, axis=-1)
    sin_signed = jnp.concatenate([-jnp.sin(ang_r), jnp.sin(ang_r), -jnp.sin(ang_c), jnp.sin(ang_c)], axis=-1)
    return cos.astype(F32), sin_signed.astype(F32)


def _out_proj_kernel(yr_ref, ya_ref, w_ref, x_ref, gate_ref, o_ref, y_ref):
    @pl.when(pl.program_id(1) == 0)
    def _():
        y_ref[:, :D_RET] = yr_ref[...].astype(BF16)
        y_ref[:, D_RET:] = ya_ref[...].astype(BF16)

    mix = jnp.dot(y_ref[...], w_ref[...].astype(BF16), preferred_element_type=F32)
    o_ref[...] = x_ref[...] + gate_ref[...] * mix


def _out_project(y_r, y_a, x, mod4, w_out, layer, row0, row_step):
    n, d = x.shape
    d_mix = w_out.shape[1]
    tm, tn = TOKEN_TILE, 512
    return pl.pallas_call(
        _out_proj_kernel,
        grid=(n // tm, d // tn),
        in_specs=[
            _once((tm, D_RET), lambda i, j: (i, 0)),
            _once((tm, D_ATT), lambda i, j: (i, 0)),
            pl.BlockSpec((None, d_mix, tn), lambda i, j: (layer, 0, j)),
            pl.BlockSpec((tm, tn), lambda i, j: (i, j)),
            pl.BlockSpec((None, None, 1, tn), lambda i, j: (layer, row0 + row_step * i, 0, 2 * (d // tn) + j)),
        ],
        out_specs=pl.BlockSpec((tm, tn), lambda i, j: (i, j)),
        out_shape=jax.ShapeDtypeStruct((n, d), F32),
        scratch_shapes=[pltpu.VMEM((tm, d_mix), BF16)],
        compiler_params=_params(("arbitrary", "arbitrary")),
        name="out_proj",
    )(y_r, y_a, w_out, x, mod4)


def _ffn_kernel(x_ref, shift_ref, scale_ref, gate_ref, wg_ref, wv_ref, cwg_ref, cwv_ref,
                cbg_ref, cbv_ref, wd_ref, o_ref, h_ref, *, seq_len):
    j = pl.program_id(1)
    rows = x_ref.shape[0]

    @pl.when(j == 0)
    def _():
        _modulated_norm_to(h_ref, x_ref, shift_ref, scale_ref)
        o_ref[...] = jnp.zeros_like(o_ref)

    h = h_ref[...]

    def conv_branch(w_ref, cw_ref, cb_ref):
        up = jnp.dot(h, w_ref[...].astype(BF16), preferred_element_type=F32)
        t = lax.broadcasted_iota(jnp.int32, up.shape, 0) % seq_len
        prev = jnp.where(t == 0, 0.0, pltpu.roll(up, 1, axis=0))
        nxt = jnp.where(t == seq_len - 1, 0.0, pltpu.roll(up, rows - 1, axis=0))
        return cb_ref[...] + prev * cw_ref[0:1, :] + up * cw_ref[1:2, :] + nxt * cw_ref[2:3, :]

    act = jax.nn.silu(conv_branch(wg_ref, cwg_ref, cbg_ref)) * conv_branch(wv_ref, cwv_ref, cbv_ref)
    o_ref[...] += jnp.dot(act.astype(BF16), wd_ref[...].astype(BF16), preferred_element_type=F32)

    @pl.when(j == pl.num_programs(1) - 1)
    def _():
        o_ref[...] = x_ref[...] + gate_ref[...] * o_ref[...]


def _conv_ffn(x, mod4, w_up, conv_w, conv_b, w_down, layer, row0, row_step, seq_len):
    n, d = x.shape
    d_ff = w_down.shape[1]
    tm, tf = TOKEN_TILE, 512
    nf = d_ff // tf
    conv_b3 = conv_b.reshape(conv_b.shape[0], 1, 2 * d_ff)
    return pl.pallas_call(
        functools.partial(_ffn_kernel, seq_len=seq_len),
        grid=(n // tm, nf),
        in_specs=[
            _once((tm, d), lambda i, j: (i, 0)),
            _mod_spec(layer, 3, d, row0, row_step),
            _mod_spec(layer, 4, d, row0, row_step),
            _mod_spec(layer, 5, d, row0, row_step),
            pl.BlockSpec((None, d, tf), lambda i, j: (layer, 0, j)),
            pl.BlockSpec((None, d, tf), lambda i, j: (layer, 0, nf + j)),
            pl.BlockSpec((None, CONV_W, tf), lambda i, j: (layer, 0, j)),
            pl.BlockSpec((None, CONV_W, tf), lambda i, j: (layer, 0, nf + j)),
            pl.BlockSpec((None, 1, tf), lambda i, j: (layer, 0, j)),
            pl.BlockSpec((None, 1, tf), lambda i, j: (layer, 0, nf + j)),
            pl.BlockSpec((None, tf, d), lambda i, j: (layer, j, 0)),
        ],
        out_specs=_once((tm, d), lambda i, j: (i, 0)),
        out_shape=jax.ShapeDtypeStruct((n, d), F32),
        scratch_shapes=[pltpu.VMEM((tm, d), BF16)],
        compiler_params=_params(("arbitrary", "arbitrary")),
        name="conv_ffn",
    )(x, mod4, mod4, mod4, w_up, w_up, conv_w, conv_w, conv_b3, conv_b3, w_down)


def _final_norm_kernel(x_ref, gain_ref, o_ref):
    x = x_ref[...]
    o_ref[...] = x * lax.rsqrt(jnp.mean(x * x, axis=-1, keepdims=True) + EPS) * gain_ref[...]


def _final_norm(x, gain):
    n, d = x.shape
    tm = 256
    return pl.pallas_call(
        _final_norm_kernel,
        grid=(n // tm,),
        in_specs=[pl.BlockSpec((tm, d), lambda i: (i, 0)), pl.BlockSpec((1, d), lambda i: (0, 0))],
        out_specs=pl.BlockSpec((tm, d), lambda i: (i, 0)),
        out_shape=jax.ShapeDtypeStruct((n, d), F32),
        compiler_params=_params(("arbitrary",)),
        name="final_norm",
    )(x, gain.reshape(1, d))


def kernel(x_prompt, x_sample, cache_k, cache_v, state_ret, c, c_ctx, w_mod, b_mod, w_in, w_out,
           ret_log_decay, ret_gn, att_sink, w_up, conv_w, conv_b, w_down, final_gain):
    batch, seq, d = x_prompt.shape
    dec_batch, dec_seq, _ = x_sample.shape
    depth = w_in.shape[0]
    assert TOKEN_TILE % seq == 0 and dec_seq == TOKEN_TILE and 1 + dec_batch <= COND_ROWS
    assert w_in.shape[-1] == D_IN and d == D_RET + D_ATT

    cond = jnp.zeros((COND_ROWS, d), F32).at[0].set(c_ctx).at[1:1 + dec_batch].set(c)
    mod = _modulation(cond, w_mod, b_mod)
    mod4 = mod.reshape(depth, COND_ROWS, 1, N_MOD * d)
    cos, sin_signed = _rope_tables(dec_seq)

    xp = x_prompt.reshape(batch * seq, d)
    xs = x_sample.reshape(dec_batch * dec_seq, d)
    new_k, new_v, new_s = [], [], []
    for l in range(depth):
        proj = _project(xp, mod4, w_in, l, 0, 0)
        y_r, s_l = _retention(proj, ret_log_decay[l], ret_gn[l], seq, emit_state=True)
        y_a = _context_attention(proj, att_sink[l], seq)
        xp = _out_project(y_r, y_a, xp, mod4, w_out, l, 0, 0)
        xp = _conv_ffn(xp, mod4, w_up, conv_w, conv_b, w_down, l, 0, 0, seq)
        kv = proj[:, OFF_KA:].reshape(batch, seq, 2, H_KV, DH_ATT)
        new_k.append(kv[:, :, 0].transpose(0, 2, 1, 3))
        new_v.append(kv[:, :, 1].transpose(0, 2, 1, 3))
        new_s.append(s_l)
        proj = _project(xs, mod4, w_in, l, 1, 1)
        (y_r,) = _retention(proj, ret_log_decay[l], ret_gn[l], dec_seq, state=state_ret, layer=l)
        y_a = _latent_attention(proj, att_sink[l], cache_k, cache_v, l, dec_seq, cos, sin_signed)
        xs = _out_project(y_r, y_a, xs, mod4, w_out, l, 1, 1)
        xs = _conv_ffn(xs, mod4, w_up, conv_w, conv_b, w_down, l, 1, 1, dec_seq)

    y_prompt = _final_norm(xp, final_gain).reshape(batch, seq, d)
    y_sample = _final_norm(xs, final_gain).reshape(dec_batch, dec_seq, d)
    return (y_prompt, y_sample, jnp.stack(new_k, axis=1), jnp.stack(new_v, axis=1),
            jnp.stack(new_s, axis=1))
```

```python
import functools

import jax
import jax.numpy as jnp
from jax import lax
from jax.experimental import pallas as pl
from jax.experimental.pallas import tpu as pltpu

F32 = jnp.float32
BF16 = jnp.bfloat16

GRID_W = 64
H_RET = 8
DK_RET = 128
DV_RET = 128
D_RET = H_RET * DV_RET
H_ATT = 8
H_KV = 2
G_ATT = H_ATT // H_KV
DH_ATT = 128
D_ATT = H_ATT * DH_ATT
WINDOW = 128
BLOCK = 128
CONV_W = 3
N_MOD = 6
ROPE_BASE = 10000.0
EPS = 1e-6
NEG_INF = -1e30

OFF_QR = 0
OFF_KR = OFF_QR + H_RET * DK_RET
OFF_VR = OFF_KR + H_RET * DK_RET
OFF_GR = OFF_VR + H_RET * DV_RET
OFF_QA = OFF_GR + H_RET * DV_RET
OFF_KA = OFF_QA + H_ATT * DH_ATT
OFF_VA = OFF_KA + H_KV * DH_ATT
D_IN = OFF_VA + H_KV * DH_ATT
D_KV = 2 * H_KV * DH_ATT

COND_ROWS = 8
TOKEN_TILE = 1024
PROJ_COLS = 512
FFN_COLS = 512
NORM_ROWS = 256
RET_CHUNK = 256
CAST_BLOCK_BYTES = 8 * 1024 * 1024
V7X_VMEM_LIMIT = 58 * 1024 * 1024


def _params(semantics):
    return pltpu.CompilerParams(dimension_semantics=semantics, vmem_limit_bytes=V7X_VMEM_LIMIT)


def _cast_kernel(w_ref, o_ref):
    o_ref[...] = w_ref[...].astype(o_ref.dtype)


def _to_bf16(w):
    depth, k, n = w.shape
    rows = max(16, min(k, CAST_BLOCK_BYTES // (4 * n) // 16 * 16))
    while k % rows:
        rows -= 16
    return pl.pallas_call(
        _cast_kernel,
        grid=(depth, k // rows),
        in_specs=[pl.BlockSpec((None, rows, n), lambda l, i: (l, i, 0))],
        out_specs=pl.BlockSpec((None, rows, n), lambda l, i: (l, i, 0)),
        out_shape=jax.ShapeDtypeStruct(w.shape, BF16),
        compiler_params=_params(("arbitrary", "arbitrary")),
        name="weight_to_bf16",
    )(w)


def _mod_kernel(cond_ref, w_ref, b_ref, o_ref):
    a = jax.nn.silu(cond_ref[...]).astype(BF16)
    o_ref[...] = jnp.dot(a, w_ref[...].astype(BF16), preferred_element_type=F32) + b_ref[...]


def _modulation(cond, w_mod, b_mod):
    depth, d, n = w_mod.shape
    tn = 1024
    return pl.pallas_call(
        _mod_kernel,
        grid=(depth, n // tn),
        in_specs=[
            pl.BlockSpec((COND_ROWS, d), lambda l, j: (0, 0)),
            pl.BlockSpec((None, d, tn), lambda l, j: (l, 0, j)),
            pl.BlockSpec((None, 1, tn), lambda l, j: (l, 0, j)),
        ],
        out_specs=pl.BlockSpec((None, COND_ROWS, tn), lambda l, j: (l, 0, j)),
        out_shape=jax.ShapeDtypeStruct((depth, COND_ROWS, n), F32),
        compiler_params=_params(("arbitrary", "arbitrary")),
        name="modulation",
    )(cond, w_mod, b_mod.reshape(depth, 1, n))


def _mod_spec(layer, chunk, d, row0, row_step):
    return pl.BlockSpec((None, None, 1, d), lambda i, j: (layer, row0 + row_step * i, 0, chunk))


def _modulated_norm_to(h_ref, x_ref, shift_ref, scale_ref):
    rows = x_ref.shape[0]
    one_plus = 1.0 + scale_ref[...]
    shift = shift_ref[...]

    def body(c, carry):
        r = pl.ds(pl.multiple_of(c * NORM_ROWS, NORM_ROWS), NORM_ROWS)
        x = x_ref[r, :]
        inv = lax.rsqrt(jnp.mean(x * x, axis=-1, keepdims=True) + EPS)
        h_ref[r, :] = ((x * inv) * one_plus + shift).astype(h_ref.dtype)
        return carry

    lax.fori_loop(0, rows // NORM_ROWS, body, 0)


def _proj_kernel(x_ref, shift_ref, scale_ref, w_ref, o_ref, *rest, kv_step):
    kv_ref = rest[0] if kv_step is not None else None
    h_ref = rest[-1]
    j = pl.program_id(1)

    @pl.when(j == 0)
    def _():
        _modulated_norm_to(h_ref, x_ref, shift_ref, scale_ref)

    res = jnp.dot(h_ref[...], w_ref[...], preferred_element_type=F32)
    o_ref[...] = res.astype(o_ref.dtype)
    if kv_step is not None:
        @pl.when(j == kv_step)
        def _():
            kv_ref[...] = res[:, res.shape[1] - D_KV:]


def _project(x, mod4, w_in, layer, row0, row_step, emit_kv):
    n, d = x.shape
    d_in = w_in.shape[-1]
    tm, tn = TOKEN_TILE, PROJ_COLS
    assert tn >= D_KV and d_in % tn == 0
    n_steps = d_in // tn
    out_specs = [pl.BlockSpec((tm, tn), lambda i, j: (i, j))]
    out_shape = [jax.ShapeDtypeStruct((n, d_in), BF16)]
    if emit_kv:
        out_specs.append(pl.BlockSpec((tm, D_KV), lambda i, j: (i, 0)))
        out_shape.append(jax.ShapeDtypeStruct((n, D_KV), F32))
    return pl.pallas_call(
        functools.partial(_proj_kernel, kv_step=n_steps - 1 if emit_kv else None),
        grid=(n // tm, n_steps),
        in_specs=[
            pl.BlockSpec((tm, d), lambda i, j: (i, 0)),
            _mod_spec(layer, 0, d, row0, row_step),
            _mod_spec(layer, 1, d, row0, row_step),
            pl.BlockSpec((None, d, tn), lambda i, j: (layer, 0, j)),
        ],
        out_specs=out_specs,
        out_shape=out_shape,
        scratch_shapes=[pltpu.VMEM((tm, d), BF16)],
        compiler_params=_params(("arbitrary", "arbitrary")),
        name="in_proj",
    )(x, mod4, mod4, w_in)


def _ret_kernel(*refs, seq_len, chunk, has_state, emit_state):
    it = iter(refs)
    ld_ref = next(it)
    q_ref, k_ref, v_ref, g_ref, gn_ref = (next(it) for _ in range(5))
    s0_ref = next(it) if has_state else None
    y_ref = next(it)
    snew_ref = next(it) if emit_state else None
    o_ref, intra_ref, qdf_ref, kdf_ref, qdb_ref, kdb_ref = (next(it) for _ in range(6))

    n_chunks = seq_len // chunk
    k_scale = DK_RET ** -0.5

    @pl.when(pl.program_id(0) == 0)
    def _():
        row = lax.broadcasted_iota(jnp.int32, (chunk, chunk), 0).astype(F32)
        col = lax.broadcasted_iota(jnp.int32, (chunk, chunk), 1).astype(F32)
        diff = row - col
        pos = lax.broadcasted_iota(jnp.int32, (chunk, DK_RET), 0).astype(F32)
        for h in range(H_RET):
            lg_f = ld_ref[0, h]
            lg_b = ld_ref[1, h]
            intra_ref[h] = (jnp.where(diff >= 0, jnp.exp(lg_f * jnp.maximum(diff, 0.0)), 0.0)
                            + jnp.where(diff <= 0, jnp.exp(lg_b * jnp.maximum(-diff, 0.0)), 0.0)) * k_scale
            qdf_ref[h] = jnp.exp(lg_f * (pos + 1.0))
            kdf_ref[h] = jnp.exp(lg_f * (chunk - 1.0 - pos)) * k_scale
            qdb_ref[h] = jnp.exp(lg_b * (chunk - pos))
            kdb_ref[h] = jnp.exp(lg_b * pos) * k_scale

    def rows(n):
        return pl.ds(n * chunk, chunk)

    def kv_update(k_b, dec, v_b):
        kd = (k_b.astype(F32) * dec).T.astype(BF16)
        return jnp.dot(kd, v_b, preferred_element_type=F32)

    for h in range(H_RET):
        cols = slice(h * DK_RET, (h + 1) * DK_RET)
        c_dec_f = jnp.exp(jnp.full((DK_RET, DV_RET), ld_ref[0, h] * chunk, F32))
        c_dec_b = jnp.exp(jnp.full((DK_RET, DV_RET), ld_ref[1, h] * chunk, F32))

        s_f = s0_ref[0, h] if has_state else jnp.zeros((DK_RET, DV_RET), F32)
        for n in range(n_chunks):
            q_b = q_ref[rows(n), cols]
            k_b = k_ref[rows(n), cols]
            v_b = v_ref[rows(n), cols]
            sc = lax.dot_general(q_b, k_b, (((1,), (1,)), ((), ())),
                                 preferred_element_type=F32) * intra_ref[h]
            o = jnp.dot(sc.astype(BF16), v_b, preferred_element_type=F32)
            if has_state or n > 0:
                o = o + jnp.dot(q_b, s_f.astype(BF16), preferred_element_type=F32) * qdf_ref[h]
            o_ref[rows(n), :] = o
            s_f = c_dec_f * s_f + kv_update(k_b, kdf_ref[h], v_b)

        s_b = s0_ref[1, h] if has_state else jnp.zeros((DK_RET, DV_RET), F32)
        for n in reversed(range(n_chunks)):
            q_b = q_ref[rows(n), cols]
            k_b = k_ref[rows(n), cols]
            v_b = v_ref[rows(n), cols]
            if has_state or n < n_chunks - 1:
                o_ref[rows(n), :] += jnp.dot(q_b, s_b.astype(BF16), preferred_element_type=F32) * qdb_ref[h]
            s_b = c_dec_b * s_b + kv_update(k_b, kdb_ref[h], v_b)

        if emit_state:
            snew_ref[0, h] = s_f
            snew_ref[1, h] = s_b

        gn = gn_ref[:, cols]
        for n in range(n_chunks):
            o = o_ref[rows(n), :]
            mu = jnp.mean(o, axis=-1, keepdims=True)
            cen = o - mu
            var = jnp.mean(cen * cen, axis=-1, keepdims=True)
            on = cen * lax.rsqrt(var + EPS) * gn
            y_ref[rows(n), cols] = (jax.nn.silu(g_ref[rows(n), cols].astype(F32)) * on).astype(y_ref.dtype)


def _retention(proj, log_decay, gn, seq_len, state=None, layer=None, emit_state=False):
    n = proj.shape[0]
    batch = n // seq_len
    has_state = state is not None
    chunk = min(RET_CHUNK, seq_len)
    blk = lambda off: pl.BlockSpec((seq_len, D_RET), lambda b: (b, off // D_RET))
    in_specs = [
        pl.BlockSpec(memory_space=pltpu.SMEM),
        blk(OFF_QR), blk(OFF_KR), blk(OFF_VR), blk(OFF_GR),
        pl.BlockSpec((1, D_RET), lambda b: (0, 0)),
    ]
    args = [log_decay, proj, proj, proj, proj, gn.reshape(1, D_RET)]
    if has_state:
        in_specs.append(pl.BlockSpec((None, None, 2, H_RET, DK_RET, DV_RET),
                                     lambda b: (b, layer, 0, 0, 0, 0)))
        args.append(state)
    out_specs = [pl.BlockSpec((seq_len, D_RET), lambda b: (b, 0))]
    out_shape = [jax.ShapeDtypeStruct((n, D_RET), BF16)]
    if emit_state:
        out_specs.append(pl.BlockSpec((None, 2, H_RET, DK_RET, DV_RET), lambda b: (b, 0, 0, 0, 0)))
        out_shape.append(jax.ShapeDtypeStruct((batch, 2, H_RET, DK_RET, DV_RET), F32))
    kern = functools.partial(_ret_kernel, seq_len=seq_len, chunk=chunk,
                             has_state=has_state, emit_state=emit_state)
    return pl.pallas_call(
        kern,
        grid=(batch,),
        in_specs=in_specs,
        out_specs=out_specs,
        out_shape=out_shape,
        scratch_shapes=[
            pltpu.VMEM((seq_len, DV_RET), F32),
            pltpu.VMEM((H_RET, chunk, chunk), F32),
            pltpu.VMEM((H_RET, chunk, DK_RET), F32),
            pltpu.VMEM((H_RET, chunk, DK_RET), F32),
            pltpu.VMEM((H_RET, chunk, DK_RET), F32),
            pltpu.VMEM((H_RET, chunk, DK_RET), F32),
        ],
        compiler_params=_params(("arbitrary",)),
        name="retention",
    )(*args)


def _softmax_parts(scores, sink):
    m = jnp.maximum(sink, functools.reduce(
        jnp.maximum, [jnp.max(s, axis=-1, keepdims=True) for s in scores]))
    ps = [jnp.exp(s - m) for s in scores]
    denom = jnp.exp(sink - m) + functools.reduce(
        lambda a, b: a + b, [jnp.sum(p, axis=-1, keepdims=True) for p in ps])
    return ps, denom


def _ctx_attn_kernel(sink_ref, q_ref, k_ref, v_ref, o_ref):
    kv_head = pl.program_id(1)
    scale = DH_ATT ** -0.5
    k_b = k_ref[...]
    v_b = v_ref[...]
    for g in range(G_ATT):
        cols = slice(g * DH_ATT, (g + 1) * DH_ATT)
        s = lax.dot_general(q_ref[:, cols], k_b, (((1,), (1,)), ((), ())), preferred_element_type=F32) * scale
        (p,), denom = _softmax_parts([s], sink_ref[kv_head * G_ATT + g])
        o = jnp.dot(p.astype(BF16), v_b, preferred_element_type=F32)
        o_ref[:, cols] = (o / denom).astype(o_ref.dtype)


def _context_attention(proj, sink, seq_len):
    n = proj.shape[0]
    qw = G_ATT * DH_ATT
    return pl.pallas_call(
        _ctx_attn_kernel,
        grid=(n // seq_len, H_KV),
        in_specs=[
            pl.BlockSpec(memory_space=pltpu.SMEM),
            pl.BlockSpec((seq_len, qw), lambda b, h: (b, OFF_QA // qw + h)),
            pl.BlockSpec((seq_len, DH_ATT), lambda b, h: (b, OFF_KA // DH_ATT + h)),
            pl.BlockSpec((seq_len, DH_ATT), lambda b, h: (b, OFF_VA // DH_ATT + h)),
        ],
        out_specs=pl.BlockSpec((seq_len, qw), lambda b, h: (b, h)),
        out_shape=jax.ShapeDtypeStruct((n, D_ATT), BF16),
        compiler_params=_params(("arbitrary", "arbitrary")),
        name="context_attention",
    )(sink, proj, proj, proj)


def _rope(x, cos, sin_signed):
    quarter = DH_ATT // 4
    lane = lax.broadcasted_iota(jnp.int32, x.shape, 1)
    first = (lane % (2 * quarter)) < quarter
    partner = jnp.where(first, pltpu.roll(x, DH_ATT - quarter, axis=1), pltpu.roll(x, quarter, axis=1))
    return x * cos + partner * sin_signed


def _lat_attn_kernel(sink_ref, q_ref, k_ref, v_ref, ck_ref, cv_ref, cos_ref, sin_ref, o_ref,
                     kr_ref, *, seq_len):
    kv_head = pl.program_id(1)
    qb = pl.program_id(2)
    scale = DH_ATT ** -0.5
    span = BLOCK + 2 * WINDOW

    @pl.when(qb == 0)
    def _():
        kr_ref[...] = _rope(k_ref[...].astype(F32), cos_ref[...], sin_ref[...]).astype(BF16)

    q_rows = pl.ds(pl.multiple_of(qb * BLOCK, BLOCK), BLOCK)
    cos_q = cos_ref[q_rows, :]
    sin_q = sin_ref[q_rows, :]
    q4 = jnp.concatenate(
        [_rope(q_ref[:, g * DH_ATT:(g + 1) * DH_ATT].astype(F32), cos_q, sin_q) for g in range(G_ATT)],
        axis=0).astype(BF16)

    start = pl.multiple_of(jnp.clip(qb * BLOCK - WINDOW, 0, seq_len - span), BLOCK)
    k_win = kr_ref[pl.ds(start, span), :]
    v_win = v_ref[pl.ds(start, span), :]
    q_pos = qb * BLOCK + lax.broadcasted_iota(jnp.int32, (G_ATT * BLOCK, span), 0) % BLOCK
    k_pos = start + lax.broadcasted_iota(jnp.int32, (G_ATT * BLOCK, span), 1)
    band4 = jnp.abs(k_pos - q_pos) <= WINDOW

    nt = (((1,), (1,)), ((), ()))
    s_loc = lax.dot_general(q4, k_win, nt, preferred_element_type=F32) * scale
    s_loc = jnp.where(band4, s_loc, NEG_INF)
    s_ctx = lax.dot_general(q4, ck_ref[...].astype(BF16), nt, preferred_element_type=F32) * scale
    row_head = lax.broadcasted_iota(jnp.int32, (G_ATT * BLOCK, 1), 0) // BLOCK
    sink4 = jnp.full((G_ATT * BLOCK, 1), sink_ref[kv_head * G_ATT], F32)
    for g in range(1, G_ATT):
        sink4 = jnp.where(row_head == g, sink_ref[kv_head * G_ATT + g], sink4)
    (p_loc, p_ctx), denom = _softmax_parts([s_loc, s_ctx], sink4)
    o4 = (jnp.dot(p_loc.astype(BF16), v_win, preferred_element_type=F32)
          + jnp.dot(p_ctx.astype(BF16), cv_ref[...].astype(BF16), preferred_element_type=F32)) / denom
    for g in range(G_ATT):
        o_ref[:, g * DH_ATT:(g + 1) * DH_ATT] = o4[g * BLOCK:(g + 1) * BLOCK, :].astype(o_ref.dtype)


def _latent_attention(proj, sink, cache_k, cache_v, layer, seq_len, cos, sin_signed):
    n = proj.shape[0]
    batch = n // seq_len
    nq = seq_len // BLOCK
    qw = G_ATT * DH_ATT
    past = cache_k.shape[3]
    cache_spec = pl.BlockSpec((None, None, None, past, DH_ATT), lambda b, h, i: (b, layer, h, 0, 0))
    return pl.pallas_call(
        functools.partial(_lat_attn_kernel, seq_len=seq_len),
        grid=(batch, H_KV, nq),
        in_specs=[
            pl.BlockSpec(memory_space=pltpu.SMEM),
            pl.BlockSpec((BLOCK, qw), lambda b, h, i: (b * nq + i, OFF_QA // qw + h)),
            pl.BlockSpec((seq_len, DH_ATT), lambda b, h, i: (b, OFF_KA // DH_ATT + h)),
            pl.BlockSpec((seq_len, DH_ATT), lambda b, h, i: (b, OFF_VA // DH_ATT + h)),
            cache_spec, cache_spec,
            pl.BlockSpec((seq_len, DH_ATT), lambda b, h, i: (0, 0)),
            pl.BlockSpec((seq_len, DH_ATT), lambda b, h, i: (0, 0)),
        ],
        out_specs=pl.BlockSpec((BLOCK, qw), lambda b, h, i: (b * nq + i, h)),
        out_shape=jax.ShapeDtypeStruct((n, D_ATT), BF16),
        scratch_shapes=[pltpu.VMEM((seq_len, DH_ATT), BF16)],
        compiler_params=_params(("arbitrary", "arbitrary", "arbitrary")),
        name="latent_attention",
    )(sink, proj, proj, proj, cache_k, cache_v, cos, sin_signed)


def _rope_tables(seq_len):
    rows = seq_len // GRID_W
    row = jnp.repeat(jnp.arange(rows), GRID_W)
    col = jnp.tile(jnp.arange(GRID_W), rows)
    quarter = DH_ATT // 4
    inv = ROPE_BASE ** (-jnp.arange(quarter, dtype=F32) / quarter)
    ang_r, ang_c = row[:, None] * inv, col[:, None] * inv
    cos = jnp.concatenate([jnp.cos(ang_r)] * 2 + [jnp.cos(ang_c)] * 2, axis=-1)
    sin_signed = jnp.concatenate([-jnp.sin(ang_r), jnp.sin(ang_r), -jnp.sin(ang_c), jnp.sin(ang_c)], axis=-1)
    return cos.astype(F32), sin_signed.astype(F32)


def _out_proj_kernel(yr_ref, ya_ref, w_ref, x_ref, gate_ref, o_ref, y_ref):
    @pl.when(pl.program_id(1) == 0)
    def _():
        y_ref[:, :D_RET] = yr_ref[...]
        y_ref[:, D_RET:] = ya_ref[...]

    mix = jnp.dot(y_ref[...], w_ref[...], preferred_element_type=F32)
    o_ref[...] = x_ref[...] + gate_ref[...] * mix


def _out_project(y_r, y_a, x, mod4, w_out, layer, row0, row_step):
    n, d = x.shape
    d_mix = w_out.shape[1]
    tm, tn = TOKEN_TILE, PROJ_COLS
    return pl.pallas_call(
        _out_proj_kernel,
        grid=(n // tm, d // tn),
        in_specs=[
            pl.BlockSpec((tm, D_RET), lambda i, j: (i, 0)),
            pl.BlockSpec((tm, D_ATT), lambda i, j: (i, 0)),
            pl.BlockSpec((None, d_mix, tn), lambda i, j: (layer, 0, j)),
            pl.BlockSpec((tm, tn), lambda i, j: (i, j)),
            pl.BlockSpec((None, None, 1, tn), lambda i, j: (layer, row0 + row_step * i, 0, 2 * (d // tn) + j)),
        ],
        out_specs=pl.BlockSpec((tm, tn), lambda i, j: (i, j)),
        out_shape=jax.ShapeDtypeStruct((n, d), F32),
        scratch_shapes=[pltpu.VMEM((tm, d_mix), BF16)],
        compiler_params=_params(("arbitrary", "arbitrary")),
        name="out_proj",
    )(y_r, y_a, w_out, x, mod4)


def _ffn_kernel(x_ref, shift_ref, scale_ref, gate_ref, wg_ref, wv_ref, cwg_ref, cwv_ref,
                cbg_ref, cbv_ref, wd_ref, o_ref, h_ref, *, seq_len):
    j = pl.program_id(1)
    rows = x_ref.shape[0]

    @pl.when(j == 0)
    def _():
        _modulated_norm_to(h_ref, x_ref, shift_ref, scale_ref)
        o_ref[...] = jnp.zeros_like(o_ref)

    h = h_ref[...]

    def conv_branch(w_ref, cw_ref, cb_ref):
        up = jnp.dot(h, w_ref[...], preferred_element_type=F32)
        t = lax.broadcasted_iota(jnp.int32, up.shape, 0) % seq_len
        prev = jnp.where(t == 0, 0.0, pltpu.roll(up, 1, axis=0))
        nxt = jnp.where(t == seq_len - 1, 0.0, pltpu.roll(up, rows - 1, axis=0))
        return cb_ref[...] + prev * cw_ref[0:1, :] + up * cw_ref[1:2, :] + nxt * cw_ref[2:3, :]

    act = jax.nn.silu(conv_branch(wg_ref, cwg_ref, cbg_ref)) * conv_branch(wv_ref, cwv_ref, cbv_ref)
    o_ref[...] += jnp.dot(act.astype(BF16), wd_ref[...], preferred_element_type=F32)

    @pl.when(j == pl.num_programs(1) - 1)
    def _():
        o_ref[...] = x_ref[...] + gate_ref[...] * o_ref[...]


def _conv_ffn(x, mod4, w_up, conv_w, conv_b, w_down, layer, row0, row_step, seq_len):
    n, d = x.shape
    d_ff = w_down.shape[1]
    tm, tf = TOKEN_TILE, FFN_COLS
    nf = d_ff // tf
    conv_b3 = conv_b.reshape(conv_b.shape[0], 1, 2 * d_ff)
    return pl.pallas_call(
        functools.partial(_ffn_kernel, seq_len=seq_len),
        grid=(n // tm, nf),
        in_specs=[
            pl.BlockSpec((tm, d), lambda i, j: (i, 0)),
            _mod_spec(layer, 3, d, row0, row_step),
            _mod_spec(layer, 4, d, row0, row_step),
            _mod_spec(layer, 5, d, row0, row_step),
            pl.BlockSpec((None, d, tf), lambda i, j: (layer, 0, j)),
            pl.BlockSpec((None, d, tf), lambda i, j: (layer, 0, nf + j)),
            pl.BlockSpec((None, CONV_W, tf), lambda i, j: (layer, 0, j)),
            pl.BlockSpec((None, CONV_W, tf), lambda i, j: (layer, 0, nf + j)),
            pl.BlockSpec((None, 1, tf), lambda i, j: (layer, 0, j)),
            pl.BlockSpec((None, 1, tf), lambda i, j: (layer, 0, nf + j)),
            pl.BlockSpec((None, tf, d), lambda i, j: (layer, j, 0)),
        ],
        out_specs=pl.BlockSpec((tm, d), lambda i, j: (i, 0)),
        out_shape=jax.ShapeDtypeStruct((n, d), F32),
        scratch_shapes=[pltpu.VMEM((tm, d), BF16)],
        compiler_params=_params(("arbitrary", "arbitrary")),
        name="conv_ffn",
    )(x, mod4, mod4, mod4, w_up, w_up, conv_w, conv_w, conv_b3, conv_b3, w_down)


def _final_norm_kernel(x_ref, gain_ref, o_ref):
    x = x_ref[...]
    o_ref[...] = x * lax.rsqrt(jnp.mean(x * x, axis=-1, keepdims=True) + EPS) * gain_ref[...]


def _final_norm(x, gain):
    n, d = x.shape
    tm = 256
    return pl.pallas_call(
        _final_norm_kernel,
        grid=(n // tm,),
        in_specs=[pl.BlockSpec((tm, d), lambda i: (i, 0)), pl.BlockSpec((1, d), lambda i: (0, 0))],
        out_specs=pl.BlockSpec((tm, d), lambda i: (i, 0)),
        out_shape=jax.ShapeDtypeStruct((n, d), F32),
        compiler_params=_params(("arbitrary",)),
        name="final_norm",
    )(x, gain.reshape(1, d))


def kernel(x_prompt, x_sample, cache_k, cache_v, state_ret, c, c_ctx, w_mod, b_mod, w_in, w_out,
           ret_log_decay, ret_gn, att_sink, w_up, conv_w, conv_b, w_down, final_gain):
    batch, seq, d = x_prompt.shape
    dec_batch, dec_seq, _ = x_sample.shape
    depth = w_in.shape[0]
    assert TOKEN_TILE % seq == 0 and dec_seq == TOKEN_TILE and 1 + dec_batch <= COND_ROWS
    assert w_in.shape[-1] == D_IN and d == D_RET + D_ATT

    cond = jnp.zeros((COND_ROWS, d), F32).at[0].set(c_ctx).at[1:1 + dec_batch].set(c)
    mod = _modulation(cond, w_mod, b_mod)
    mod4 = mod.reshape(depth, COND_ROWS, 1, N_MOD * d)
    cos, sin_signed = _rope_tables(dec_seq)
    w_in, w_out, w_up, w_down = (_to_bf16(w) for w in (w_in, w_out, w_up, w_down))

    xp = x_prompt.reshape(batch * seq, d)
    xs = x_sample.reshape(dec_batch * dec_seq, d)
    new_k, new_v, new_s = [], [], []
    for l in range(depth):
        proj, kv = _project(xp, mod4, w_in, l, 0, 0, emit_kv=True)
        y_r, s_l = _retention(proj, ret_log_decay[l], ret_gn[l], seq, emit_state=True)
        y_a = _context_attention(proj, att_sink[l], seq)
        xp = _out_project(y_r, y_a, xp, mod4, w_out, l, 0, 0)
        xp = _conv_ffn(xp, mod4, w_up, conv_w, conv_b, w_down, l, 0, 0, seq)
        kv = kv.reshape(batch, seq, 2, H_KV, DH_ATT)
        new_k.append(kv[:, :, 0].transpose(0, 2, 1, 3))
        new_v.append(kv[:, :, 1].transpose(0, 2, 1, 3))
        new_s.append(s_l)
        (proj,) = _project(xs, mod4, w_in, l, 1, 1, emit_kv=False)
        (y_r,) = _retention(proj, ret_log_decay[l], ret_gn[l], dec_seq, state=state_ret, layer=l)
        y_a = _latent_attention(proj, att_sink[l], cache_k, cache_v, l, dec_seq, cos, sin_signed)
        xs = _out_project(y_r, y_a, xs, mod4, w_out, l, 1, 1)
        xs = _conv_ffn(xs, mod4, w_up, conv_w, conv_b, w_down, l, 1, 1, dec_seq)

    y_prompt = _final_norm(xp, final_gain).reshape(batch, seq, d)
    y_sample = _final_norm(xs, final_gain).reshape(dec_batch, dec_seq, d)
    return (y_prompt, y_sample, jnp.stack(new_k, axis=1), jnp.stack(new_v, axis=1),
            jnp.stack(new_s, axis=1))
```

```python
import functools

import jax
import jax.numpy as jnp
from jax import lax
from jax.experimental import pallas as pl
from jax.experimental.pallas import tpu as pltpu

F32 = jnp.float32
BF16 = jnp.bfloat16

GRID_W = 64
H_RET = 8
DK_RET = 128
DV_RET = 128
D_RET = H_RET * DV_RET
H_ATT = 8
H_KV = 2
G_ATT = H_ATT // H_KV
DH_ATT = 128
D_ATT = H_ATT * DH_ATT
WINDOW = 128
BLOCK = 128
CONV_W = 3
N_MOD = 6
ROPE_BASE = 10000.0
EPS = 1e-6
NEG_INF = -1e30

OFF_QR = 0
OFF_KR = OFF_QR + H_RET * DK_RET
OFF_VR = OFF_KR + H_RET * DK_RET
OFF_GR = OFF_VR + H_RET * DV_RET
OFF_QA = OFF_GR + H_RET * DV_RET
OFF_KA = OFF_QA + H_ATT * DH_ATT
OFF_VA = OFF_KA + H_KV * DH_ATT
D_IN = OFF_VA + H_KV * DH_ATT
D_KV = 2 * H_KV * DH_ATT

COND_ROWS = 8
TOKEN_TILE = 1024
PROJ_COLS = 512
FFN_COLS = 512
NORM_ROWS = 256
CONV_ROWS = 256
RET_CHUNK = 256
CAST_BLOCK_BYTES = 8 * 1024 * 1024
V7X_VMEM_LIMIT = 58 * 1024 * 1024


def _params(semantics):
    return pltpu.CompilerParams(dimension_semantics=semantics, vmem_limit_bytes=V7X_VMEM_LIMIT)


def _cast_kernel(w_ref, o_ref):
    o_ref[...] = w_ref[...].astype(o_ref.dtype)


def _to_bf16(w):
    depth, k, n = w.shape
    rows = max(16, min(k, CAST_BLOCK_BYTES // (4 * n) // 16 * 16))
    while k % rows:
        rows -= 16
    return pl.pallas_call(
        _cast_kernel,
        grid=(depth, k // rows),
        in_specs=[pl.BlockSpec((None, rows, n), lambda l, i: (l, i, 0))],
        out_specs=pl.BlockSpec((None, rows, n), lambda l, i: (l, i, 0)),
        out_shape=jax.ShapeDtypeStruct(w.shape, BF16),
        compiler_params=_params(("arbitrary", "arbitrary")),
        name="weight_to_bf16",
    )(w)


def _mod_kernel(cond_ref, w_ref, b_ref, o_ref):
    a = jax.nn.silu(cond_ref[...]).astype(BF16)
    o_ref[...] = jnp.dot(a, w_ref[...].astype(BF16), preferred_element_type=F32) + b_ref[...]


def _modulation(cond, w_mod, b_mod):
    depth, d, n = w_mod.shape
    tn = 1024
    return pl.pallas_call(
        _mod_kernel,
        grid=(depth, n // tn),
        in_specs=[
            pl.BlockSpec((COND_ROWS, d), lambda l, j: (0, 0)),
            pl.BlockSpec((None, d, tn), lambda l, j: (l, 0, j)),
            pl.BlockSpec((None, 1, tn), lambda l, j: (l, 0, j)),
        ],
        out_specs=pl.BlockSpec((None, COND_ROWS, tn), lambda l, j: (l, 0, j)),
        out_shape=jax.ShapeDtypeStruct((depth, COND_ROWS, n), F32),
        compiler_params=_params(("arbitrary", "arbitrary")),
        name="modulation",
    )(cond, w_mod, b_mod.reshape(depth, 1, n))


def _mod_spec(layer, chunk, d, row0, row_step):
    return pl.BlockSpec((None, None, 1, d), lambda i, j: (layer, row0 + row_step * i, 0, chunk))


def _modulated_norm_to(h_ref, x_ref, shift_ref, scale_ref):
    rows = x_ref.shape[0]
    one_plus = 1.0 + scale_ref[...]
    shift = shift_ref[...]

    def body(c, carry):
        r = pl.ds(pl.multiple_of(c * NORM_ROWS, NORM_ROWS), NORM_ROWS)
        x = x_ref[r, :]
        inv = lax.rsqrt(jnp.mean(x * x, axis=-1, keepdims=True) + EPS)
        h_ref[r, :] = ((x * inv) * one_plus + shift).astype(h_ref.dtype)
        return carry

    lax.fori_loop(0, rows // NORM_ROWS, body, 0)


def _proj_kernel(*refs, kv_step, n_carried):
    x_ref, shift_ref, scale_ref, w_ref = refs[:4]
    o_ref = refs[4 + n_carried]
    h_ref = refs[-1]
    j = pl.program_id(1)

    @pl.when(j == 0)
    def _():
        _modulated_norm_to(h_ref, x_ref, shift_ref, scale_ref)

    res = jnp.dot(h_ref[...], w_ref[...], preferred_element_type=F32)
    o_ref[...] = res.astype(o_ref.dtype)
    if kv_step is not None:
        k_ref, v_ref = refs[5 + n_carried:7 + n_carried]
        seqs, _, seq_len, _ = k_ref.shape
        kv0 = res.shape[1] - D_KV

        @pl.when(j == kv_step)
        def _():
            for dst, off in ((k_ref, kv0), (v_ref, kv0 + H_KV * DH_ATT)):
                for s in range(seqs):
                    for hh in range(H_KV):
                        dst[s, hh] = res[s * seq_len:(s + 1) * seq_len,
                                         off + hh * DH_ATT:off + (hh + 1) * DH_ATT]


def _project(x, mod4, w_in, layer, row0, row_step, cache_out=None):
    n, d = x.shape
    d_in = w_in.shape[-1]
    tm, tn = TOKEN_TILE, PROJ_COLS
    assert tn >= D_KV and d_in % tn == 0
    n_steps = d_in // tn
    in_specs = [
        pl.BlockSpec((tm, d), lambda i, j: (i, 0)),
        _mod_spec(layer, 0, d, row0, row_step),
        _mod_spec(layer, 1, d, row0, row_step),
        pl.BlockSpec((None, d, tn), lambda i, j: (layer, 0, j)),
    ]
    args = [x, mod4, mod4, w_in]
    out_specs = [pl.BlockSpec((tm, tn), lambda i, j: (i, j))]
    out_shape = [jax.ShapeDtypeStruct((n, d_in), BF16)]
    aliases = {}
    n_carried = 0
    if cache_out is not None:
        seq_len, depth, k_buf, v_buf = cache_out
        seqs = tm // seq_len
        cache_shape = jax.ShapeDtypeStruct((n // seq_len, depth, H_KV, seq_len, DH_ATT), F32)
        for buf in (k_buf, v_buf):
            out_specs.append(pl.BlockSpec((seqs, None, H_KV, seq_len, DH_ATT), lambda i, j: (i, layer, 0, 0, 0)))
            out_shape.append(cache_shape)
            if buf is not None:
                aliases[len(args)] = len(out_shape) - 1
                in_specs.append(pl.BlockSpec(memory_space=pl.ANY))
                args.append(buf)
                n_carried += 1
    return pl.pallas_call(
        functools.partial(_proj_kernel, kv_step=n_steps - 1 if cache_out is not None else None,
                          n_carried=n_carried),
        grid=(n // tm, n_steps),
        in_specs=in_specs,
        out_specs=out_specs,
        out_shape=out_shape,
        input_output_aliases=aliases,
        scratch_shapes=[pltpu.VMEM((tm, d), BF16)],
        compiler_params=_params(("arbitrary", "arbitrary")),
        name="in_proj",
    )(*args)


def _ret_kernel(*refs, seq_len, chunk, has_state, emit_state, carried):
    it = iter(refs)
    ld_ref = next(it)
    q_ref, k_ref, v_ref, g_ref, gn_ref = (next(it) for _ in range(5))
    s0_ref = next(it) if has_state else None
    if carried:
        next(it)
    y_ref = next(it)
    snew_ref = next(it) if emit_state else None
    o_ref, intra_ref, qdf_ref, kdf_ref, qdb_ref, kdb_ref = (next(it) for _ in range(6))

    n_chunks = seq_len // chunk
    k_scale = DK_RET ** -0.5

    @pl.when(pl.program_id(0) == 0)
    def _():
        row = lax.broadcasted_iota(jnp.int32, (chunk, chunk), 0).astype(F32)
        col = lax.broadcasted_iota(jnp.int32, (chunk, chunk), 1).astype(F32)
        diff = row - col
        pos = lax.broadcasted_iota(jnp.int32, (chunk, DK_RET), 0).astype(F32)
        for h in range(H_RET):
            lg_f = ld_ref[0, h]
            lg_b = ld_ref[1, h]
            intra_ref[h] = (jnp.where(diff >= 0, jnp.exp(lg_f * jnp.maximum(diff, 0.0)), 0.0)
                            + jnp.where(diff <= 0, jnp.exp(lg_b * jnp.maximum(-diff, 0.0)), 0.0)) * k_scale
            qdf_ref[h] = jnp.exp(lg_f * (pos + 1.0))
            kdf_ref[h] = jnp.exp(lg_f * (chunk - 1.0 - pos)) * k_scale
            qdb_ref[h] = jnp.exp(lg_b * (chunk - pos))
            kdb_ref[h] = jnp.exp(lg_b * pos) * k_scale

    def rows(n):
        return pl.ds(n * chunk, chunk)

    def kv_update(k_b, dec, v_b):
        kd = (k_b.astype(F32) * dec).T.astype(BF16)
        return jnp.dot(kd, v_b, preferred_element_type=F32)

    for h in range(H_RET):
        cols = slice(h * DK_RET, (h + 1) * DK_RET)
        c_dec_f = jnp.exp(jnp.full((DK_RET, DV_RET), ld_ref[0, h] * chunk, F32))
        c_dec_b = jnp.exp(jnp.full((DK_RET, DV_RET), ld_ref[1, h] * chunk, F32))

        s_f = s0_ref[0, h] if has_state else jnp.zeros((DK_RET, DV_RET), F32)
        for n in range(n_chunks):
            q_b = q_ref[rows(n), cols]
            k_b = k_ref[rows(n), cols]
            v_b = v_ref[rows(n), cols]
            sc = lax.dot_general(q_b, k_b, (((1,), (1,)), ((), ())),
                                 preferred_element_type=F32) * intra_ref[h]
            o = jnp.dot(sc.astype(BF16), v_b, preferred_element_type=F32)
            if has_state or n > 0:
                o = o + jnp.dot(q_b, s_f.astype(BF16), preferred_element_type=F32) * qdf_ref[h]
            o_ref[rows(n), :] = o
            s_f = c_dec_f * s_f + kv_update(k_b, kdf_ref[h], v_b)

        s_b = s0_ref[1, h] if has_state else jnp.zeros((DK_RET, DV_RET), F32)
        for n in reversed(range(n_chunks)):
            q_b = q_ref[rows(n), cols]
            k_b = k_ref[rows(n), cols]
            v_b = v_ref[rows(n), cols]
            if has_state or n < n_chunks - 1:
                o_ref[rows(n), :] += jnp.dot(q_b, s_b.astype(BF16), preferred_element_type=F32) * qdb_ref[h]
            s_b = c_dec_b * s_b + kv_update(k_b, kdb_ref[h], v_b)

        if emit_state:
            snew_ref[0, h] = s_f
            snew_ref[1, h] = s_b

        gn = gn_ref[:, cols]
        for n in range(n_chunks):
            o = o_ref[rows(n), :]
            mu = jnp.mean(o, axis=-1, keepdims=True)
            cen = o - mu
            var = jnp.mean(cen * cen, axis=-1, keepdims=True)
            on = cen * lax.rsqrt(var + EPS) * gn
            y_ref[rows(n), cols] = (jax.nn.silu(g_ref[rows(n), cols].astype(F32)) * on).astype(y_ref.dtype)


def _retention(proj, log_decay, gn, seq_len, layer, state=None, state_out=None):
    n = proj.shape[0]
    batch = n // seq_len
    has_state = state is not None
    chunk = min(RET_CHUNK, seq_len)
    blk = lambda off: pl.BlockSpec((seq_len, D_RET), lambda b: (b, off // D_RET))
    state_spec = pl.BlockSpec((None, None, 2, H_RET, DK_RET, DV_RET), lambda b: (b, layer, 0, 0, 0, 0))
    in_specs = [
        pl.BlockSpec(memory_space=pltpu.SMEM),
        blk(OFF_QR), blk(OFF_KR), blk(OFF_VR), blk(OFF_GR),
        pl.BlockSpec((1, D_RET), lambda b: (0, 0)),
    ]
    args = [log_decay, proj, proj, proj, proj, gn.reshape(1, D_RET)]
    if has_state:
        in_specs.append(state_spec)
        args.append(state)
    out_specs = [pl.BlockSpec((seq_len, D_RET), lambda b: (b, 0))]
    out_shape = [jax.ShapeDtypeStruct((n, D_RET), BF16)]
    aliases = {}
    if state_out is not None:
        depth, buf = state_out
        out_specs.append(state_spec)
        out_shape.append(jax.ShapeDtypeStruct((batch, depth, 2, H_RET, DK_RET, DV_RET), F32))
        if buf is not None:
            aliases[len(args)] = 1
            in_specs.append(pl.BlockSpec(memory_space=pl.ANY))
            args.append(buf)
    kern = functools.partial(_ret_kernel, seq_len=seq_len, chunk=chunk, has_state=has_state,
                             emit_state=state_out is not None, carried=bool(aliases))
    return pl.pallas_call(
        kern,
        grid=(batch,),
        in_specs=in_specs,
        out_specs=out_specs,
        out_shape=out_shape,
        input_output_aliases=aliases,
        scratch_shapes=[
            pltpu.VMEM((seq_len, DV_RET), F32),
            pltpu.VMEM((H_RET, chunk, chunk), F32),
            pltpu.VMEM((H_RET, chunk, DK_RET), F32),
            pltpu.VMEM((H_RET, chunk, DK_RET), F32),
            pltpu.VMEM((H_RET, chunk, DK_RET), F32),
            pltpu.VMEM((H_RET, chunk, DK_RET), F32),
        ],
        compiler_params=_params(("arbitrary",)),
        name="retention",
    )(*args)


def _softmax_parts(scores, sink):
    m = jnp.maximum(sink, functools.reduce(
        jnp.maximum, [jnp.max(s, axis=-1, keepdims=True) for s in scores]))
    ps = [jnp.exp(s - m) for s in scores]
    denom = jnp.exp(sink - m) + functools.reduce(
        lambda a, b: a + b, [jnp.sum(p, axis=-1, keepdims=True) for p in ps])
    return ps, denom


def _ctx_attn_kernel(sink_ref, q_ref, k_ref, v_ref, o_ref):
    kv_head = pl.program_id(1)
    scale = DH_ATT ** -0.5
    k_b = k_ref[...]
    v_b = v_ref[...]
    for g in range(G_ATT):
        cols = slice(g * DH_ATT, (g + 1) * DH_ATT)
        s = lax.dot_general(q_ref[:, cols], k_b, (((1,), (1,)), ((), ())), preferred_element_type=F32) * scale
        (p,), denom = _softmax_parts([s], sink_ref[kv_head * G_ATT + g])
        o = jnp.dot(p.astype(BF16), v_b, preferred_element_type=F32)
        o_ref[:, cols] = (o / denom).astype(o_ref.dtype)


def _context_attention(proj, sink, seq_len):
    n = proj.shape[0]
    qw = G_ATT * DH_ATT
    return pl.pallas_call(
        _ctx_attn_kernel,
        grid=(n // seq_len, H_KV),
        in_specs=[
            pl.BlockSpec(memory_space=pltpu.SMEM),
            pl.BlockSpec((seq_len, qw), lambda b, h: (b, OFF_QA // qw + h)),
            pl.BlockSpec((seq_len, DH_ATT), lambda b, h: (b, OFF_KA // DH_ATT + h)),
            pl.BlockSpec((seq_len, DH_ATT), lambda b, h: (b, OFF_VA // DH_ATT + h)),
        ],
        out_specs=pl.BlockSpec((seq_len, qw), lambda b, h: (b, h)),
        out_shape=jax.ShapeDtypeStruct((n, D_ATT), BF16),
        compiler_params=_params(("arbitrary", "arbitrary")),
        name="context_attention",
    )(sink, proj, proj, proj)


def _rope(x, cos, sin_signed):
    quarter = DH_ATT // 4
    lane = lax.broadcasted_iota(jnp.int32, x.shape, 1)
    first = (lane % (2 * quarter)) < quarter
    partner = jnp.where(first, pltpu.roll(x, DH_ATT - quarter, axis=1), pltpu.roll(x, quarter, axis=1))
    return x * cos + partner * sin_signed


def _lat_attn_kernel(sink_ref, q_ref, k_ref, v_ref, ck_ref, cv_ref, cos_ref, sin_ref, o_ref,
                     kr_ref, *, seq_len):
    kv_head = pl.program_id(1)
    qb = pl.program_id(2)
    scale = DH_ATT ** -0.5
    span = BLOCK + 2 * WINDOW

    @pl.when(qb == 0)
    def _():
        kr_ref[...] = _rope(k_ref[...].astype(F32), cos_ref[...], sin_ref[...]).astype(BF16)

    q_rows = pl.ds(pl.multiple_of(qb * BLOCK, BLOCK), BLOCK)
    cos_q = cos_ref[q_rows, :]
    sin_q = sin_ref[q_rows, :]
    q4 = jnp.concatenate(
        [_rope(q_ref[:, g * DH_ATT:(g + 1) * DH_ATT].astype(F32), cos_q, sin_q) for g in range(G_ATT)],
        axis=0).astype(BF16)

    start = pl.multiple_of(jnp.clip(qb * BLOCK - WINDOW, 0, seq_len - span), BLOCK)
    k_win = kr_ref[pl.ds(start, span), :]
    v_win = v_ref[pl.ds(start, span), :]
    q_pos = qb * BLOCK + lax.broadcasted_iota(jnp.int32, (G_ATT * BLOCK, span), 0) % BLOCK
    k_pos = start + lax.broadcasted_iota(jnp.int32, (G_ATT * BLOCK, span), 1)
    band4 = jnp.abs(k_pos - q_pos) <= WINDOW

    nt = (((1,), (1,)), ((), ()))
    s_loc = lax.dot_general(q4, k_win, nt, preferred_element_type=F32) * scale
    s_loc = jnp.where(band4, s_loc, NEG_INF)
    s_ctx = lax.dot_general(q4, ck_ref[...].astype(BF16), nt, preferred_element_type=F32) * scale
    row_head = lax.broadcasted_iota(jnp.int32, (G_ATT * BLOCK, 1), 0) // BLOCK
    sink4 = jnp.full((G_ATT * BLOCK, 1), sink_ref[kv_head * G_ATT], F32)
    for g in range(1, G_ATT):
        sink4 = jnp.where(row_head == g, sink_ref[kv_head * G_ATT + g], sink4)
    (p_loc, p_ctx), denom = _softmax_parts([s_loc, s_ctx], sink4)
    o4 = (jnp.dot(p_loc.astype(BF16), v_win, preferred_element_type=F32)
          + jnp.dot(p_ctx.astype(BF16), cv_ref[...].astype(BF16), preferred_element_type=F32)) / denom
    for g in range(G_ATT):
        o_ref[:, g * DH_ATT:(g + 1) * DH_ATT] = o4[g * BLOCK:(g + 1) * BLOCK, :].astype(o_ref.dtype)


def _latent_attention(proj, sink, cache_k, cache_v, layer, seq_len, cos, sin_signed):
    n = proj.shape[0]
    batch = n // seq_len
    nq = seq_len // BLOCK
    qw = G_ATT * DH_ATT
    past = cache_k.shape[3]
    cache_spec = pl.BlockSpec((None, None, None, past, DH_ATT), lambda b, h, i: (b, layer, h, 0, 0))
    return pl.pallas_call(
        functools.partial(_lat_attn_kernel, seq_len=seq_len),
        grid=(batch, H_KV, nq),
        in_specs=[
            pl.BlockSpec(memory_space=pltpu.SMEM),
            pl.BlockSpec((BLOCK, qw), lambda b, h, i: (b * nq + i, OFF_QA // qw + h)),
            pl.BlockSpec((seq_len, DH_ATT), lambda b, h, i: (b, OFF_KA // DH_ATT + h)),
            pl.BlockSpec((seq_len, DH_ATT), lambda b, h, i: (b, OFF_VA // DH_ATT + h)),
            cache_spec, cache_spec,
            pl.BlockSpec((seq_len, DH_ATT), lambda b, h, i: (0, 0)),
            pl.BlockSpec((seq_len, DH_ATT), lambda b, h, i: (0, 0)),
        ],
        out_specs=pl.BlockSpec((BLOCK, qw), lambda b, h, i: (b * nq + i, h)),
        out_shape=jax.ShapeDtypeStruct((n, D_ATT), BF16),
        scratch_shapes=[pltpu.VMEM((seq_len, DH_ATT), BF16)],
        compiler_params=_params(("arbitrary", "arbitrary", "arbitrary")),
        name="latent_attention",
    )(sink, proj, proj, proj, cache_k, cache_v, cos, sin_signed)


def _rope_tables(seq_len):
    rows = seq_len // GRID_W
    row = jnp.repeat(jnp.arange(rows), GRID_W)
    col = jnp.tile(jnp.arange(GRID_W), rows)
    quarter = DH_ATT // 4
    inv = ROPE_BASE ** (-jnp.arange(quarter, dtype=F32) / quarter)
    ang_r, ang_c = row[:, None] * inv, col[:, None] * inv
    cos = jnp.concatenate([jnp.cos(ang_r)] * 2 + [jnp.cos(ang_c)] * 2, axis=-1)
    sin_signed = jnp.concatenate([-jnp.sin(ang_r), jnp.sin(ang_r), -jnp.sin(ang_c), jnp.sin(ang_c)], axis=-1)
    return cos.astype(F32), sin_signed.astype(F32)


def _out_proj_kernel(yr_ref, ya_ref, w_ref, x_ref, gate_ref, o_ref, y_ref):
    @pl.when(pl.program_id(1) == 0)
    def _():
        y_ref[:, :D_RET] = yr_ref[...]
        y_ref[:, D_RET:] = ya_ref[...]

    mix = jnp.dot(y_ref[...], w_ref[...], preferred_element_type=F32)
    o_ref[...] = x_ref[...] + gate_ref[...] * mix


def _out_project(y_r, y_a, x, mod4, w_out, layer, row0, row_step):
    n, d = x.shape
    d_mix = w_out.shape[1]
    tm, tn = TOKEN_TILE, PROJ_COLS
    return pl.pallas_call(
        _out_proj_kernel,
        grid=(n // tm, d // tn),
        in_specs=[
            pl.BlockSpec((tm, D_RET), lambda i, j: (i, 0)),
            pl.BlockSpec((tm, D_ATT), lambda i, j: (i, 0)),
            pl.BlockSpec((None, d_mix, tn), lambda i, j: (layer, 0, j)),
            pl.BlockSpec((tm, tn), lambda i, j: (i, j)),
            pl.BlockSpec((None, None, 1, tn), lambda i, j: (layer, row0 + row_step * i, 0, 2 * (d // tn) + j)),
        ],
        out_specs=pl.BlockSpec((tm, tn), lambda i, j: (i, j)),
        out_shape=jax.ShapeDtypeStruct((n, d), F32),
        scratch_shapes=[pltpu.VMEM((tm, d_mix), BF16)],
        compiler_params=_params(("arbitrary", "arbitrary")),
        name="out_proj",
    )(y_r, y_a, w_out, x, mod4)


def _ffn_kernel(x_ref, shift_ref, scale_ref, gate_ref, wg_ref, wv_ref, cwg_ref, cwv_ref,
                cbg_ref, cbv_ref, wd_ref, o_ref, h_ref, *, seq_len):
    j = pl.program_id(1)
    rows = x_ref.shape[0]

    @pl.when(j == 0)
    def _():
        _modulated_norm_to(h_ref, x_ref, shift_ref, scale_ref)
        o_ref[...] = jnp.zeros_like(o_ref)

    cr = CONV_ROWS
    n_chunks = rows // cr
    t = lax.broadcasted_iota(jnp.int32, (cr, wg_ref.shape[1]), 0)
    first, last = t == 0, t == cr - 1
    ups = []
    for m in range(n_chunks):
        hm = h_ref[m * cr:(m + 1) * cr, :]
        ups.append(tuple(jnp.dot(hm, w_ref[...], preferred_element_type=F32) for w_ref in (wg_ref, wv_ref)))

    for m in range(n_chunks):
        def conv_branch(b, cw_ref, cb_ref):
            up = ups[m][b]
            before = ups[m - 1][b][cr - 1:cr, :] if (m * cr) % seq_len else 0.0
            after = ups[m + 1][b][0:1, :] if ((m + 1) * cr) % seq_len else 0.0
            prev = jnp.where(first, before, pltpu.roll(up, 1, axis=0))
            nxt = jnp.where(last, after, pltpu.roll(up, cr - 1, axis=0))
            return cb_ref[...] + prev * cw_ref[0:1, :] + up * cw_ref[1:2, :] + nxt * cw_ref[2:3, :]

        act = jax.nn.silu(conv_branch(0, cwg_ref, cbg_ref)) * conv_branch(1, cwv_ref, cbv_ref)
        o_ref[m * cr:(m + 1) * cr, :] += jnp.dot(act.astype(BF16), wd_ref[...], preferred_element_type=F32)

    @pl.when(j == pl.num_programs(1) - 1)
    def _():
        o_ref[...] = x_ref[...] + gate_ref[...] * o_ref[...]


def _conv_ffn(x, mod4, w_up, conv_w, conv_b, w_down, layer, row0, row_step, seq_len):
    n, d = x.shape
    d_ff = w_down.shape[1]
    tm, tf = TOKEN_TILE, FFN_COLS
    nf = d_ff // tf
    conv_b3 = conv_b.reshape(conv_b.shape[0], 1, 2 * d_ff)
    return pl.pallas_call(
        functools.partial(_ffn_kernel, seq_len=seq_len),
        grid=(n // tm, nf),
        in_specs=[
            pl.BlockSpec((tm, d), lambda i, j: (i, 0)),
            _mod_spec(layer, 3, d, row0, row_step),
            _mod_spec(layer, 4, d, row0, row_step),
            _mod_spec(layer, 5, d, row0, row_step),
            pl.BlockSpec((None, d, tf), lambda i, j: (layer, 0, j)),
            pl.BlockSpec((None, d, tf), lambda i, j: (layer, 0, nf + j)),
            pl.BlockSpec((None, CONV_W, tf), lambda i, j: (layer, 0, j)),
            pl.BlockSpec((None, CONV_W, tf), lambda i, j: (layer, 0, nf + j)),
            pl.BlockSpec((None, 1, tf), lambda i, j: (layer, 0, j)),
            pl.BlockSpec((None, 1, tf), lambda i, j: (layer, 0, nf + j)),
            pl.BlockSpec((None, tf, d), lambda i, j: (layer, j, 0)),
        ],
        out_specs=pl.BlockSpec((tm, d), lambda i, j: (i, 0)),
        out_shape=jax.ShapeDtypeStruct((n, d), F32),
        scratch_shapes=[pltpu.VMEM((tm, d), BF16)],
        compiler_params=_params(("arbitrary", "arbitrary")),
        name="conv_ffn",
    )(x, mod4, mod4, mod4, w_up, w_up, conv_w, conv_w, conv_b3, conv_b3, w_down)


def _final_norm_kernel(x_ref, gain_ref, o_ref):
    x = x_ref[...]
    o_ref[...] = x * lax.rsqrt(jnp.mean(x * x, axis=-1, keepdims=True) + EPS) * gain_ref[...]


def _final_norm(x, gain):
    n, d = x.shape
    tm = 256
    return pl.pallas_call(
        _final_norm_kernel,
        grid=(n // tm,),
        in_specs=[pl.BlockSpec((tm, d), lambda i: (i, 0)), pl.BlockSpec((1, d), lambda i: (0, 0))],
        out_specs=pl.BlockSpec((tm, d), lambda i: (i, 0)),
        out_shape=jax.ShapeDtypeStruct((n, d), F32),
        compiler_params=_params(("arbitrary",)),
        name="final_norm",
    )(x, gain.reshape(1, d))


def kernel(x_prompt, x_sample, cache_k, cache_v, state_ret, c, c_ctx, w_mod, b_mod, w_in, w_out,
           ret_log_decay, ret_gn, att_sink, w_up, conv_w, conv_b, w_down, final_gain):
    batch, seq, d = x_prompt.shape
    dec_batch, dec_seq, _ = x_sample.shape
    depth = w_in.shape[0]
    assert TOKEN_TILE % seq == 0 and dec_seq == TOKEN_TILE and 1 + dec_batch <= COND_ROWS
    assert w_in.shape[-1] == D_IN and d == D_RET + D_ATT

    cond = jnp.zeros((COND_ROWS, d), F32).at[0].set(c_ctx).at[1:1 + dec_batch].set(c)
    mod = _modulation(cond, w_mod, b_mod)
    mod4 = mod.reshape(depth, COND_ROWS, 1, N_MOD * d)
    cos, sin_signed = _rope_tables(dec_seq)
    w_in, w_out, w_up, w_down = (_to_bf16(w) for w in (w_in, w_out, w_up, w_down))

    xp = x_prompt.reshape(batch * seq, d)
    xs = x_sample.reshape(dec_batch * dec_seq, d)
    new_k = new_v = new_s = None
    for l in range(depth):
        proj, new_k, new_v = _project(xp, mod4, w_in, l, 0, 0, cache_out=(seq, depth, new_k, new_v))
        y_r, new_s = _retention(proj, ret_log_decay[l], ret_gn[l], seq, l, state_out=(depth, new_s))
        y_a = _context_attention(proj, att_sink[l], seq)
        xp = _out_project(y_r, y_a, xp, mod4, w_out, l, 0, 0)
        xp = _conv_ffn(xp, mod4, w_up, conv_w, conv_b, w_down, l, 0, 0, seq)
        (proj,) = _project(xs, mod4, w_in, l, 1, 1)
        (y_r,) = _retention(proj, ret_log_decay[l], ret_gn[l], dec_seq, l, state=state_ret)
        y_a = _latent_attention(proj, att_sink[l], cache_k, cache_v, l, dec_seq, cos, sin_signed)
        xs = _out_project(y_r, y_a, xs, mod4, w_out, l, 1, 1)
        xs = _conv_ffn(xs, mod4, w_up, conv_w, conv_b, w_down, l, 1, 1, dec_seq)

    y_prompt = _final_norm(xp, final_gain).reshape(batch, seq, d)
    y_sample = _final_norm(xs, final_gain).reshape(dec_batch, dec_seq, d)
    return (y_prompt, y_sample, new_k, new_v, new_s)
```

```python
import functools

import jax
import jax.numpy as jnp
from jax import lax
from jax.experimental import pallas as pl
from jax.experimental.pallas import tpu as pltpu

F32 = jnp.float32
BF16 = jnp.bfloat16

GRID_W = 64
H_RET = 8
DK_RET = 128
DV_RET = 128
D_RET = H_RET * DV_RET
H_ATT = 8
H_KV = 2
G_ATT = H_ATT // H_KV
DH_ATT = 128
D_ATT = H_ATT * DH_ATT
WINDOW = 128
BLOCK = 128
CONV_W = 3
N_MOD = 6
ROPE_BASE = 10000.0
EPS = 1e-6
NEG_INF = -1e30

OFF_QR = 0
OFF_KR = OFF_QR + H_RET * DK_RET
OFF_VR = OFF_KR + H_RET * DK_RET
OFF_GR = OFF_VR + H_RET * DV_RET
OFF_QA = OFF_GR + H_RET * DV_RET
OFF_KA = OFF_QA + H_ATT * DH_ATT
OFF_VA = OFF_KA + H_KV * DH_ATT
D_IN = OFF_VA + H_KV * DH_ATT
D_KV = 2 * H_KV * DH_ATT

COND_ROWS = 8
TOKEN_TILE = 1024
PROJ_COLS = 1408
OUT_COLS = 1024
FFN_COLS = 512
NORM_ROWS = 256
CONV_ROWS = 256
RET_CHUNK = 256
CAST_BLOCK_BYTES = 8 * 1024 * 1024
V7X_VMEM_LIMIT = 58 * 1024 * 1024


def _params(semantics):
    return pltpu.CompilerParams(dimension_semantics=semantics, vmem_limit_bytes=V7X_VMEM_LIMIT)


def _cast_kernel(w_ref, o_ref):
    o_ref[...] = w_ref[...].astype(o_ref.dtype)


def _to_bf16(w):
    depth, k, n = w.shape
    rows = max(16, min(k, CAST_BLOCK_BYTES // (4 * n) // 16 * 16))
    while k % rows:
        rows -= 16
    return pl.pallas_call(
        _cast_kernel,
        grid=(depth, k // rows),
        in_specs=[pl.BlockSpec((None, rows, n), lambda l, i: (l, i, 0))],
        out_specs=pl.BlockSpec((None, rows, n), lambda l, i: (l, i, 0)),
        out_shape=jax.ShapeDtypeStruct(w.shape, BF16),
        compiler_params=_params(("arbitrary", "arbitrary")),
        name="weight_to_bf16",
    )(w)


def _mod_kernel(cond_ref, w_ref, b_ref, o_ref):
    a = jax.nn.silu(cond_ref[...]).astype(BF16)
    o_ref[...] = jnp.dot(a, w_ref[...].astype(BF16), preferred_element_type=F32) + b_ref[...]


def _modulation(cond, w_mod, b_mod):
    depth, d, n = w_mod.shape
    tn = 1024
    return pl.pallas_call(
        _mod_kernel,
        grid=(depth, n // tn),
        in_specs=[
            pl.BlockSpec((COND_ROWS, d), lambda l, j: (0, 0)),
            pl.BlockSpec((None, d, tn), lambda l, j: (l, 0, j)),
            pl.BlockSpec((None, 1, tn), lambda l, j: (l, 0, j)),
        ],
        out_specs=pl.BlockSpec((None, COND_ROWS, tn), lambda l, j: (l, 0, j)),
        out_shape=jax.ShapeDtypeStruct((depth, COND_ROWS, n), F32),
        compiler_params=_params(("arbitrary", "arbitrary")),
        name="modulation",
    )(cond, w_mod, b_mod.reshape(depth, 1, n))


def _mod_spec(layer, chunk, d, row0, row_step):
    return pl.BlockSpec((None, None, 1, d), lambda i, j: (layer, row0 + row_step * i, 0, chunk))


def _modulated_norm_to(h_ref, x_ref, shift_ref, scale_ref):
    rows = x_ref.shape[0]
    one_plus = 1.0 + scale_ref[...]
    shift = shift_ref[...]

    def body(c, carry):
        r = pl.ds(pl.multiple_of(c * NORM_ROWS, NORM_ROWS), NORM_ROWS)
        x = x_ref[r, :]
        inv = lax.rsqrt(jnp.mean(x * x, axis=-1, keepdims=True) + EPS)
        h_ref[r, :] = ((x * inv) * one_plus + shift).astype(h_ref.dtype)
        return carry

    lax.fori_loop(0, rows // NORM_ROWS, body, 0)


def _proj_kernel(*refs, kv_step, n_carried):
    x_ref, shift_ref, scale_ref, w_ref = refs[:4]
    o_ref = refs[4 + n_carried]
    h_ref = refs[-1]
    j = pl.program_id(1)

    @pl.when(j == 0)
    def _():
        _modulated_norm_to(h_ref, x_ref, shift_ref, scale_ref)

    res = jnp.dot(h_ref[...], w_ref[...], preferred_element_type=F32)
    o_ref[...] = res.astype(o_ref.dtype)
    if kv_step is not None:
        k_ref, v_ref = refs[5 + n_carried:7 + n_carried]
        seqs, _, seq_len, _ = k_ref.shape
        kv0 = res.shape[1] - D_KV

        @pl.when(j == kv_step)
        def _():
            for dst, off in ((k_ref, kv0), (v_ref, kv0 + H_KV * DH_ATT)):
                for s in range(seqs):
                    for hh in range(H_KV):
                        dst[s, hh] = res[s * seq_len:(s + 1) * seq_len,
                                         off + hh * DH_ATT:off + (hh + 1) * DH_ATT]


def _project(x, mod4, w_in, layer, row0, row_step, cache_out=None):
    n, d = x.shape
    d_in = w_in.shape[-1]
    tm, tn = TOKEN_TILE, PROJ_COLS
    assert tn >= D_KV and d_in % tn == 0
    n_steps = d_in // tn
    in_specs = [
        pl.BlockSpec((tm, d), lambda i, j: (i, 0)),
        _mod_spec(layer, 0, d, row0, row_step),
        _mod_spec(layer, 1, d, row0, row_step),
        pl.BlockSpec((None, d, tn), lambda i, j: (layer, 0, j)),
    ]
    args = [x, mod4, mod4, w_in]
    out_specs = [pl.BlockSpec((tm, tn), lambda i, j: (i, j))]
    out_shape = [jax.ShapeDtypeStruct((n, d_in), BF16)]
    aliases = {}
    n_carried = 0
    if cache_out is not None:
        seq_len, depth, k_buf, v_buf = cache_out
        seqs = tm // seq_len
        cache_shape = jax.ShapeDtypeStruct((n // seq_len, depth, H_KV, seq_len, DH_ATT), F32)
        for buf in (k_buf, v_buf):
            out_specs.append(pl.BlockSpec((seqs, None, H_KV, seq_len, DH_ATT), lambda i, j: (i, layer, 0, 0, 0)))
            out_shape.append(cache_shape)
            if buf is not None:
                aliases[len(args)] = len(out_shape) - 1
                in_specs.append(pl.BlockSpec(memory_space=pl.ANY))
                args.append(buf)
                n_carried += 1
    return pl.pallas_call(
        functools.partial(_proj_kernel, kv_step=n_steps - 1 if cache_out is not None else None,
                          n_carried=n_carried),
        grid=(n // tm, n_steps),
        in_specs=in_specs,
        out_specs=out_specs,
        out_shape=out_shape,
        input_output_aliases=aliases,
        scratch_shapes=[pltpu.VMEM((tm, d), BF16)],
        compiler_params=_params(("arbitrary", "arbitrary")),
        name="in_proj",
    )(*args)


def _ret_kernel(*refs, seq_len, chunk, has_state, emit_state, carried):
    it = iter(refs)
    ld_ref = next(it)
    q_ref, k_ref, v_ref, g_ref, gn_ref = (next(it) for _ in range(5))
    s0_ref = next(it) if has_state else None
    if carried:
        next(it)
    y_ref = next(it)
    snew_ref = next(it) if emit_state else None
    o_ref, intra_ref, qdf_ref, kdf_ref, qdb_ref, kdb_ref = (next(it) for _ in range(6))

    n_chunks = seq_len // chunk
    k_scale = DK_RET ** -0.5

    @pl.when(pl.program_id(0) == 0)
    def _():
        row = lax.broadcasted_iota(jnp.int32, (chunk, chunk), 0).astype(F32)
        col = lax.broadcasted_iota(jnp.int32, (chunk, chunk), 1).astype(F32)
        diff = row - col
        pos = lax.broadcasted_iota(jnp.int32, (chunk, DK_RET), 0).astype(F32)
        for h in range(H_RET):
            lg_f = ld_ref[0, h]
            lg_b = ld_ref[1, h]
            intra_ref[h] = (jnp.where(diff >= 0, jnp.exp(lg_f * jnp.maximum(diff, 0.0)), 0.0)
                            + jnp.where(diff <= 0, jnp.exp(lg_b * jnp.maximum(-diff, 0.0)), 0.0)) * k_scale
            qdf_ref[h] = jnp.exp(lg_f * (pos + 1.0))
            kdf_ref[h] = jnp.exp(lg_f * (chunk - 1.0 - pos)) * k_scale
            qdb_ref[h] = jnp.exp(lg_b * (chunk - pos))
            kdb_ref[h] = jnp.exp(lg_b * pos) * k_scale

    def rows(n):
        return pl.ds(n * chunk, chunk)

    def kv_update(k_b, dec, v_b):
        kd = (k_b.astype(F32) * dec).T.astype(BF16)
        return jnp.dot(kd, v_b, preferred_element_type=F32)

    for h in range(H_RET):
        cols = slice(h * DK_RET, (h + 1) * DK_RET)
        c_dec_f = jnp.exp(jnp.full((DK_RET, DV_RET), ld_ref[0, h] * chunk, F32))
        c_dec_b = jnp.exp(jnp.full((DK_RET, DV_RET), ld_ref[1, h] * chunk, F32))

        s_f = s0_ref[0, h] if has_state else jnp.zeros((DK_RET, DV_RET), F32)
        for n in range(n_chunks):
            q_b = q_ref[rows(n), cols]
            k_b = k_ref[rows(n), cols]
            v_b = v_ref[rows(n), cols]
            sc = lax.dot_general(q_b, k_b, (((1,), (1,)), ((), ())),
                                 preferred_element_type=F32) * intra_ref[h]
            o = jnp.dot(sc.astype(BF16), v_b, preferred_element_type=F32)
            if has_state or n > 0:
                o = o + jnp.dot(q_b, s_f.astype(BF16), preferred_element_type=F32) * qdf_ref[h]
            o_ref[rows(n), :] = o
            s_f = c_dec_f * s_f + kv_update(k_b, kdf_ref[h], v_b)

        s_b = s0_ref[1, h] if has_state else jnp.zeros((DK_RET, DV_RET), F32)
        for n in reversed(range(n_chunks)):
            q_b = q_ref[rows(n), cols]
            k_b = k_ref[rows(n), cols]
            v_b = v_ref[rows(n), cols]
            if has_state or n < n_chunks - 1:
                o_ref[rows(n), :] += jnp.dot(q_b, s_b.astype(BF16), preferred_element_type=F32) * qdb_ref[h]
            s_b = c_dec_b * s_b + kv_update(k_b, kdb_ref[h], v_b)

        if emit_state:
            snew_ref[0, h] = s_f
            snew_ref[1, h] = s_b

        gn = gn_ref[:, cols]
        for n in range(n_chunks):
            o = o_ref[rows(n), :]
            mu = jnp.mean(o, axis=-1, keepdims=True)
            cen = o - mu
            var = jnp.mean(cen * cen, axis=-1, keepdims=True)
            on = cen * lax.rsqrt(var + EPS) * gn
            y_ref[rows(n), cols] = (jax.nn.silu(g_ref[rows(n), cols].astype(F32)) * on).astype(y_ref.dtype)


def _retention(proj, log_decay, gn, seq_len, layer, state=None, state_out=None):
    n = proj.shape[0]
    batch = n // seq_len
    has_state = state is not None
    chunk = min(RET_CHUNK, seq_len)
    blk = lambda off: pl.BlockSpec((seq_len, D_RET), lambda b: (b, off // D_RET))
    state_spec = pl.BlockSpec((None, None, 2, H_RET, DK_RET, DV_RET), lambda b: (b, layer, 0, 0, 0, 0))
    in_specs = [
        pl.BlockSpec(memory_space=pltpu.SMEM),
        blk(OFF_QR), blk(OFF_KR), blk(OFF_VR), blk(OFF_GR),
        pl.BlockSpec((1, D_RET), lambda b: (0, 0)),
    ]
    args = [log_decay, proj, proj, proj, proj, gn.reshape(1, D_RET)]
    if has_state:
        in_specs.append(state_spec)
        args.append(state)
    out_specs = [pl.BlockSpec((seq_len, D_RET), lambda b: (b, 0))]
    out_shape = [jax.ShapeDtypeStruct((n, D_RET), BF16)]
    aliases = {}
    if state_out is not None:
        depth, buf = state_out
        out_specs.append(state_spec)
        out_shape.append(jax.ShapeDtypeStruct((batch, depth, 2, H_RET, DK_RET, DV_RET), F32))
        if buf is not None:
            aliases[len(args)] = 1
            in_specs.append(pl.BlockSpec(memory_space=pl.ANY))
            args.append(buf)
    kern = functools.partial(_ret_kernel, seq_len=seq_len, chunk=chunk, has_state=has_state,
                             emit_state=state_out is not None, carried=bool(aliases))
    return pl.pallas_call(
        kern,
        grid=(batch,),
        in_specs=in_specs,
        out_specs=out_specs,
        out_shape=out_shape,
        input_output_aliases=aliases,
        scratch_shapes=[
            pltpu.VMEM((seq_len, DV_RET), F32),
            pltpu.VMEM((H_RET, chunk, chunk), F32),
            pltpu.VMEM((H_RET, chunk, DK_RET), F32),
            pltpu.VMEM((H_RET, chunk, DK_RET), F32),
            pltpu.VMEM((H_RET, chunk, DK_RET), F32),
            pltpu.VMEM((H_RET, chunk, DK_RET), F32),
        ],
        compiler_params=_params(("arbitrary",)),
        name="retention",
    )(*args)


def _softmax_parts(scores, sink):
    m = jnp.maximum(sink, functools.reduce(
        jnp.maximum, [jnp.max(s, axis=-1, keepdims=True) for s in scores]))
    ps = [jnp.exp(s - m) for s in scores]
    denom = jnp.exp(sink - m) + functools.reduce(
        lambda a, b: a + b, [jnp.sum(p, axis=-1, keepdims=True) for p in ps])
    return ps, denom


def _ctx_attn_kernel(sink_ref, q_ref, k_ref, v_ref, o_ref):
    scale = DH_ATT ** -0.5
    for head in range(H_ATT):
        cols = slice(head * DH_ATT, (head + 1) * DH_ATT)
        kv_cols = slice(head // G_ATT * DH_ATT, (head // G_ATT + 1) * DH_ATT)
        s = lax.dot_general(q_ref[:, cols], k_ref[:, kv_cols], (((1,), (1,)), ((), ())),
                            preferred_element_type=F32) * scale
        (p,), denom = _softmax_parts([s], sink_ref[head])
        o = jnp.dot(p.astype(BF16), v_ref[:, kv_cols], preferred_element_type=F32)
        o_ref[:, cols] = (o / denom).astype(o_ref.dtype)


def _context_attention(proj, sink, seq_len):
    n = proj.shape[0]
    kvw = H_KV * DH_ATT
    return pl.pallas_call(
        _ctx_attn_kernel,
        grid=(n // seq_len,),
        in_specs=[
            pl.BlockSpec(memory_space=pltpu.SMEM),
            pl.BlockSpec((seq_len, D_ATT), lambda b: (b, OFF_QA // D_ATT)),
            pl.BlockSpec((seq_len, kvw), lambda b: (b, OFF_KA // kvw)),
            pl.BlockSpec((seq_len, kvw), lambda b: (b, OFF_VA // kvw)),
        ],
        out_specs=pl.BlockSpec((seq_len, D_ATT), lambda b: (b, 0)),
        out_shape=jax.ShapeDtypeStruct((n, D_ATT), BF16),
        compiler_params=_params(("arbitrary",)),
        name="context_attention",
    )(sink, proj, proj, proj)


def _rope(x, cos, sin_signed):
    quarter = DH_ATT // 4
    lane = lax.broadcasted_iota(jnp.int32, x.shape, 1)
    first = (lane % (2 * quarter)) < quarter
    partner = jnp.where(first, pltpu.roll(x, DH_ATT - quarter, axis=1), pltpu.roll(x, quarter, axis=1))
    return x * cos + partner * sin_signed


def _lat_attn_kernel(sink_ref, q_ref, k_ref, v_ref, ck_ref, cv_ref, cos_ref, sin_ref, o_ref,
                     kr_ref, *, seq_len):
    kv_head = pl.program_id(1)
    qb = pl.program_id(2)
    scale = DH_ATT ** -0.5
    span = BLOCK + 2 * WINDOW

    @pl.when(qb == 0)
    def _():
        kr_ref[...] = _rope(k_ref[...].astype(F32), cos_ref[...], sin_ref[...]).astype(BF16)

    q_rows = pl.ds(pl.multiple_of(qb * BLOCK, BLOCK), BLOCK)
    cos_q = cos_ref[q_rows, :]
    sin_q = sin_ref[q_rows, :]
    q4 = jnp.concatenate(
        [_rope(q_ref[:, g * DH_ATT:(g + 1) * DH_ATT].astype(F32), cos_q, sin_q) for g in range(G_ATT)],
        axis=0).astype(BF16)

    start = pl.multiple_of(jnp.clip(qb * BLOCK - WINDOW, 0, seq_len - span), BLOCK)
    k_win = kr_ref[pl.ds(start, span), :]
    v_win = v_ref[pl.ds(start, span), :]
    q_pos = qb * BLOCK + lax.broadcasted_iota(jnp.int32, (G_ATT * BLOCK, span), 0) % BLOCK
    k_pos = start + lax.broadcasted_iota(jnp.int32, (G_ATT * BLOCK, span), 1)
    band4 = jnp.abs(k_pos - q_pos) <= WINDOW

    nt = (((1,), (1,)), ((), ()))
    s_loc = lax.dot_general(q4, k_win, nt, preferred_element_type=F32) * scale
    s_loc = jnp.where(band4, s_loc, NEG_INF)
    s_ctx = lax.dot_general(q4, ck_ref[...].astype(BF16), nt, preferred_element_type=F32) * scale
    row_head = lax.broadcasted_iota(jnp.int32, (G_ATT * BLOCK, 1), 0) // BLOCK
    sink4 = jnp.full((G_ATT * BLOCK, 1), sink_ref[kv_head * G_ATT], F32)
    for g in range(1, G_ATT):
        sink4 = jnp.where(row_head == g, sink_ref[kv_head * G_ATT + g], sink4)
    (p_loc, p_ctx), denom = _softmax_parts([s_loc, s_ctx], sink4)
    o4 = (jnp.dot(p_loc.astype(BF16), v_win, preferred_element_type=F32)
          + jnp.dot(p_ctx.astype(BF16), cv_ref[...].astype(BF16), preferred_element_type=F32)) / denom
    for g in range(G_ATT):
        o_ref[:, g * DH_ATT:(g + 1) * DH_ATT] = o4[g * BLOCK:(g + 1) * BLOCK, :].astype(o_ref.dtype)


def _latent_attention(proj, sink, cache_k, cache_v, layer, seq_len, cos, sin_signed):
    n = proj.shape[0]
    batch = n // seq_len
    nq = seq_len // BLOCK
    qw = G_ATT * DH_ATT
    past = cache_k.shape[3]
    cache_spec = pl.BlockSpec((None, None, None, past, DH_ATT), lambda b, h, i: (b, layer, h, 0, 0))
    return pl.pallas_call(
        functools.partial(_lat_attn_kernel, seq_len=seq_len),
        grid=(batch, H_KV, nq),
        in_specs=[
            pl.BlockSpec(memory_space=pltpu.SMEM),
            pl.BlockSpec((BLOCK, qw), lambda b, h, i: (b * nq + i, OFF_QA // qw + h)),
            pl.BlockSpec((seq_len, DH_ATT), lambda b, h, i: (b, OFF_KA // DH_ATT + h)),
            pl.BlockSpec((seq_len, DH_ATT), lambda b, h, i: (b, OFF_VA // DH_ATT + h)),
            cache_spec, cache_spec,
            pl.BlockSpec((seq_len, DH_ATT), lambda b, h, i: (0, 0)),
            pl.BlockSpec((seq_len, DH_ATT), lambda b, h, i: (0, 0)),
        ],
        out_specs=pl.BlockSpec((BLOCK, qw), lambda b, h, i: (b * nq + i, h)),
        out_shape=jax.ShapeDtypeStruct((n, D_ATT), BF16),
        scratch_shapes=[pltpu.VMEM((seq_len, DH_ATT), BF16)],
        compiler_params=_params(("arbitrary", "arbitrary", "arbitrary")),
        name="latent_attention",
    )(sink, proj, proj, proj, cache_k, cache_v, cos, sin_signed)


def _rope_tables(seq_len):
    rows = seq_len // GRID_W
    row = jnp.repeat(jnp.arange(rows), GRID_W)
    col = jnp.tile(jnp.arange(GRID_W), rows)
    quarter = DH_ATT // 4
    inv = ROPE_BASE ** (-jnp.arange(quarter, dtype=F32) / quarter)
    ang_r, ang_c = row[:, None] * inv, col[:, None] * inv
    cos = jnp.concatenate([jnp.cos(ang_r)] * 2 + [jnp.cos(ang_c)] * 2, axis=-1)
    sin_signed = jnp.concatenate([-jnp.sin(ang_r), jnp.sin(ang_r), -jnp.sin(ang_c), jnp.sin(ang_c)], axis=-1)
    return cos.astype(F32), sin_signed.astype(F32)


def _out_proj_kernel(yr_ref, ya_ref, w_ref, x_ref, gate_ref, o_ref, y_ref):
    @pl.when(pl.program_id(1) == 0)
    def _():
        y_ref[:, :D_RET] = yr_ref[...]
        y_ref[:, D_RET:] = ya_ref[...]

    mix = jnp.dot(y_ref[...], w_ref[...], preferred_element_type=F32)
    o_ref[...] = x_ref[...] + gate_ref[...] * mix


def _out_project(y_r, y_a, x, mod4, w_out, layer, row0, row_step):
    n, d = x.shape
    d_mix = w_out.shape[1]
    tm, tn = TOKEN_TILE, OUT_COLS
    return pl.pallas_call(
        _out_proj_kernel,
        grid=(n // tm, d // tn),
        in_specs=[
            pl.BlockSpec((tm, D_RET), lambda i, j: (i, 0)),
            pl.BlockSpec((tm, D_ATT), lambda i, j: (i, 0)),
            pl.BlockSpec((None, d_mix, tn), lambda i, j: (layer, 0, j)),
            pl.BlockSpec((tm, tn), lambda i, j: (i, j)),
            pl.BlockSpec((None, None, 1, tn), lambda i, j: (layer, row0 + row_step * i, 0, 2 * (d // tn) + j)),
        ],
        out_specs=pl.BlockSpec((tm, tn), lambda i, j: (i, j)),
        out_shape=jax.ShapeDtypeStruct((n, d), F32),
        scratch_shapes=[pltpu.VMEM((tm, d_mix), BF16)],
        compiler_params=_params(("arbitrary", "arbitrary")),
        name="out_proj",
    )(y_r, y_a, w_out, x, mod4)


def _ffn_kernel(x_ref, shift_ref, scale_ref, gate_ref, wg_ref, wv_ref, cwg_ref, cwv_ref,
                cbg_ref, cbv_ref, wd_ref, o_ref, h_ref, *, seq_len):
    j = pl.program_id(1)
    rows = x_ref.shape[0]

    @pl.when(j == 0)
    def _():
        _modulated_norm_to(h_ref, x_ref, shift_ref, scale_ref)
        o_ref[...] = jnp.zeros_like(o_ref)

    cr = CONV_ROWS
    n_chunks = rows // cr
    t = lax.broadcasted_iota(jnp.int32, (cr, wg_ref.shape[1]), 0)
    first, last = t == 0, t == cr - 1
    ups = []
    for m in range(n_chunks):
        hm = h_ref[m * cr:(m + 1) * cr, :]
        ups.append(tuple(jnp.dot(hm, w_ref[...], preferred_element_type=F32) for w_ref in (wg_ref, wv_ref)))

    for m in range(n_chunks):
        def conv_branch(b, cw_ref, cb_ref):
            up = ups[m][b]
            before = ups[m - 1][b][cr - 1:cr, :] if (m * cr) % seq_len else 0.0
            after = ups[m + 1][b][0:1, :] if ((m + 1) * cr) % seq_len else 0.0
            prev = jnp.where(first, before, pltpu.roll(up, 1, axis=0))
            nxt = jnp.where(last, after, pltpu.roll(up, cr - 1, axis=0))
            return cb_ref[...] + prev * cw_ref[0:1, :] + up * cw_ref[1:2, :] + nxt * cw_ref[2:3, :]

        act = jax.nn.silu(conv_branch(0, cwg_ref, cbg_ref)) * conv_branch(1, cwv_ref, cbv_ref)
        o_ref[m * cr:(m + 1) * cr, :] += jnp.dot(act.astype(BF16), wd_ref[...], preferred_element_type=F32)

    @pl.when(j == pl.num_programs(1) - 1)
    def _():
        o_ref[...] = x_ref[...] + gate_ref[...] * o_ref[...]


def _conv_ffn(x, mod4, w_up, conv_w, conv_b, w_down, layer, row0, row_step, seq_len):
    n, d = x.shape
    d_ff = w_down.shape[1]
    tm, tf = TOKEN_TILE, FFN_COLS
    nf = d_ff // tf
    conv_b3 = conv_b.reshape(conv_b.shape[0], 1, 2 * d_ff)
    return pl.pallas_call(
        functools.partial(_ffn_kernel, seq_len=seq_len),
        grid=(n // tm, nf),
        in_specs=[
            pl.BlockSpec((tm, d), lambda i, j: (i, 0)),
            _mod_spec(layer, 3, d, row0, row_step),
            _mod_spec(layer, 4, d, row0, row_step),
            _mod_spec(layer, 5, d, row0, row_step),
            pl.BlockSpec((None, d, tf), lambda i, j: (layer, 0, j)),
            pl.BlockSpec((None, d, tf), lambda i, j: (layer, 0, nf + j)),
            pl.BlockSpec((None, CONV_W, tf), lambda i, j: (layer, 0, j)),
            pl.BlockSpec((None, CONV_W, tf), lambda i, j: (layer, 0, nf + j)),
            pl.BlockSpec((None, 1, tf), lambda i, j: (layer, 0, j)),
            pl.BlockSpec((None, 1, tf), lambda i, j: (layer, 0, nf + j)),
            pl.BlockSpec((None, tf, d), lambda i, j: (layer, j, 0)),
        ],
        out_specs=pl.BlockSpec((tm, d), lambda i, j: (i, 0)),
        out_shape=jax.ShapeDtypeStruct((n, d), F32),
        scratch_shapes=[pltpu.VMEM((tm, d), BF16)],
        compiler_params=_params(("arbitrary", "arbitrary")),
        name="conv_ffn",
    )(x, mod4, mod4, mod4, w_up, w_up, conv_w, conv_w, conv_b3, conv_b3, w_down)


def _final_norm_kernel(x_ref, gain_ref, o_ref):
    x = x_ref[...]
    o_ref[...] = x * lax.rsqrt(jnp.mean(x * x, axis=-1, keepdims=True) + EPS) * gain_ref[...]


def _final_norm(x, gain):
    n, d = x.shape
    tm = 256
    return pl.pallas_call(
        _final_norm_kernel,
        grid=(n // tm,),
        in_specs=[pl.BlockSpec((tm, d), lambda i: (i, 0)), pl.BlockSpec((1, d), lambda i: (0, 0))],
        out_specs=pl.BlockSpec((tm, d), lambda i: (i, 0)),
        out_shape=jax.ShapeDtypeStruct((n, d), F32),
        compiler_params=_params(("arbitrary",)),
        name="final_norm",
    )(x, gain.reshape(1, d))


def kernel(x_prompt, x_sample, cache_k, cache_v, state_ret, c, c_ctx, w_mod, b_mod, w_in, w_out,
           ret_log_decay, ret_gn, att_sink, w_up, conv_w, conv_b, w_down, final_gain):
    batch, seq, d = x_prompt.shape
    dec_batch, dec_seq, _ = x_sample.shape
    depth = w_in.shape[0]
    assert TOKEN_TILE % seq == 0 and dec_seq == TOKEN_TILE and 1 + dec_batch <= COND_ROWS
    assert w_in.shape[-1] == D_IN and d == D_RET + D_ATT

    cond = jnp.zeros((COND_ROWS, d), F32).at[0].set(c_ctx).at[1:1 + dec_batch].set(c)
    mod = _modulation(cond, w_mod, b_mod)
    mod4 = mod.reshape(depth, COND_ROWS, 1, N_MOD * d)
    cos, sin_signed = _rope_tables(dec_seq)
    w_in, w_out, w_up, w_down = (_to_bf16(w) for w in (w_in, w_out, w_up, w_down))

    xp = x_prompt.reshape(batch * seq, d)
    xs = x_sample.reshape(dec_batch * dec_seq, d)
    new_k = new_v = new_s = None
    for l in range(depth):
        proj, new_k, new_v = _project(xp, mod4, w_in, l, 0, 0, cache_out=(seq, depth, new_k, new_v))
        y_r, new_s = _retention(proj, ret_log_decay[l], ret_gn[l], seq, l, state_out=(depth, new_s))
        y_a = _context_attention(proj, att_sink[l], seq)
        xp = _out_project(y_r, y_a, xp, mod4, w_out, l, 0, 0)
        xp = _conv_ffn(xp, mod4, w_up, conv_w, conv_b, w_down, l, 0, 0, seq)
        (proj,) = _project(xs, mod4, w_in, l, 1, 1)
        (y_r,) = _retention(proj, ret_log_decay[l], ret_gn[l], dec_seq, l, state=state_ret)
        y_a = _latent_attention(proj, att_sink[l], cache_k, cache_v, l, dec_seq, cos, sin_signed)
        xs = _out_project(y_r, y_a, xs, mod4, w_out, l, 1, 1)
        xs = _conv_ffn(xs, mod4, w_up, conv_w, conv_b, w_down, l, 1, 1, dec_seq)

    y_prompt = _final_norm(xp, final_gain).reshape(batch, seq, d)
    y_sample = _final_norm(xs, final_gain).reshape(dec_batch, dec_seq, d)
    return (y_prompt, y_sample, new_k, new_v, new_s)
```

```python
import functools

import jax
import jax.numpy as jnp
from jax import lax
from jax.experimental import pallas as pl
from jax.experimental.pallas import tpu as pltpu

F32 = jnp.float32
BF16 = jnp.bfloat16

GRID_W = 64
H_RET = 8
DK_RET = 128
DV_RET = 128
D_RET = H_RET * DV_RET
H_ATT = 8
H_KV = 2
G_ATT = H_ATT // H_KV
DH_ATT = 128
D_ATT = H_ATT * DH_ATT
WINDOW = 128
BLOCK = 128
CONV_W = 3
N_MOD = 6
ROPE_BASE = 10000.0
EPS = 1e-6
NEG_INF = -1e30

OFF_QR = 0
OFF_KR = OFF_QR + H_RET * DK_RET
OFF_VR = OFF_KR + H_RET * DK_RET
OFF_GR = OFF_VR + H_RET * DV_RET
OFF_QA = OFF_GR + H_RET * DV_RET
OFF_KA = OFF_QA + H_ATT * DH_ATT
OFF_VA = OFF_KA + H_KV * DH_ATT
D_IN = OFF_VA + H_KV * DH_ATT
D_KV = 2 * H_KV * DH_ATT

COND_ROWS = 8
TOKEN_TILE = 1024
NORM_ROWS = 256
CONV_ROWS = 256
RET_CHUNK = 256
PROJ_COLS, PROJ_COLS_ROUNDING = 1408, 512
OUT_COLS, OUT_COLS_ROUNDING = 1024, 512
FFN_COLS, FFN_COLS_ROUNDING = 512, 256
V7X_VMEM_LIMIT = 58 * 1024 * 1024


def _params(n_axes):
    return pltpu.CompilerParams(dimension_semantics=("arbitrary",) * n_axes,
                                vmem_limit_bytes=V7X_VMEM_LIMIT)


def _call(body, name, grid, ins, outs, scratch, carried=(), **static):
    in_names = [n for n, _, _ in ins]
    in_specs = [s for _, s, _ in ins]
    args = [a for _, _, a in ins]
    out_names = [n for n, _, _ in outs]
    aliases = {}
    for out_name, arr in carried:
        if arr is not None:
            aliases[len(args)] = out_names.index(out_name)
            in_names.append("carried_" + out_name)
            in_specs.append(pl.BlockSpec(memory_space=pl.ANY))
            args.append(arr)
    names = tuple(in_names + out_names + [n for n, _ in scratch])

    def kern(*refs):
        body(dict(zip(names, refs)), **static)

    res = pl.pallas_call(
        kern,
        grid=grid,
        in_specs=in_specs,
        out_specs=[s for _, s, _ in outs],
        out_shape=[s for _, _, s in outs],
        input_output_aliases=aliases,
        scratch_shapes=[s for _, s in scratch],
        compiler_params=_params(len(grid)),
        name=name,
    )(*args)
    return dict(zip(out_names, res))


def _mod_kernel(cond_ref, w_ref, b_ref, o_ref):
    a = jax.nn.silu(cond_ref[...]).astype(BF16)
    o_ref[...] = jnp.dot(a, w_ref[...].astype(BF16), preferred_element_type=F32) + b_ref[...]


def _modulation(cond, w_mod, b_mod):
    depth, d, n = w_mod.shape
    tn = 1024
    return pl.pallas_call(
        _mod_kernel,
        grid=(depth, n // tn),
        in_specs=[
            pl.BlockSpec((COND_ROWS, d), lambda l, j: (0, 0)),
            pl.BlockSpec((None, d, tn), lambda l, j: (l, 0, j)),
            pl.BlockSpec((None, 1, tn), lambda l, j: (l, 0, j)),
        ],
        out_specs=pl.BlockSpec((None, COND_ROWS, tn), lambda l, j: (l, 0, j)),
        out_shape=jax.ShapeDtypeStruct((depth, COND_ROWS, n), F32),
        compiler_params=_params(2),
        name="modulation",
    )(cond, w_mod, b_mod.reshape(depth, 1, n))


class _Tiles:
    def __init__(self, first, count, row0, row_step):
        self.first, self.count, self.row0, self.row_step = first, count, row0, row_step

    def rows(self, width, full_width=None):
        if width == full_width:
            return lambda i, j: (self.first + i, 0)
        return lambda i, j: (self.first + i, j)

    def mod(self, mod4, layer, chunk, width, col_block=lambda j: 0):
        per_chunk = mod4.shape[-1] // N_MOD // width
        return pl.BlockSpec(
            (None, None, 1, width),
            lambda i, j: (layer, self.row0 + self.row_step * (self.first + i), 0, chunk * per_chunk + col_block(j)))


def _weight_tile(name, w, layer, block, index, rounding):
    if not rounding:
        return (name, pl.BlockSpec(block, index), w), None
    spec = pl.BlockSpec((None,) + block, lambda i, j: (layer,) + index(i, j))
    out = (name + "_bf", pl.BlockSpec(block, index), jax.ShapeDtypeStruct(w.shape[1:], BF16))
    return (name, spec, w), out


def _tile(r, name):
    w = r[name][...]
    if name + "_bf" in r:
        w = w.astype(BF16)
        r[name + "_bf"][...] = w
    return w


def _modulated_norm_to(h_ref, x_ref, shift_ref, scale_ref):
    rows = x_ref.shape[0]
    one_plus = 1.0 + scale_ref[...]
    shift = shift_ref[...]

    def body(c, carry):
        r = pl.ds(pl.multiple_of(c * NORM_ROWS, NORM_ROWS), NORM_ROWS)
        x = x_ref[r, :]
        inv = lax.rsqrt(jnp.mean(x * x, axis=-1, keepdims=True) + EPS)
        h_ref[r, :] = ((x * inv) * one_plus + shift).astype(h_ref.dtype)
        return carry

    lax.fori_loop(0, rows // NORM_ROWS, body, 0)


def _proj_body(r, *, kv_step):
    j = pl.program_id(1)

    @pl.when(j == 0)
    def _():
        _modulated_norm_to(r["h"], r["x"], r["shift"], r["scale"])

    res = jnp.dot(r["h"][...], _tile(r, "w"), preferred_element_type=F32)
    r["proj"][...] = res.astype(BF16)
    if "k" in r:
        seqs, _, seq_len, _ = r["k"].shape
        kv0 = res.shape[1] - D_KV

        @pl.when(j == kv_step)
        def _():
            for dst, off in ((r["k"], kv0), (r["v"], kv0 + H_KV * DH_ATT)):
                for s in range(seqs):
                    for hh in range(H_KV):
                        dst[s, hh] = res[s * seq_len:(s + 1) * seq_len,
                                         off + hh * DH_ATT:off + (hh + 1) * DH_ATT]


def _project(x, mod4, w, layer, tiles, rounding, proj=None, cache=None):
    n, d = x.shape
    d_in = w.shape[-1]
    tm = TOKEN_TILE
    tn = PROJ_COLS_ROUNDING if rounding else PROJ_COLS
    assert tn >= D_KV and d_in % tn == 0
    w_in, w_out = _weight_tile("w", w, layer, (d, tn), lambda i, j: (0, j), rounding)
    ins = [("x", pl.BlockSpec((tm, d), tiles.rows(d, d)), x),
           ("shift", tiles.mod(mod4, layer, 0, d), mod4),
           ("scale", tiles.mod(mod4, layer, 1, d), mod4),
           w_in]
    outs = [("proj", pl.BlockSpec((tm, tn), tiles.rows(tn, d_in)), jax.ShapeDtypeStruct((n, d_in), BF16))]
    carried = [("proj", proj)]
    if cache is not None:
        seq_len, depth, k_buf, v_buf = cache
        seqs = tm // seq_len
        shape = jax.ShapeDtypeStruct((n // seq_len, depth, H_KV, seq_len, DH_ATT), F32)
        spec = pl.BlockSpec((seqs, None, H_KV, seq_len, DH_ATT), lambda i, j: (tiles.first + i, layer, 0, 0, 0))
        outs += [("k", spec, shape), ("v", spec, shape)]
        carried += [("k", k_buf), ("v", v_buf)]
    if w_out:
        outs.append(w_out)
    return _call(_proj_body, "in_proj", (tiles.count, d_in // tn), ins, outs,
                 [("h", pltpu.VMEM((tm, d), BF16))], carried, kv_step=d_in // tn - 1)


def _ret_kernel(*refs, seq_len, chunk, has_state, emit_state, carried):
    it = iter(refs)
    ld_ref = next(it)
    q_ref, k_ref, v_ref, g_ref, gn_ref = (next(it) for _ in range(5))
    s0_ref = next(it) if has_state else None
    if carried:
        next(it)
    y_ref = next(it)
    snew_ref = next(it) if emit_state else None
    o_ref, intra_ref, qdf_ref, kdf_ref, qdb_ref, kdb_ref = (next(it) for _ in range(6))

    n_chunks = seq_len // chunk
    k_scale = DK_RET ** -0.5

    @pl.when(pl.program_id(0) == 0)
    def _():
        row = lax.broadcasted_iota(jnp.int32, (chunk, chunk), 0).astype(F32)
        col = lax.broadcasted_iota(jnp.int32, (chunk, chunk), 1).astype(F32)
        diff = row - col
        pos = lax.broadcasted_iota(jnp.int32, (chunk, DK_RET), 0).astype(F32)
        for h in range(H_RET):
            lg_f = ld_ref[0, h]
            lg_b = ld_ref[1, h]
            intra_ref[h] = (jnp.where(diff >= 0, jnp.exp(lg_f * jnp.maximum(diff, 0.0)), 0.0)
                            + jnp.where(diff <= 0, jnp.exp(lg_b * jnp.maximum(-diff, 0.0)), 0.0)) * k_scale
            qdf_ref[h] = jnp.exp(lg_f * (pos + 1.0))
            kdf_ref[h] = jnp.exp(lg_f * (chunk - 1.0 - pos)) * k_scale
            qdb_ref[h] = jnp.exp(lg_b * (chunk - pos))
            kdb_ref[h] = jnp.exp(lg_b * pos) * k_scale

    def rows(n):
        return pl.ds(n * chunk, chunk)

    def kv_update(k_b, dec, v_b):
        kd = (k_b.astype(F32) * dec).T.astype(BF16)
        return jnp.dot(kd, v_b, preferred_element_type=F32)

    for h in range(H_RET):
        cols = slice(h * DK_RET, (h + 1) * DK_RET)
        c_dec_f = jnp.exp(jnp.full((DK_RET, DV_RET), ld_ref[0, h] * chunk, F32))
        c_dec_b = jnp.exp(jnp.full((DK_RET, DV_RET), ld_ref[1, h] * chunk, F32))

        s_f = s0_ref[0, h] if has_state else jnp.zeros((DK_RET, DV_RET), F32)
        for n in range(n_chunks):
            q_b = q_ref[rows(n), cols]
            k_b = k_ref[rows(n), cols]
            v_b = v_ref[rows(n), cols]
            sc = lax.dot_general(q_b, k_b, (((1,), (1,)), ((), ())),
                                 preferred_element_type=F32) * intra_ref[h]
            o = jnp.dot(sc.astype(BF16), v_b, preferred_element_type=F32)
            if has_state or n > 0:
                o = o + jnp.dot(q_b, s_f.astype(BF16), preferred_element_type=F32) * qdf_ref[h]
            o_ref[rows(n), :] = o
            s_f = c_dec_f * s_f + kv_update(k_b, kdf_ref[h], v_b)

        s_b = s0_ref[1, h] if has_state else jnp.zeros((DK_RET, DV_RET), F32)
        for n in reversed(range(n_chunks)):
            q_b = q_ref[rows(n), cols]
            k_b = k_ref[rows(n), cols]
            v_b = v_ref[rows(n), cols]
            if has_state or n < n_chunks - 1:
                o_ref[rows(n), :] += jnp.dot(q_b, s_b.astype(BF16), preferred_element_type=F32) * qdb_ref[h]
            s_b = c_dec_b * s_b + kv_update(k_b, kdb_ref[h], v_b)

        if emit_state:
            snew_ref[0, h] = s_f
            snew_ref[1, h] = s_b

        gn = gn_ref[:, cols]
        for n in range(n_chunks):
            o = o_ref[rows(n), :]
            mu = jnp.mean(o, axis=-1, keepdims=True)
            cen = o - mu
            var = jnp.mean(cen * cen, axis=-1, keepdims=True)
            on = cen * lax.rsqrt(var + EPS) * gn
            y_ref[rows(n), cols] = (jax.nn.silu(g_ref[rows(n), cols].astype(F32)) * on).astype(y_ref.dtype)


def _retention(proj, log_decay, gn, seq_len, layer, state=None, state_out=None):
    n = proj.shape[0]
    batch = n // seq_len
    has_state = state is not None
    chunk = min(RET_CHUNK, seq_len)
    blk = lambda off: pl.BlockSpec((seq_len, D_RET), lambda b: (b, off // D_RET))
    state_spec = pl.BlockSpec((None, None, 2, H_RET, DK_RET, DV_RET), lambda b: (b, layer, 0, 0, 0, 0))
    in_specs = [
        pl.BlockSpec(memory_space=pltpu.SMEM),
        blk(OFF_QR), blk(OFF_KR), blk(OFF_VR), blk(OFF_GR),
        pl.BlockSpec((1, D_RET), lambda b: (0, 0)),
    ]
    args = [log_decay, proj, proj, proj, proj, gn.reshape(1, D_RET)]
    if has_state:
        in_specs.append(state_spec)
        args.append(state)
    out_specs = [pl.BlockSpec((seq_len, D_RET), lambda b: (b, 0))]
    out_shape = [jax.ShapeDtypeStruct((n, D_RET), BF16)]
    aliases = {}
    if state_out is not None:
        depth, buf = state_out
        out_specs.append(state_spec)
        out_shape.append(jax.ShapeDtypeStruct((batch, depth, 2, H_RET, DK_RET, DV_RET), F32))
        if buf is not None:
            aliases[len(args)] = 1
            in_specs.append(pl.BlockSpec(memory_space=pl.ANY))
            args.append(buf)
    kern = functools.partial(_ret_kernel, seq_len=seq_len, chunk=chunk, has_state=has_state,
                             emit_state=state_out is not None, carried=bool(aliases))
    return pl.pallas_call(
        kern,
        grid=(batch,),
        in_specs=in_specs,
        out_specs=out_specs,
        out_shape=out_shape,
        input_output_aliases=aliases,
        scratch_shapes=[
            pltpu.VMEM((seq_len, DV_RET), F32),
            pltpu.VMEM((H_RET, chunk, chunk), F32),
            pltpu.VMEM((H_RET, chunk, DK_RET), F32),
            pltpu.VMEM((H_RET, chunk, DK_RET), F32),
            pltpu.VMEM((H_RET, chunk, DK_RET), F32),
            pltpu.VMEM((H_RET, chunk, DK_RET), F32),
        ],
        compiler_params=_params(1),
        name="retention",
    )(*args)


def _softmax_parts(scores, sink):
    m = jnp.maximum(sink, functools.reduce(
        jnp.maximum, [jnp.max(s, axis=-1, keepdims=True) for s in scores]))
    ps = [jnp.exp(s - m) for s in scores]
    denom = jnp.exp(sink - m) + functools.reduce(
        lambda a, b: a + b, [jnp.sum(p, axis=-1, keepdims=True) for p in ps])
    return ps, denom


def _ctx_attn_kernel(sink_ref, q_ref, k_ref, v_ref, o_ref):
    scale = DH_ATT ** -0.5
    for head in range(H_ATT):
        cols = slice(head * DH_ATT, (head + 1) * DH_ATT)
        kv_cols = slice(head // G_ATT * DH_ATT, (head // G_ATT + 1) * DH_ATT)
        s = lax.dot_general(q_ref[:, cols], k_ref[:, kv_cols], (((1,), (1,)), ((), ())),
                            preferred_element_type=F32) * scale
        (p,), denom = _softmax_parts([s], sink_ref[head])
        o = jnp.dot(p.astype(BF16), v_ref[:, kv_cols], preferred_element_type=F32)
        o_ref[:, cols] = (o / denom).astype(o_ref.dtype)


def _context_attention(proj, sink, seq_len):
    n = proj.shape[0]
    kvw = H_KV * DH_ATT
    return pl.pallas_call(
        _ctx_attn_kernel,
        grid=(n // seq_len,),
        in_specs=[
            pl.BlockSpec(memory_space=pltpu.SMEM),
            pl.BlockSpec((seq_len, D_ATT), lambda b: (b, OFF_QA // D_ATT)),
            pl.BlockSpec((seq_len, kvw), lambda b: (b, OFF_KA // kvw)),
            pl.BlockSpec((seq_len, kvw), lambda b: (b, OFF_VA // kvw)),
        ],
        out_specs=pl.BlockSpec((seq_len, D_ATT), lambda b: (b, 0)),
        out_shape=jax.ShapeDtypeStruct((n, D_ATT), BF16),
        compiler_params=_params(1),
        name="context_attention",
    )(sink, proj, proj, proj)


def _rope(x, cos, sin_signed):
    quarter = DH_ATT // 4
    lane = lax.broadcasted_iota(jnp.int32, x.shape, 1)
    first = (lane % (2 * quarter)) < quarter
    partner = jnp.where(first, pltpu.roll(x, DH_ATT - quarter, axis=1), pltpu.roll(x, quarter, axis=1))
    return x * cos + partner * sin_signed


def _lat_attn_kernel(sink_ref, q_ref, k_ref, v_ref, ck_ref, cv_ref, cos_ref, sin_ref, o_ref,
                     kr_ref, *, seq_len):
    kv_head = pl.program_id(1)
    qb = pl.program_id(2)
    scale = DH_ATT ** -0.5
    span = BLOCK + 2 * WINDOW

    @pl.when(qb == 0)
    def _():
        kr_ref[...] = _rope(k_ref[...].astype(F32), cos_ref[...], sin_ref[...]).astype(BF16)

    q_rows = pl.ds(pl.multiple_of(qb * BLOCK, BLOCK), BLOCK)
    cos_q = cos_ref[q_rows, :]
    sin_q = sin_ref[q_rows, :]
    q4 = jnp.concatenate(
        [_rope(q_ref[:, g * DH_ATT:(g + 1) * DH_ATT].astype(F32), cos_q, sin_q) for g in range(G_ATT)],
        axis=0).astype(BF16)

    start = pl.multiple_of(jnp.clip(qb * BLOCK - WINDOW, 0, seq_len - span), BLOCK)
    k_win = kr_ref[pl.ds(start, span), :]
    v_win = v_ref[pl.ds(start, span), :]
    q_pos = qb * BLOCK + lax.broadcasted_iota(jnp.int32, (G_ATT * BLOCK, span), 0) % BLOCK
    k_pos = start + lax.broadcasted_iota(jnp.int32, (G_ATT * BLOCK, span), 1)
    band4 = jnp.abs(k_pos - q_pos) <= WINDOW

    nt = (((1,), (1,)), ((), ()))
    s_loc = lax.dot_general(q4, k_win, nt, preferred_element_type=F32) * scale
    s_loc = jnp.where(band4, s_loc, NEG_INF)
    s_ctx = lax.dot_general(q4, ck_ref[...].astype(BF16), nt, preferred_element_type=F32) * scale
    row_head = lax.broadcasted_iota(jnp.int32, (G_ATT * BLOCK, 1), 0) // BLOCK
    sink4 = jnp.full((G_ATT * BLOCK, 1), sink_ref[kv_head * G_ATT], F32)
    for g in range(1, G_ATT):
        sink4 = jnp.where(row_head == g, sink_ref[kv_head * G_ATT + g], sink4)
    (p_loc, p_ctx), denom = _softmax_parts([s_loc, s_ctx], sink4)
    o4 = (jnp.dot(p_loc.astype(BF16), v_win, preferred_element_type=F32)
          + jnp.dot(p_ctx.astype(BF16), cv_ref[...].astype(BF16), preferred_element_type=F32)) / denom
    for g in range(G_ATT):
        o_ref[:, g * DH_ATT:(g + 1) * DH_ATT] = o4[g * BLOCK:(g + 1) * BLOCK, :].astype(o_ref.dtype)


def _latent_attention(proj, sink, cache_k, cache_v, layer, seq_len, cos, sin_signed):
    n = proj.shape[0]
    batch = n // seq_len
    nq = seq_len // BLOCK
    qw = G_ATT * DH_ATT
    past = cache_k.shape[3]
    cache_spec = pl.BlockSpec((None, None, None, past, DH_ATT), lambda b, h, i: (b, layer, h, 0, 0))
    return pl.pallas_call(
        functools.partial(_lat_attn_kernel, seq_len=seq_len),
        grid=(batch, H_KV, nq),
        in_specs=[
            pl.BlockSpec(memory_space=pltpu.SMEM),
            pl.BlockSpec((BLOCK, qw), lambda b, h, i: (b * nq + i, OFF_QA // qw + h)),
            pl.BlockSpec((seq_len, DH_ATT), lambda b, h, i: (b, OFF_KA // DH_ATT + h)),
            pl.BlockSpec((seq_len, DH_ATT), lambda b, h, i: (b, OFF_VA // DH_ATT + h)),
            cache_spec, cache_spec,
            pl.BlockSpec((seq_len, DH_ATT), lambda b, h, i: (0, 0)),
            pl.BlockSpec((seq_len, DH_ATT), lambda b, h, i: (0, 0)),
        ],
        out_specs=pl.BlockSpec((BLOCK, qw), lambda b, h, i: (b * nq + i, h)),
        out_shape=jax.ShapeDtypeStruct((n, D_ATT), BF16),
        scratch_shapes=[pltpu.VMEM((seq_len, DH_ATT), BF16)],
        compiler_params=_params(3),
        name="latent_attention",
    )(sink, proj, proj, proj, cache_k, cache_v, cos, sin_signed)


def _rope_tables(seq_len):
    rows = seq_len // GRID_W
    row = jnp.repeat(jnp.arange(rows), GRID_W)
    col = jnp.tile(jnp.arange(GRID_W), rows)
    quarter = DH_ATT // 4
    inv = ROPE_BASE ** (-jnp.arange(quarter, dtype=F32) / quarter)
    ang_r, ang_c = row[:, None] * inv, col[:, None] * inv
    cos = jnp.concatenate([jnp.cos(ang_r)] * 2 + [jnp.cos(ang_c)] * 2, axis=-1)
    sin_signed = jnp.concatenate([-jnp.sin(ang_r), jnp.sin(ang_r), -jnp.sin(ang_c), jnp.sin(ang_c)], axis=-1)
    return cos.astype(F32), sin_signed.astype(F32)


def _out_proj_body(r):
    @pl.when(pl.program_id(1) == 0)
    def _():
        r["y"][:, :D_RET] = r["y_r"][...]
        r["y"][:, D_RET:] = r["y_a"][...]

    mix = jnp.dot(r["y"][...], _tile(r, "w"), preferred_element_type=F32)
    r["out"][...] = r["x"][...] + r["gate"][...] * mix


def _out_project(y_r, y_a, x, mod4, w, layer, tiles, rounding, out=None):
    n, d = x.shape
    tm = TOKEN_TILE
    tn = OUT_COLS_ROUNDING if rounding else OUT_COLS
    w_in, w_out = _weight_tile("w", w, layer, (D_RET + D_ATT, tn), lambda i, j: (0, j), rounding)
    ins = [("y_r", pl.BlockSpec((tm, D_RET), tiles.rows(D_RET, D_RET)), y_r),
           ("y_a", pl.BlockSpec((tm, D_ATT), tiles.rows(D_ATT, D_ATT)), y_a),
           w_in,
           ("x", pl.BlockSpec((tm, tn), tiles.rows(tn, d)), x),
           ("gate", tiles.mod(mod4, layer, 2, tn, lambda j: j), mod4)]
    outs = [("out", pl.BlockSpec((tm, tn), tiles.rows(tn, d)), jax.ShapeDtypeStruct((n, d), F32))]
    if w_out:
        outs.append(w_out)
    return _call(_out_proj_body, "out_proj", (tiles.count, d // tn), ins, outs,
                 [("y", pltpu.VMEM((tm, D_RET + D_ATT), BF16))], [("out", out)])


def _ffn_body(r, *, seq_len):
    j = pl.program_id(1)
    x_ref, o_ref, h_ref = r["x"], r["out"], r["h"]
    rows = x_ref.shape[0]

    @pl.when(j == 0)
    def _():
        _modulated_norm_to(h_ref, x_ref, r["shift"], r["scale"])
        o_ref[...] = jnp.zeros_like(o_ref)

    cr = CONV_ROWS
    n_chunks = rows // cr
    w_gate, w_val, w_down = _tile(r, "wg"), _tile(r, "wv"), _tile(r, "wd")
    t = lax.broadcasted_iota(jnp.int32, (cr, w_gate.shape[1]), 0)
    first, last = t == 0, t == cr - 1
    ups = []
    for m in range(n_chunks):
        hm = h_ref[m * cr:(m + 1) * cr, :]
        ups.append(tuple(jnp.dot(hm, w, preferred_element_type=F32) for w in (w_gate, w_val)))

    for m in range(n_chunks):
        def conv_branch(b, cw_ref, cb_ref):
            up = ups[m][b]
            before = ups[m - 1][b][cr - 1:cr, :] if (m * cr) % seq_len else 0.0
            after = ups[m + 1][b][0:1, :] if ((m + 1) * cr) % seq_len else 0.0
            prev = jnp.where(first, before, pltpu.roll(up, 1, axis=0))
            nxt = jnp.where(last, after, pltpu.roll(up, cr - 1, axis=0))
            return cb_ref[...] + prev * cw_ref[0:1, :] + up * cw_ref[1:2, :] + nxt * cw_ref[2:3, :]

        act = jax.nn.silu(conv_branch(0, r["cwg"], r["cbg"])) * conv_branch(1, r["cwv"], r["cbv"])
        o_ref[m * cr:(m + 1) * cr, :] += jnp.dot(act.astype(BF16), w_down, preferred_element_type=F32)

    @pl.when(j == pl.num_programs(1) - 1)
    def _():
        o_ref[...] = x_ref[...] + r["gate"][...] * o_ref[...]


def _conv_ffn(x, mod4, w_gate, w_val, conv_w, conv_b, w_down, layer, tiles, rounding, seq_len, out=None):
    n, d = x.shape
    d_ff = w_down.shape[-2]
    tm = TOKEN_TILE
    tf = FFN_COLS_ROUNDING if rounding else FFN_COLS
    nf = d_ff // tf
    val0 = nf if rounding else 0
    wg_in, wg_out = _weight_tile("wg", w_gate, layer, (d, tf), lambda i, j: (0, j), rounding)
    wv_in, wv_out = _weight_tile("wv", w_val, layer, (d, tf), lambda i, j: (0, val0 + j), rounding)
    wd_in, wd_out = _weight_tile("wd", w_down, layer, (tf, d), lambda i, j: (j, 0), rounding)
    if rounding:
        half = jax.ShapeDtypeStruct((d, d_ff), BF16)
        wg_out = (wg_out[0], wg_out[1], half)
        wv_out = (wv_out[0], pl.BlockSpec((d, tf), lambda i, j: (0, j)), half)
    conv_b3 = conv_b.reshape(conv_b.shape[0], 1, 2 * d_ff)
    x_spec = pl.BlockSpec((tm, d), tiles.rows(d, d), **({"pipeline_mode": pl.Buffered(1)} if rounding else {}))
    ins = [("x", x_spec, x),
           ("shift", tiles.mod(mod4, layer, 3, d), mod4),
           ("scale", tiles.mod(mod4, layer, 4, d), mod4),
           ("gate", tiles.mod(mod4, layer, 5, d), mod4),
           wg_in, wv_in,
           ("cwg", pl.BlockSpec((None, CONV_W, tf), lambda i, j: (layer, 0, j)), conv_w),
           ("cwv", pl.BlockSpec((None, CONV_W, tf), lambda i, j: (layer, 0, nf + j)), conv_w),
           ("cbg", pl.BlockSpec((None, 1, tf), lambda i, j: (layer, 0, j)), conv_b3),
           ("cbv", pl.BlockSpec((None, 1, tf), lambda i, j: (layer, 0, nf + j)), conv_b3),
           wd_in]
    outs = [("out", pl.BlockSpec((tm, d), tiles.rows(d, d)), jax.ShapeDtypeStruct((n, d), F32))]
    outs += [o for o in (wg_out, wv_out, wd_out) if o]
    return _call(_ffn_body, "conv_ffn", (tiles.count, nf), ins, outs,
                 [("h", pltpu.VMEM((tm, d), BF16))], [("out", out)], seq_len=seq_len)


def _final_norm_kernel(x_ref, gain_ref, o_ref):
    x = x_ref[...]
    o_ref[...] = x * lax.rsqrt(jnp.mean(x * x, axis=-1, keepdims=True) + EPS) * gain_ref[...]


def _final_norm(x, gain):
    n, d = x.shape
    tm = 256
    return pl.pallas_call(
        _final_norm_kernel,
        grid=(n // tm,),
        in_specs=[pl.BlockSpec((tm, d), lambda i: (i, 0)), pl.BlockSpec((1, d), lambda i: (0, 0))],
        out_specs=pl.BlockSpec((tm, d), lambda i: (i, 0)),
        out_shape=jax.ShapeDtypeStruct((n, d), F32),
        compiler_params=_params(1),
        name="final_norm",
    )(x, gain.reshape(1, d))


def kernel(x_prompt, x_sample, cache_k, cache_v, state_ret, c, c_ctx, w_mod, b_mod, w_in, w_out,
           ret_log_decay, ret_gn, att_sink, w_up, conv_w, conv_b, w_down, final_gain):
    batch, seq, d = x_prompt.shape
    dec_batch, dec_seq, _ = x_sample.shape
    depth = w_in.shape[0]
    assert TOKEN_TILE % seq == 0 and dec_seq == TOKEN_TILE and 1 + dec_batch <= COND_ROWS
    assert w_in.shape[-1] == D_IN and d == D_RET + D_ATT

    cond = jnp.zeros((COND_ROWS, d), F32).at[0].set(c_ctx).at[1:1 + dec_batch].set(c)
    mod = _modulation(cond, w_mod, b_mod)
    mod4 = mod.reshape(depth, COND_ROWS, 1, N_MOD * d)
    cos, sin_signed = _rope_tables(dec_seq)

    xp = x_prompt.reshape(batch * seq, d)
    xs = x_sample.reshape(dec_batch * dec_seq, d)
    n_ctx_tiles = batch * seq // TOKEN_TILE
    ctx_first = _Tiles(0, 1, 0, 0)
    ctx_rest = _Tiles(1, n_ctx_tiles - 1, 0, 0)
    lat = _Tiles(0, dec_batch, 1, 1)

    new_k = new_v = new_s = None
    for l in range(depth):
        o = _project(xp, mod4, w_in, l, ctx_first, True, cache=(seq, depth, new_k, new_v))
        w_in_l = o["w_bf"]
        o = _project(xp, mod4, w_in_l, l, ctx_rest, False, proj=o["proj"], cache=(seq, depth, o["k"], o["v"]))
        proj, new_k, new_v = o["proj"], o["k"], o["v"]
        y_r, new_s = _retention(proj, ret_log_decay[l], ret_gn[l], seq, l, state_out=(depth, new_s))
        y_a = _context_attention(proj, att_sink[l], seq)
        o = _out_project(y_r, y_a, xp, mod4, w_out, l, ctx_first, True)
        w_out_l = o["w_bf"]
        x1 = _out_project(y_r, y_a, xp, mod4, w_out_l, l, ctx_rest, False, out=o["out"])["out"]
        o = _conv_ffn(x1, mod4, w_up, w_up, conv_w, conv_b, w_down, l, ctx_first, True, seq)
        w_gate_l, w_val_l, w_down_l = o["wg_bf"], o["wv_bf"], o["wd_bf"]
        xp = _conv_ffn(x1, mod4, w_gate_l, w_val_l, conv_w, conv_b, w_down_l, l, ctx_rest, False, seq,
                       out=o["out"])["out"]
        proj = _project(xs, mod4, w_in_l, l, lat, False)["proj"]
        (y_r,) = _retention(proj, ret_log_decay[l], ret_gn[l], dec_seq, l, state=state_ret)
        y_a = _latent_attention(proj, att_sink[l], cache_k, cache_v, l, dec_seq, cos, sin_signed)
        x1 = _out_project(y_r, y_a, xs, mod4, w_out_l, l, lat, False)["out"]
        xs = _conv_ffn(x1, mod4, w_gate_l, w_val_l, conv_w, conv_b, w_down_l, l, lat, False, dec_seq)["out"]

    y_prompt = _final_norm(xp, final_gain).reshape(batch, seq, d)
    y_sample = _final_norm(xs, final_gain).reshape(dec_batch, dec_seq, d)
    return (y_prompt, y_sample, new_k, new_v, new_s)
```

```python
import functools

import jax
import jax.numpy as jnp
from jax import lax
from jax.experimental import pallas as pl
from jax.experimental.pallas import tpu as pltpu

F32 = jnp.float32
BF16 = jnp.bfloat16

GRID_W = 64
H_RET = 8
DK_RET = 128
DV_RET = 128
D_RET = H_RET * DV_RET
H_ATT = 8
H_KV = 2
G_ATT = H_ATT // H_KV
DH_ATT = 128
D_ATT = H_ATT * DH_ATT
WINDOW = 128
BLOCK = 128
CONV_W = 3
N_MOD = 6
ROPE_BASE = 10000.0
EPS = 1e-6
NEG_INF = -1e30

OFF_QR = 0
OFF_KR = OFF_QR + H_RET * DK_RET
OFF_VR = OFF_KR + H_RET * DK_RET
OFF_GR = OFF_VR + H_RET * DV_RET
OFF_QA = OFF_GR + H_RET * DV_RET
OFF_KA = OFF_QA + H_ATT * DH_ATT
OFF_VA = OFF_KA + H_KV * DH_ATT
D_IN = OFF_VA + H_KV * DH_ATT
D_KV = 2 * H_KV * DH_ATT

COND_ROWS = 8
TOKEN_TILE = 1024
NORM_ROWS = 256
CONV_ROWS = 512
RET_CHUNK = 256
PROJ_COLS, PROJ_COLS_ROUNDING = 1408, 512
OUT_COLS, OUT_COLS_ROUNDING = 1024, 512
FFN_COLS, FFN_COLS_ROUNDING = 512, 256
V7X_VMEM_LIMIT = 58 * 1024 * 1024


def _params(n_axes):
    return pltpu.CompilerParams(dimension_semantics=("arbitrary",) * n_axes,
                                vmem_limit_bytes=V7X_VMEM_LIMIT)


def _call(body, name, grid, ins, outs, scratch, carried=(), **static):
    in_names = [n for n, _, _ in ins]
    in_specs = [s for _, s, _ in ins]
    args = [a for _, _, a in ins]
    out_names = [n for n, _, _ in outs]
    aliases = {}
    for out_name, arr in carried:
        if arr is not None:
            aliases[len(args)] = out_names.index(out_name)
            in_names.append("carried_" + out_name)
            in_specs.append(pl.BlockSpec(memory_space=pl.ANY))
            args.append(arr)
    names = tuple(in_names + out_names + [n for n, _ in scratch])

    def kern(*refs):
        body(dict(zip(names, refs)), **static)

    res = pl.pallas_call(
        kern,
        grid=grid,
        in_specs=in_specs,
        out_specs=[s for _, s, _ in outs],
        out_shape=[s for _, _, s in outs],
        input_output_aliases=aliases,
        scratch_shapes=[s for _, s in scratch],
        compiler_params=_params(len(grid)),
        name=name,
    )(*args)
    return dict(zip(out_names, res))


def _mod_kernel(cond_ref, w_ref, b_ref, o_ref):
    a = jax.nn.silu(cond_ref[...]).astype(BF16)
    o_ref[...] = jnp.dot(a, w_ref[...].astype(BF16), preferred_element_type=F32) + b_ref[...]


def _modulation(cond, w_mod, b_mod):
    depth, d, n = w_mod.shape
    tn = 1024
    return pl.pallas_call(
        _mod_kernel,
        grid=(depth, n // tn),
        in_specs=[
            pl.BlockSpec((COND_ROWS, d), lambda l, j: (0, 0)),
            pl.BlockSpec((None, d, tn), lambda l, j: (l, 0, j)),
            pl.BlockSpec((None, 1, tn), lambda l, j: (l, 0, j)),
        ],
        out_specs=pl.BlockSpec((None, COND_ROWS, tn), lambda l, j: (l, 0, j)),
        out_shape=jax.ShapeDtypeStruct((depth, COND_ROWS, n), F32),
        compiler_params=_params(2),
        name="modulation",
    )(cond, w_mod, b_mod.reshape(depth, 1, n))


class _Tiles:
    def __init__(self, first, count, row0, row_step):
        self.first, self.count, self.row0, self.row_step = first, count, row0, row_step

    def rows(self, width, full_width=None):
        if width == full_width:
            return lambda i, j: (self.first + i, 0)
        return lambda i, j: (self.first + i, j)

    def mod(self, mod4, layer, chunk, width, col_block=lambda j: 0):
        per_chunk = mod4.shape[-1] // N_MOD // width
        return pl.BlockSpec(
            (None, None, 1, width),
            lambda i, j: (layer, self.row0 + self.row_step * (self.first + i), 0, chunk * per_chunk + col_block(j)))


def _weight_tile(name, w, layer, block, index, rounding):
    if not rounding:
        return (name, pl.BlockSpec(block, index), w), None
    spec = pl.BlockSpec((None,) + block, lambda i, j: (layer,) + index(i, j))
    out = (name + "_bf", pl.BlockSpec(block, index), jax.ShapeDtypeStruct(w.shape[1:], BF16))
    return (name, spec, w), out


def _tile(r, name):
    w = r[name][...]
    if name + "_bf" in r:
        w = w.astype(BF16)
        r[name + "_bf"][...] = w
    return w


def _modulated_norm_to(h_ref, x_ref, shift_ref, scale_ref):
    rows = x_ref.shape[0]
    one_plus = 1.0 + scale_ref[...]
    shift = shift_ref[...]

    def body(c, carry):
        r = pl.ds(pl.multiple_of(c * NORM_ROWS, NORM_ROWS), NORM_ROWS)
        x = x_ref[r, :]
        inv = lax.rsqrt(jnp.mean(x * x, axis=-1, keepdims=True) + EPS)
        h_ref[r, :] = ((x * inv) * one_plus + shift).astype(h_ref.dtype)
        return carry

    lax.fori_loop(0, rows // NORM_ROWS, body, 0)


def _proj_body(r, *, kv_step):
    j = pl.program_id(1)

    @pl.when(j == 0)
    def _():
        _modulated_norm_to(r["h"], r["x"], r["shift"], r["scale"])

    res = jnp.dot(r["h"][...], _tile(r, "w"), preferred_element_type=F32)
    r["proj"][...] = res.astype(BF16)
    if "k" in r:
        seqs, _, seq_len, _ = r["k"].shape
        kv0 = res.shape[1] - D_KV

        @pl.when(j == kv_step)
        def _():
            for dst, off in ((r["k"], kv0), (r["v"], kv0 + H_KV * DH_ATT)):
                for s in range(seqs):
                    for hh in range(H_KV):
                        dst[s, hh] = res[s * seq_len:(s + 1) * seq_len,
                                         off + hh * DH_ATT:off + (hh + 1) * DH_ATT]


def _project(x, mod4, w, layer, tiles, rounding, proj=None, cache=None):
    n, d = x.shape
    d_in = w.shape[-1]
    tm = TOKEN_TILE
    tn = PROJ_COLS_ROUNDING if rounding else PROJ_COLS
    assert tn >= D_KV and d_in % tn == 0
    w_in, w_out = _weight_tile("w", w, layer, (d, tn), lambda i, j: (0, j), rounding)
    ins = [("x", pl.BlockSpec((tm, d), tiles.rows(d, d)), x),
           ("shift", tiles.mod(mod4, layer, 0, d), mod4),
           ("scale", tiles.mod(mod4, layer, 1, d), mod4),
           w_in]
    outs = [("proj", pl.BlockSpec((tm, tn), tiles.rows(tn, d_in)), jax.ShapeDtypeStruct((n, d_in), BF16))]
    carried = [("proj", proj)]
    if cache is not None:
        seq_len, depth, k_buf, v_buf = cache
        seqs = tm // seq_len
        shape = jax.ShapeDtypeStruct((n // seq_len, depth, H_KV, seq_len, DH_ATT), F32)
        spec = pl.BlockSpec((seqs, None, H_KV, seq_len, DH_ATT), lambda i, j: (tiles.first + i, layer, 0, 0, 0))
        outs += [("k", spec, shape), ("v", spec, shape)]
        carried += [("k", k_buf), ("v", v_buf)]
    if w_out:
        outs.append(w_out)
    return _call(_proj_body, "in_proj", (tiles.count, d_in // tn), ins, outs,
                 [("h", pltpu.VMEM((tm, d), BF16))], carried, kv_step=d_in // tn - 1)


def _ret_kernel(*refs, seq_len, chunk, has_state, emit_state, carried):
    it = iter(refs)
    ld_ref = next(it)
    q_ref, k_ref, v_ref, g_ref, gn_ref = (next(it) for _ in range(5))
    s0_ref = next(it) if has_state else None
    if carried:
        next(it)
    y_ref = next(it)
    snew_ref = next(it) if emit_state else None
    o_ref, intra_ref, qdf_ref, kdf_ref, qdb_ref, kdb_ref = (next(it) for _ in range(6))

    n_chunks = seq_len // chunk
    k_scale = DK_RET ** -0.5

    @pl.when(pl.program_id(0) == 0)
    def _():
        row = lax.broadcasted_iota(jnp.int32, (chunk, chunk), 0).astype(F32)
        col = lax.broadcasted_iota(jnp.int32, (chunk, chunk), 1).astype(F32)
        diff = row - col
        pos = lax.broadcasted_iota(jnp.int32, (chunk, DK_RET), 0).astype(F32)
        for h in range(H_RET):
            lg_f = ld_ref[0, h]
            lg_b = ld_ref[1, h]
            intra_ref[h] = (jnp.where(diff >= 0, jnp.exp(lg_f * jnp.maximum(diff, 0.0)), 0.0)
                            + jnp.where(diff <= 0, jnp.exp(lg_b * jnp.maximum(-diff, 0.0)), 0.0)) * k_scale
            qdf_ref[h] = jnp.exp(lg_f * (pos + 1.0))
            kdf_ref[h] = jnp.exp(lg_f * (chunk - 1.0 - pos)) * k_scale
            qdb_ref[h] = jnp.exp(lg_b * (chunk - pos))
            kdb_ref[h] = jnp.exp(lg_b * pos) * k_scale

    def rows(n):
        return pl.ds(n * chunk, chunk)

    def kv_update(k_b, dec, v_b):
        kd = (k_b.astype(F32) * dec).T.astype(BF16)
        return jnp.dot(kd, v_b, preferred_element_type=F32)

    for h in range(H_RET):
        cols = slice(h * DK_RET, (h + 1) * DK_RET)
        c_dec_f = jnp.exp(jnp.full((DK_RET, DV_RET), ld_ref[0, h] * chunk, F32))
        c_dec_b = jnp.exp(jnp.full((DK_RET, DV_RET), ld_ref[1, h] * chunk, F32))

        s_f = s0_ref[0, h] if has_state else jnp.zeros((DK_RET, DV_RET), F32)
        for n in range(n_chunks):
            q_b = q_ref[rows(n), cols]
            k_b = k_ref[rows(n), cols]
            v_b = v_ref[rows(n), cols]
            sc = lax.dot_general(q_b, k_b, (((1,), (1,)), ((), ())),
                                 preferred_element_type=F32) * intra_ref[h]
            o = jnp.dot(sc.astype(BF16), v_b, preferred_element_type=F32)
            if has_state or n > 0:
                o = o + jnp.dot(q_b, s_f.astype(BF16), preferred_element_type=F32) * qdf_ref[h]
            o_ref[rows(n), :] = o
            s_f = c_dec_f * s_f + kv_update(k_b, kdf_ref[h], v_b)

        s_b = s0_ref[1, h] if has_state else jnp.zeros((DK_RET, DV_RET), F32)
        for n in reversed(range(n_chunks)):
            q_b = q_ref[rows(n), cols]
            k_b = k_ref[rows(n), cols]
            v_b = v_ref[rows(n), cols]
            if has_state or n < n_chunks - 1:
                o_ref[rows(n), :] += jnp.dot(q_b, s_b.astype(BF16), preferred_element_type=F32) * qdb_ref[h]
            s_b = c_dec_b * s_b + kv_update(k_b, kdb_ref[h], v_b)

        if emit_state:
            snew_ref[0, h] = s_f
            snew_ref[1, h] = s_b

        gn = gn_ref[:, cols]
        for n in range(n_chunks):
            o = o_ref[rows(n), :]
            mu = jnp.mean(o, axis=-1, keepdims=True)
            cen = o - mu
            var = jnp.mean(cen * cen, axis=-1, keepdims=True)
            on = cen * lax.rsqrt(var + EPS) * gn
            y_ref[rows(n), cols] = (jax.nn.silu(g_ref[rows(n), cols].astype(F32)) * on).astype(y_ref.dtype)


def _retention(proj, log_decay, gn, seq_len, layer, state=None, state_out=None):
    n = proj.shape[0]
    batch = n // seq_len
    has_state = state is not None
    chunk = min(RET_CHUNK, seq_len)
    blk = lambda off: pl.BlockSpec((seq_len, D_RET), lambda b: (b, off // D_RET))
    state_spec = pl.BlockSpec((None, None, 2, H_RET, DK_RET, DV_RET), lambda b: (b, layer, 0, 0, 0, 0))
    in_specs = [
        pl.BlockSpec(memory_space=pltpu.SMEM),
        blk(OFF_QR), blk(OFF_KR), blk(OFF_VR), blk(OFF_GR),
        pl.BlockSpec((1, D_RET), lambda b: (0, 0)),
    ]
    args = [log_decay, proj, proj, proj, proj, gn.reshape(1, D_RET)]
    if has_state:
        in_specs.append(state_spec)
        args.append(state)
    out_specs = [pl.BlockSpec((seq_len, D_RET), lambda b: (b, 0))]
    out_shape = [jax.ShapeDtypeStruct((n, D_RET), BF16)]
    aliases = {}
    if state_out is not None:
        depth, buf = state_out
        out_specs.append(state_spec)
        out_shape.append(jax.ShapeDtypeStruct((batch, depth, 2, H_RET, DK_RET, DV_RET), F32))
        if buf is not None:
            aliases[len(args)] = 1
            in_specs.append(pl.BlockSpec(memory_space=pl.ANY))
            args.append(buf)
    kern = functools.partial(_ret_kernel, seq_len=seq_len, chunk=chunk, has_state=has_state,
                             emit_state=state_out is not None, carried=bool(aliases))
    return pl.pallas_call(
        kern,
        grid=(batch,),
        in_specs=in_specs,
        out_specs=out_specs,
        out_shape=out_shape,
        input_output_aliases=aliases,
        scratch_shapes=[
            pltpu.VMEM((seq_len, DV_RET), F32),
            pltpu.VMEM((H_RET, chunk, chunk), F32),
            pltpu.VMEM((H_RET, chunk, DK_RET), F32),
            pltpu.VMEM((H_RET, chunk, DK_RET), F32),
            pltpu.VMEM((H_RET, chunk, DK_RET), F32),
            pltpu.VMEM((H_RET, chunk, DK_RET), F32),
        ],
        compiler_params=_params(1),
        name="retention",
    )(*args)


def _softmax_parts(scores, sink):
    m = jnp.maximum(sink, functools.reduce(
        jnp.maximum, [jnp.max(s, axis=-1, keepdims=True) for s in scores]))
    ps = [jnp.exp(s - m) for s in scores]
    denom = jnp.exp(sink - m) + functools.reduce(
        lambda a, b: a + b, [jnp.sum(p, axis=-1, keepdims=True) for p in ps])
    return ps, denom


def _ctx_attn_kernel(sink_ref, q_ref, k_ref, v_ref, o_ref):
    scale = DH_ATT ** -0.5
    for head in range(H_ATT):
        cols = slice(head * DH_ATT, (head + 1) * DH_ATT)
        kv_cols = slice(head // G_ATT * DH_ATT, (head // G_ATT + 1) * DH_ATT)
        s = lax.dot_general(q_ref[:, cols], k_ref[:, kv_cols], (((1,), (1,)), ((), ())),
                            preferred_element_type=F32) * scale
        (p,), denom = _softmax_parts([s], sink_ref[head])
        o = jnp.dot(p.astype(BF16), v_ref[:, kv_cols], preferred_element_type=F32)
        o_ref[:, cols] = (o / denom).astype(o_ref.dtype)


def _context_attention(proj, sink, seq_len):
    n = proj.shape[0]
    kvw = H_KV * DH_ATT
    return pl.pallas_call(
        _ctx_attn_kernel,
        grid=(n // seq_len,),
        in_specs=[
            pl.BlockSpec(memory_space=pltpu.SMEM),
            pl.BlockSpec((seq_len, D_ATT), lambda b: (b, OFF_QA // D_ATT)),
            pl.BlockSpec((seq_len, kvw), lambda b: (b, OFF_KA // kvw)),
            pl.BlockSpec((seq_len, kvw), lambda b: (b, OFF_VA // kvw)),
        ],
        out_specs=pl.BlockSpec((seq_len, D_ATT), lambda b: (b, 0)),
        out_shape=jax.ShapeDtypeStruct((n, D_ATT), BF16),
        compiler_params=_params(1),
        name="context_attention",
    )(sink, proj, proj, proj)


def _rope(x, cos, sin_signed):
    quarter = DH_ATT // 4
    lane = lax.broadcasted_iota(jnp.int32, x.shape, 1)
    first = (lane % (2 * quarter)) < quarter
    partner = jnp.where(first, pltpu.roll(x, DH_ATT - quarter, axis=1), pltpu.roll(x, quarter, axis=1))
    return x * cos + partner * sin_signed


def _lat_attn_kernel(sink_ref, q_ref, k_ref, v_ref, ck_ref, cv_ref, cos_ref, sin_ref, o_ref,
                     kr_ref, *, seq_len):
    kv_head = pl.program_id(1)
    qb = pl.program_id(2)
    scale = DH_ATT ** -0.5
    span = BLOCK + 2 * WINDOW

    @pl.when(qb == 0)
    def _():
        kr_ref[...] = _rope(k_ref[...].astype(F32), cos_ref[...], sin_ref[...]).astype(BF16)

    q_rows = pl.ds(pl.multiple_of(qb * BLOCK, BLOCK), BLOCK)
    cos_q = cos_ref[q_rows, :]
    sin_q = sin_ref[q_rows, :]
    q4 = jnp.concatenate(
        [_rope(q_ref[:, g * DH_ATT:(g + 1) * DH_ATT].astype(F32), cos_q, sin_q) for g in range(G_ATT)],
        axis=0).astype(BF16)

    start = pl.multiple_of(jnp.clip(qb * BLOCK - WINDOW, 0, seq_len - span), BLOCK)
    k_win = kr_ref[pl.ds(start, span), :]
    v_win = v_ref[pl.ds(start, span), :]
    q_pos = qb * BLOCK + lax.broadcasted_iota(jnp.int32, (BLOCK, span), 0)
    k_pos = start + lax.broadcasted_iota(jnp.int32, (BLOCK, span), 1)
    band = jnp.abs(k_pos - q_pos) <= WINDOW

    nt = (((1,), (1,)), ((), ()))
    s_loc = lax.dot_general(q4, k_win, nt, preferred_element_type=F32) * scale
    s_loc = jnp.where(band[None], s_loc.reshape(G_ATT, BLOCK, span), NEG_INF).reshape(G_ATT * BLOCK, span)
    s_ctx = lax.dot_general(q4, ck_ref[...].astype(BF16), nt, preferred_element_type=F32) * scale
    row_head = lax.broadcasted_iota(jnp.int32, (G_ATT * BLOCK, 1), 0) // BLOCK
    sink4 = jnp.full((G_ATT * BLOCK, 1), sink_ref[kv_head * G_ATT], F32)
    for g in range(1, G_ATT):
        sink4 = jnp.where(row_head == g, sink_ref[kv_head * G_ATT + g], sink4)
    (p_loc, p_ctx), denom = _softmax_parts([s_loc, s_ctx], sink4)
    o4 = (jnp.dot(p_loc.astype(BF16), v_win, preferred_element_type=F32)
          + jnp.dot(p_ctx.astype(BF16), cv_ref[...].astype(BF16), preferred_element_type=F32)) / denom
    for g in range(G_ATT):
        o_ref[:, g * DH_ATT:(g + 1) * DH_ATT] = o4[g * BLOCK:(g + 1) * BLOCK, :].astype(o_ref.dtype)


def _latent_attention(proj, sink, cache_k, cache_v, layer, seq_len, cos, sin_signed):
    n = proj.shape[0]
    batch = n // seq_len
    nq = seq_len // BLOCK
    qw = G_ATT * DH_ATT
    past = cache_k.shape[3]
    cache_spec = pl.BlockSpec((None, None, None, past, DH_ATT), lambda b, h, i: (b, layer, h, 0, 0))
    return pl.pallas_call(
        functools.partial(_lat_attn_kernel, seq_len=seq_len),
        grid=(batch, H_KV, nq),
        in_specs=[
            pl.BlockSpec(memory_space=pltpu.SMEM),
            pl.BlockSpec((BLOCK, qw), lambda b, h, i: (b * nq + i, OFF_QA // qw + h)),
            pl.BlockSpec((seq_len, DH_ATT), lambda b, h, i: (b, OFF_KA // DH_ATT + h)),
            pl.BlockSpec((seq_len, DH_ATT), lambda b, h, i: (b, OFF_VA // DH_ATT + h)),
            cache_spec, cache_spec,
            pl.BlockSpec((seq_len, DH_ATT), lambda b, h, i: (0, 0)),
            pl.BlockSpec((seq_len, DH_ATT), lambda b, h, i: (0, 0)),
        ],
        out_specs=pl.BlockSpec((BLOCK, qw), lambda b, h, i: (b * nq + i, h)),
        out_shape=jax.ShapeDtypeStruct((n, D_ATT), BF16),
        scratch_shapes=[pltpu.VMEM((seq_len, DH_ATT), BF16)],
        compiler_params=_params(3),
        name="latent_attention",
    )(sink, proj, proj, proj, cache_k, cache_v, cos, sin_signed)


def _rope_tables(seq_len):
    rows = seq_len // GRID_W
    row = jnp.repeat(jnp.arange(rows), GRID_W)
    col = jnp.tile(jnp.arange(GRID_W), rows)
    quarter = DH_ATT // 4
    inv = ROPE_BASE ** (-jnp.arange(quarter, dtype=F32) / quarter)
    ang_r, ang_c = row[:, None] * inv, col[:, None] * inv
    cos = jnp.concatenate([jnp.cos(ang_r)] * 2 + [jnp.cos(ang_c)] * 2, axis=-1)
    sin_signed = jnp.concatenate([-jnp.sin(ang_r), jnp.sin(ang_r), -jnp.sin(ang_c), jnp.sin(ang_c)], axis=-1)
    return cos.astype(F32), sin_signed.astype(F32)


def _out_proj_body(r):
    @pl.when(pl.program_id(1) == 0)
    def _():
        r["y"][:, :D_RET] = r["y_r"][...]
        r["y"][:, D_RET:] = r["y_a"][...]

    mix = jnp.dot(r["y"][...], _tile(r, "w"), preferred_element_type=F32)
    r["out"][...] = r["x"][...] + r["gate"][...] * mix


def _out_project(y_r, y_a, x, mod4, w, layer, tiles, rounding, out=None):
    n, d = x.shape
    tm = TOKEN_TILE
    tn = OUT_COLS_ROUNDING if rounding else OUT_COLS
    w_in, w_out = _weight_tile("w", w, layer, (D_RET + D_ATT, tn), lambda i, j: (0, j), rounding)
    ins = [("y_r", pl.BlockSpec((tm, D_RET), tiles.rows(D_RET, D_RET)), y_r),
           ("y_a", pl.BlockSpec((tm, D_ATT), tiles.rows(D_ATT, D_ATT)), y_a),
           w_in,
           ("x", pl.BlockSpec((tm, tn), tiles.rows(tn, d)), x),
           ("gate", tiles.mod(mod4, layer, 2, tn, lambda j: j), mod4)]
    outs = [("out", pl.BlockSpec((tm, tn), tiles.rows(tn, d)), jax.ShapeDtypeStruct((n, d), F32))]
    if w_out:
        outs.append(w_out)
    return _call(_out_proj_body, "out_proj", (tiles.count, d // tn), ins, outs,
                 [("y", pltpu.VMEM((tm, D_RET + D_ATT), BF16))], [("out", out)])


def _ffn_body(r, *, seq_len):
    j = pl.program_id(1)
    x_ref, o_ref, h_ref = r["x"], r["out"], r["h"]
    rows = x_ref.shape[0]

    @pl.when(j == 0)
    def _():
        _modulated_norm_to(h_ref, x_ref, r["shift"], r["scale"])
        o_ref[...] = jnp.zeros_like(o_ref)

    cr = CONV_ROWS
    n_chunks = rows // cr
    w_gate, w_val, w_down = _tile(r, "wg"), _tile(r, "wv"), _tile(r, "wd")
    t = lax.broadcasted_iota(jnp.int32, (cr, w_gate.shape[1]), 0)
    ups = []
    for m in range(n_chunks):
        hm = h_ref[m * cr:(m + 1) * cr, :]
        ups.append(tuple(jnp.dot(hm, w, preferred_element_type=F32) for w in (w_gate, w_val)))

    for m in range(n_chunks):
        seq_pos = (t + (m * cr) % seq_len) % seq_len
        seq_start, seq_end = seq_pos == 0, seq_pos == seq_len - 1

        def conv_branch(b, cw_ref, cb_ref):
            up = ups[m][b]
            prev = pltpu.roll(up, 1, axis=0)
            nxt = pltpu.roll(up, cr - 1, axis=0)
            if (m * cr) % seq_len:
                prev = jnp.where(t == 0, ups[m - 1][b][cr - 1:cr, :], prev)
            if ((m + 1) * cr) % seq_len:
                nxt = jnp.where(t == cr - 1, ups[m + 1][b][0:1, :], nxt)
            prev = jnp.where(seq_start, 0.0, prev)
            nxt = jnp.where(seq_end, 0.0, nxt)
            return cb_ref[...] + prev * cw_ref[0:1, :] + up * cw_ref[1:2, :] + nxt * cw_ref[2:3, :]

        act = jax.nn.silu(conv_branch(0, r["cwg"], r["cbg"])) * conv_branch(1, r["cwv"], r["cbv"])
        o_ref[m * cr:(m + 1) * cr, :] += jnp.dot(act.astype(BF16), w_down, preferred_element_type=F32)

    @pl.when(j == pl.num_programs(1) - 1)
    def _():
        if "gain" not in r:
            o_ref[...] = x_ref[...] + r["gate"][...] * o_ref[...]
            return
        gate, gain = r["gate"][...], r["gain"][...]

        def body(c, carry):
            rs = pl.ds(pl.multiple_of(c * NORM_ROWS, NORM_ROWS), NORM_ROWS)
            y = x_ref[rs, :] + gate * o_ref[rs, :]
            o_ref[rs, :] = y * lax.rsqrt(jnp.mean(y * y, axis=-1, keepdims=True) + EPS) * gain
            return carry

        lax.fori_loop(0, rows // NORM_ROWS, body, 0)


def _conv_ffn(x, mod4, w_gate, w_val, conv_w, conv_b, w_down, layer, tiles, rounding, seq_len,
              gain=None, out=None):
    n, d = x.shape
    d_ff = w_down.shape[-2]
    tm = TOKEN_TILE
    tf = FFN_COLS_ROUNDING if rounding else FFN_COLS
    nf = d_ff // tf
    val0 = nf if rounding else 0
    wg_in, wg_out = _weight_tile("wg", w_gate, layer, (d, tf), lambda i, j: (0, j), rounding)
    wv_in, wv_out = _weight_tile("wv", w_val, layer, (d, tf), lambda i, j: (0, val0 + j), rounding)
    wd_in, wd_out = _weight_tile("wd", w_down, layer, (tf, d), lambda i, j: (j, 0), rounding)
    if rounding:
        half = jax.ShapeDtypeStruct((d, d_ff), BF16)
        wg_out = (wg_out[0], wg_out[1], half)
        wv_out = (wv_out[0], pl.BlockSpec((d, tf), lambda i, j: (0, j)), half)
    conv_b3 = conv_b.reshape(conv_b.shape[0], 1, 2 * d_ff)
    x_spec = pl.BlockSpec((tm, d), tiles.rows(d, d), **({"pipeline_mode": pl.Buffered(1)} if rounding else {}))
    ins = [("x", x_spec, x),
           ("shift", tiles.mod(mod4, layer, 3, d), mod4),
           ("scale", tiles.mod(mod4, layer, 4, d), mod4),
           ("gate", tiles.mod(mod4, layer, 5, d), mod4),
           wg_in, wv_in,
           ("cwg", pl.BlockSpec((None, CONV_W, tf), lambda i, j: (layer, 0, j)), conv_w),
           ("cwv", pl.BlockSpec((None, CONV_W, tf), lambda i, j: (layer, 0, nf + j)), conv_w),
           ("cbg", pl.BlockSpec((None, 1, tf), lambda i, j: (layer, 0, j)), conv_b3),
           ("cbv", pl.BlockSpec((None, 1, tf), lambda i, j: (layer, 0, nf + j)), conv_b3),
           wd_in]
    if gain is not None:
        ins.append(("gain", pl.BlockSpec((1, d), lambda i, j: (0, 0)), gain.reshape(1, d)))
    outs = [("out", pl.BlockSpec((tm, d), tiles.rows(d, d)), jax.ShapeDtypeStruct((n, d), F32))]
    outs += [o for o in (wg_out, wv_out, wd_out) if o]
    return _call(_ffn_body, "conv_ffn", (tiles.count, nf), ins, outs,
                 [("h", pltpu.VMEM((tm, d), BF16))], [("out", out)], seq_len=seq_len)


def kernel(x_prompt, x_sample, cache_k, cache_v, state_ret, c, c_ctx, w_mod, b_mod, w_in, w_out,
           ret_log_decay, ret_gn, att_sink, w_up, conv_w, conv_b, w_down, final_gain):
    batch, seq, d = x_prompt.shape
    dec_batch, dec_seq, _ = x_sample.shape
    depth = w_in.shape[0]
    assert TOKEN_TILE % seq == 0 and dec_seq == TOKEN_TILE and 1 + dec_batch <= COND_ROWS
    assert w_in.shape[-1] == D_IN and d == D_RET + D_ATT

    cond = jnp.zeros((COND_ROWS, d), F32).at[0].set(c_ctx).at[1:1 + dec_batch].set(c)
    mod = _modulation(cond, w_mod, b_mod)
    mod4 = mod.reshape(depth, COND_ROWS, 1, N_MOD * d)
    cos, sin_signed = _rope_tables(dec_seq)

    xp = x_prompt.reshape(batch * seq, d)
    xs = x_sample.reshape(dec_batch * dec_seq, d)
    n_ctx_tiles = batch * seq // TOKEN_TILE
    ctx_first = _Tiles(0, 1, 0, 0)
    ctx_rest = _Tiles(1, n_ctx_tiles - 1, 0, 0)
    lat = _Tiles(0, dec_batch, 1, 1)

    new_k = new_v = new_s = None
    for l in range(depth):
        o = _project(xp, mod4, w_in, l, ctx_first, True, cache=(seq, depth, new_k, new_v))
        w_in_l = o["w_bf"]
        o = _project(xp, mod4, w_in_l, l, ctx_rest, False, proj=o["proj"], cache=(seq, depth, o["k"], o["v"]))
        proj, new_k, new_v = o["proj"], o["k"], o["v"]
        y_r, new_s = _retention(proj, ret_log_decay[l], ret_gn[l], seq, l, state_out=(depth, new_s))
        y_a = _context_attention(proj, att_sink[l], seq)
        o = _out_project(y_r, y_a, xp, mod4, w_out, l, ctx_first, True)
        w_out_l = o["w_bf"]
        x1 = _out_project(y_r, y_a, xp, mod4, w_out_l, l, ctx_rest, False, out=o["out"])["out"]
        gain = final_gain if l == depth - 1 else None
        o = _conv_ffn(x1, mod4, w_up, w_up, conv_w, conv_b, w_down, l, ctx_first, True, seq, gain=gain)
        w_gate_l, w_val_l, w_down_l = o["wg_bf"], o["wv_bf"], o["wd_bf"]
        xp = _conv_ffn(x1, mod4, w_gate_l, w_val_l, conv_w, conv_b, w_down_l, l, ctx_rest, False, seq,
                       gain=gain, out=o["out"])["out"]
        proj = _project(xs, mod4, w_in_l, l, lat, False)["proj"]
        (y_r,) = _retention(proj, ret_log_decay[l], ret_gn[l], dec_seq, l, state=state_ret)
        y_a = _latent_attention(proj, att_sink[l], cache_k, cache_v, l, dec_seq, cos, sin_signed)
        x1 = _out_project(y_r, y_a, xs, mod4, w_out_l, l, lat, False)["out"]
        xs = _conv_ffn(x1, mod4, w_gate_l, w_val_l, conv_w, conv_b, w_down_l, l, lat, False, dec_seq,
                       gain=gain)["out"]

    return (xp.reshape(batch, seq, d), xs.reshape(dec_batch, dec_seq, d), new_k, new_v, new_s)
```

```python
import functools

import jax
import jax.numpy as jnp
from jax import lax
from jax.experimental import pallas as pl
from jax.experimental.pallas import tpu as pltpu

F32 = jnp.float32
BF16 = jnp.bfloat16

GRID_W = 64
H_RET = 8
DK_RET = 128
DV_RET = 128
D_RET = H_RET * DV_RET
H_ATT = 8
H_KV = 2
G_ATT = H_ATT // H_KV
DH_ATT = 128
D_ATT = H_ATT * DH_ATT
WINDOW = 128
BLOCK = 128
CONV_W = 3
N_MOD = 6
ROPE_BASE = 10000.0
EPS = 1e-6
NEG_INF = -1e30

OFF_QR = 0
OFF_KR = OFF_QR + H_RET * DK_RET
OFF_VR = OFF_KR + H_RET * DK_RET
OFF_GR = OFF_VR + H_RET * DV_RET
OFF_QA = OFF_GR + H_RET * DV_RET
OFF_KA = OFF_QA + H_ATT * DH_ATT
OFF_VA = OFF_KA + H_KV * DH_ATT
D_IN = OFF_VA + H_KV * DH_ATT
D_KV = 2 * H_KV * DH_ATT

COND_ROWS = 8
TOKEN_TILE = 1024
NORM_ROWS = 256
CONV_ROWS = 512
RET_CHUNK = 256
PROJ_COLS, PROJ_COLS_ROUNDING = 1408, 512
OUT_COLS, OUT_COLS_ROUNDING = 1024, 512
FFN_COLS, FFN_COLS_ROUNDING = 512, 256
V7X_VMEM_LIMIT = 62 * 1024 * 1024


def _params(n_axes):
    return pltpu.CompilerParams(dimension_semantics=("arbitrary",) * n_axes,
                                vmem_limit_bytes=V7X_VMEM_LIMIT)


def _call(body, name, grid, ins, outs, scratch, carried=(), **static):
    in_names = [n for n, _, _ in ins]
    in_specs = [s for _, s, _ in ins]
    args = [a for _, _, a in ins]
    out_names = [n for n, _, _ in outs]
    aliases = {}
    for out_name, arr in carried:
        if arr is not None:
            aliases[len(args)] = out_names.index(out_name)
            in_names.append("carried_" + out_name)
            in_specs.append(pl.BlockSpec(memory_space=pl.ANY))
            args.append(arr)
    names = tuple(in_names + out_names + [n for n, _ in scratch])

    def kern(*refs):
        body(dict(zip(names, refs)), **static)

    res = pl.pallas_call(
        kern,
        grid=grid,
        in_specs=in_specs,
        out_specs=[s for _, s, _ in outs],
        out_shape=[s for _, _, s in outs],
        input_output_aliases=aliases,
        scratch_shapes=[s for _, s in scratch],
        compiler_params=_params(len(grid)),
        name=name,
    )(*args)
    return dict(zip(out_names, res))


def _mod_kernel(cond_ref, w_ref, b_ref, o_ref):
    a = jax.nn.silu(cond_ref[...]).astype(BF16)
    o_ref[...] = jnp.dot(a, w_ref[...].astype(BF16), preferred_element_type=F32) + b_ref[...]


def _modulation(cond, w_mod, b_mod):
    depth, d, n = w_mod.shape
    tn = 1024
    return pl.pallas_call(
        _mod_kernel,
        grid=(depth, n // tn),
        in_specs=[
            pl.BlockSpec((COND_ROWS, d), lambda l, j: (0, 0)),
            pl.BlockSpec((None, d, tn), lambda l, j: (l, 0, j)),
            pl.BlockSpec((None, 1, tn), lambda l, j: (l, 0, j)),
        ],
        out_specs=pl.BlockSpec((None, COND_ROWS, tn), lambda l, j: (l, 0, j)),
        out_shape=jax.ShapeDtypeStruct((depth, COND_ROWS, n), F32),
        compiler_params=_params(2),
        name="modulation",
    )(cond, w_mod, b_mod.reshape(depth, 1, n))


class _Tiles:
    def __init__(self, first, count, row0, row_step):
        self.first, self.count, self.row0, self.row_step = first, count, row0, row_step

    def rows(self, width, full_width=None):
        if width == full_width:
            return lambda i, j: (self.first + i, 0)
        return lambda i, j: (self.first + i, j)

    def mod(self, mod4, layer, chunk, width, col_block=lambda j: 0):
        per_chunk = mod4.shape[-1] // N_MOD // width
        return pl.BlockSpec(
            (None, None, 1, width),
            lambda i, j: (layer, self.row0 + self.row_step * (self.first + i), 0, chunk * per_chunk + col_block(j)))


def _weight_tile(name, w, layer, block, index, rounding):
    if not rounding:
        return (name, pl.BlockSpec(block, index), w), None
    spec = pl.BlockSpec((None,) + block, lambda i, j: (layer,) + index(i, j))
    out = (name + "_bf", pl.BlockSpec(block, index), jax.ShapeDtypeStruct(w.shape[1:], BF16))
    return (name, spec, w), out


def _tile(r, name):
    w = r[name][...]
    if name + "_bf" in r:
        w = w.astype(BF16)
        r[name + "_bf"][...] = w
    return w


def _modulated_norm(x, scale_ref, shift_ref):
    inv = lax.rsqrt(jnp.mean(x * x, axis=-1, keepdims=True) + EPS)
    return ((x * inv) * (1.0 + scale_ref[...]) + shift_ref[...]).astype(BF16)


def _proj_body(r, *, kv_step):
    j = pl.program_id(1)
    h_ref, proj_ref = r["h"], r["proj"]
    w = _tile(r, "w")

    @pl.when(j == 0)
    def _():
        for c in range(h_ref.shape[0] // NORM_ROWS):
            rs = slice(c * NORM_ROWS, (c + 1) * NORM_ROWS)
            h = _modulated_norm(r["x"][rs, :], r["scale"], r["shift"])
            h_ref[rs, :] = h
            proj_ref[rs, :] = jnp.dot(h, w, preferred_element_type=F32).astype(BF16)

    @pl.when(j > 0)
    def _():
        res = jnp.dot(h_ref[...], w, preferred_element_type=F32)
        proj_ref[...] = res.astype(BF16)
        if "k" in r:
            seqs, _, seq_len, _ = r["k"].shape
            kv0 = res.shape[1] - D_KV

            @pl.when(j == kv_step)
            def _():
                for dst, off in ((r["k"], kv0), (r["v"], kv0 + H_KV * DH_ATT)):
                    for s in range(seqs):
                        for hh in range(H_KV):
                            dst[s, hh] = res[s * seq_len:(s + 1) * seq_len,
                                             off + hh * DH_ATT:off + (hh + 1) * DH_ATT]


def _project(x, mod4, w, layer, tiles, rounding, proj=None, cache=None):
    n, d = x.shape
    d_in = w.shape[-1]
    tm = TOKEN_TILE
    tn = PROJ_COLS_ROUNDING if rounding else PROJ_COLS
    assert tn >= D_KV and d_in % tn == 0
    w_in, w_out = _weight_tile("w", w, layer, (d, tn), lambda i, j: (0, j), rounding)
    ins = [("x", pl.BlockSpec((tm, d), tiles.rows(d, d)), x),
           ("shift", tiles.mod(mod4, layer, 0, d), mod4),
           ("scale", tiles.mod(mod4, layer, 1, d), mod4),
           w_in]
    outs = [("proj", pl.BlockSpec((tm, tn), tiles.rows(tn, d_in)), jax.ShapeDtypeStruct((n, d_in), BF16))]
    carried = [("proj", proj)]
    if cache is not None:
        seq_len, depth, k_buf, v_buf = cache
        seqs = tm // seq_len
        shape = jax.ShapeDtypeStruct((n // seq_len, depth, H_KV, seq_len, DH_ATT), F32)
        spec = pl.BlockSpec((seqs, None, H_KV, seq_len, DH_ATT), lambda i, j: (tiles.first + i, layer, 0, 0, 0))
        outs += [("k", spec, shape), ("v", spec, shape)]
        carried += [("k", k_buf), ("v", v_buf)]
    if w_out:
        outs.append(w_out)
    return _call(_proj_body, "in_proj", (tiles.count, d_in // tn), ins, outs,
                 [("h", pltpu.VMEM((tm, d), BF16))], carried, kv_step=d_in // tn - 1)


def _ret_kernel(*refs, seq_len, chunk, has_state, emit_state, carried):
    it = iter(refs)
    ld_ref = next(it)
    q_ref, k_ref, v_ref, g_ref, gn_ref = (next(it) for _ in range(5))
    s0_ref = next(it) if has_state else None
    if carried:
        next(it)
    y_ref = next(it)
    snew_ref = next(it) if emit_state else None
    o_ref, intra_ref, qdf_ref, kdf_ref, qdb_ref, kdb_ref = (next(it) for _ in range(6))

    n_chunks = seq_len // chunk
    k_scale = DK_RET ** -0.5

    @pl.when(pl.program_id(0) == 0)
    def _():
        row = lax.broadcasted_iota(jnp.int32, (chunk, chunk), 0).astype(F32)
        col = lax.broadcasted_iota(jnp.int32, (chunk, chunk), 1).astype(F32)
        diff = row - col
        pos = lax.broadcasted_iota(jnp.int32, (chunk, DK_RET), 0).astype(F32)
        for h in range(H_RET):
            lg_f = ld_ref[0, h]
            lg_b = ld_ref[1, h]
            intra_ref[h] = (jnp.where(diff >= 0, jnp.exp(lg_f * jnp.maximum(diff, 0.0)), 0.0)
                            + jnp.where(diff <= 0, jnp.exp(lg_b * jnp.maximum(-diff, 0.0)), 0.0)) * k_scale
            qdf_ref[h] = jnp.exp(lg_f * (pos + 1.0))
            kdf_ref[h] = jnp.exp(lg_f * (chunk - 1.0 - pos)) * k_scale
            qdb_ref[h] = jnp.exp(lg_b * (chunk - pos))
            kdb_ref[h] = jnp.exp(lg_b * pos) * k_scale

    def rows(n):
        return pl.ds(n * chunk, chunk)

    def kv_update(k_b, dec, v_b):
        kd = (k_b.astype(F32) * dec).T.astype(BF16)
        return jnp.dot(kd, v_b, preferred_element_type=F32)

    for h in range(H_RET):
        cols = slice(h * DK_RET, (h + 1) * DK_RET)
        c_dec_f = jnp.exp(jnp.full((DK_RET, DV_RET), ld_ref[0, h] * chunk, F32))
        c_dec_b = jnp.exp(jnp.full((DK_RET, DV_RET), ld_ref[1, h] * chunk, F32))

        s_f = s0_ref[0, h] if has_state else jnp.zeros((DK_RET, DV_RET), F32)
        for n in range(n_chunks):
            q_b = q_ref[rows(n), cols]
            k_b = k_ref[rows(n), cols]
            v_b = v_ref[rows(n), cols]
            sc = lax.dot_general(q_b, k_b, (((1,), (1,)), ((), ())),
                                 preferred_element_type=F32) * intra_ref[h]
            o = jnp.dot(sc.astype(BF16), v_b, preferred_element_type=F32)
            if has_state or n > 0:
                o = o + jnp.dot(q_b, s_f.astype(BF16), preferred_element_type=F32) * qdf_ref[h]
            o_ref[rows(n), :] = o
            s_f = c_dec_f * s_f + kv_update(k_b, kdf_ref[h], v_b)

        s_b = s0_ref[1, h] if has_state else jnp.zeros((DK_RET, DV_RET), F32)
        for n in reversed(range(n_chunks)):
            q_b = q_ref[rows(n), cols]
            k_b = k_ref[rows(n), cols]
            v_b = v_ref[rows(n), cols]
            if has_state or n < n_chunks - 1:
                o_ref[rows(n), :] += jnp.dot(q_b, s_b.astype(BF16), preferred_element_type=F32) * qdb_ref[h]
            s_b = c_dec_b * s_b + kv_update(k_b, kdb_ref[h], v_b)

        if emit_state:
            snew_ref[0, h] = s_f
            snew_ref[1, h] = s_b

        gn = gn_ref[:, cols]
        for n in range(n_chunks):
            o = o_ref[rows(n), :]
            mu = jnp.mean(o, axis=-1, keepdims=True)
            cen = o - mu
            var = jnp.mean(cen * cen, axis=-1, keepdims=True)
            on = cen * lax.rsqrt(var + EPS) * gn
            y_ref[rows(n), cols] = (jax.nn.silu(g_ref[rows(n), cols].astype(F32)) * on).astype(y_ref.dtype)


def _retention(proj, log_decay, gn, seq_len, layer, state=None, state_out=None):
    n = proj.shape[0]
    batch = n // seq_len
    has_state = state is not None
    chunk = min(RET_CHUNK, seq_len)
    blk = lambda off: pl.BlockSpec((seq_len, D_RET), lambda b: (b, off // D_RET))
    state_spec = pl.BlockSpec((None, None, 2, H_RET, DK_RET, DV_RET), lambda b: (b, layer, 0, 0, 0, 0))
    in_specs = [
        pl.BlockSpec(memory_space=pltpu.SMEM),
        blk(OFF_QR), blk(OFF_KR), blk(OFF_VR), blk(OFF_GR),
        pl.BlockSpec((1, D_RET), lambda b: (0, 0)),
    ]
    args = [log_decay, proj, proj, proj, proj, gn.reshape(1, D_RET)]
    if has_state:
        in_specs.append(state_spec)
        args.append(state)
    out_specs = [pl.BlockSpec((seq_len, D_RET), lambda b: (b, 0))]
    out_shape = [jax.ShapeDtypeStruct((n, D_RET), BF16)]
    aliases = {}
    if state_out is not None:
        depth, buf = state_out
        out_specs.append(state_spec)
        out_shape.append(jax.ShapeDtypeStruct((batch, depth, 2, H_RET, DK_RET, DV_RET), F32))
        if buf is not None:
            aliases[len(args)] = 1
            in_specs.append(pl.BlockSpec(memory_space=pl.ANY))
            args.append(buf)
    kern = functools.partial(_ret_kernel, seq_len=seq_len, chunk=chunk, has_state=has_state,
                             emit_state=state_out is not None, carried=bool(aliases))
    return pl.pallas_call(
        kern,
        grid=(batch,),
        in_specs=in_specs,
        out_specs=out_specs,
        out_shape=out_shape,
        input_output_aliases=aliases,
        scratch_shapes=[
            pltpu.VMEM((seq_len, DV_RET), F32),
            pltpu.VMEM((H_RET, chunk, chunk), F32),
            pltpu.VMEM((H_RET, chunk, DK_RET), F32),
            pltpu.VMEM((H_RET, chunk, DK_RET), F32),
            pltpu.VMEM((H_RET, chunk, DK_RET), F32),
            pltpu.VMEM((H_RET, chunk, DK_RET), F32),
        ],
        compiler_params=_params(1),
        name="retention",
    )(*args)


def _softmax_parts(scores, sink):
    m = jnp.maximum(sink, functools.reduce(
        jnp.maximum, [jnp.max(s, axis=-1, keepdims=True) for s in scores]))
    ps = [jnp.exp(s - m) for s in scores]
    denom = jnp.exp(sink - m) + functools.reduce(
        lambda a, b: a + b, [jnp.sum(p, axis=-1, keepdims=True) for p in ps])
    return ps, denom


def _ctx_attn_kernel(sink_ref, q_ref, k_ref, v_ref, o_ref):
    scale = DH_ATT ** -0.5
    for head in range(H_ATT):
        cols = slice(head * DH_ATT, (head + 1) * DH_ATT)
        kv_cols = slice(head // G_ATT * DH_ATT, (head // G_ATT + 1) * DH_ATT)
        s = lax.dot_general(q_ref[:, cols], k_ref[:, kv_cols], (((1,), (1,)), ((), ())),
                            preferred_element_type=F32) * scale
        (p,), denom = _softmax_parts([s], sink_ref[head])
        o = jnp.dot(p.astype(BF16), v_ref[:, kv_cols], preferred_element_type=F32)
        o_ref[:, cols] = (o / denom).astype(o_ref.dtype)


def _context_attention(proj, sink, seq_len):
    n = proj.shape[0]
    kvw = H_KV * DH_ATT
    return pl.pallas_call(
        _ctx_attn_kernel,
        grid=(n // seq_len,),
        in_specs=[
            pl.BlockSpec(memory_space=pltpu.SMEM),
            pl.BlockSpec((seq_len, D_ATT), lambda b: (b, OFF_QA // D_ATT)),
            pl.BlockSpec((seq_len, kvw), lambda b: (b, OFF_KA // kvw)),
            pl.BlockSpec((seq_len, kvw), lambda b: (b, OFF_VA // kvw)),
        ],
        out_specs=pl.BlockSpec((seq_len, D_ATT), lambda b: (b, 0)),
        out_shape=jax.ShapeDtypeStruct((n, D_ATT), BF16),
        compiler_params=_params(1),
        name="context_attention",
    )(sink, proj, proj, proj)


def _rope(x, cos, sin_signed):
    quarter = DH_ATT // 4
    lane = lax.broadcasted_iota(jnp.int32, x.shape, 1)
    first = (lane % (2 * quarter)) < quarter
    partner = jnp.where(first, pltpu.roll(x, DH_ATT - quarter, axis=1), pltpu.roll(x, quarter, axis=1))
    return x * cos + partner * sin_signed


def _lat_attn_kernel(sink_ref, q_ref, k_ref, v_ref, ck_ref, cv_ref, cos_ref, sin_ref, o_ref,
                     kr_ref, *, seq_len):
    kv_head = pl.program_id(1)
    qb = pl.program_id(2)
    scale = DH_ATT ** -0.5
    span = BLOCK + 2 * WINDOW

    @pl.when(qb == 0)
    def _():
        kr_ref[...] = _rope(k_ref[...].astype(F32), cos_ref[...], sin_ref[...]).astype(BF16)

    q_rows = pl.ds(pl.multiple_of(qb * BLOCK, BLOCK), BLOCK)
    cos_q = cos_ref[q_rows, :]
    sin_q = sin_ref[q_rows, :]
    q4 = jnp.concatenate(
        [_rope(q_ref[:, g * DH_ATT:(g + 1) * DH_ATT].astype(F32), cos_q, sin_q) for g in range(G_ATT)],
        axis=0).astype(BF16)

    start = pl.multiple_of(jnp.clip(qb * BLOCK - WINDOW, 0, seq_len - span), BLOCK)
    k_win = kr_ref[pl.ds(start, span), :]
    v_win = v_ref[pl.ds(start, span), :]
    q_pos = qb * BLOCK + lax.broadcasted_iota(jnp.int32, (BLOCK, span), 0)
    k_pos = start + lax.broadcasted_iota(jnp.int32, (BLOCK, span), 1)
    band = jnp.abs(k_pos - q_pos) <= WINDOW

    nt = (((1,), (1,)), ((), ()))
    s_loc = lax.dot_general(q4, k_win, nt, preferred_element_type=F32) * scale
    s_loc = jnp.where(band[None], s_loc.reshape(G_ATT, BLOCK, span), NEG_INF).reshape(G_ATT * BLOCK, span)
    s_ctx = lax.dot_general(q4, ck_ref[...].astype(BF16), nt, preferred_element_type=F32) * scale
    row_head = lax.broadcasted_iota(jnp.int32, (G_ATT * BLOCK, 1), 0) // BLOCK
    sink4 = jnp.full((G_ATT * BLOCK, 1), sink_ref[kv_head * G_ATT], F32)
    for g in range(1, G_ATT):
        sink4 = jnp.where(row_head == g, sink_ref[kv_head * G_ATT + g], sink4)
    (p_loc, p_ctx), denom = _softmax_parts([s_loc, s_ctx], sink4)
    o4 = (jnp.dot(p_loc.astype(BF16), v_win, preferred_element_type=F32)
          + jnp.dot(p_ctx.astype(BF16), cv_ref[...].astype(BF16), preferred_element_type=F32)) / denom
    for g in range(G_ATT):
        o_ref[:, g * DH_ATT:(g + 1) * DH_ATT] = o4[g * BLOCK:(g + 1) * BLOCK, :].astype(o_ref.dtype)


def _latent_attention(proj, sink, cache_k, cache_v, layer, seq_len, cos, sin_signed):
    n = proj.shape[0]
    batch = n // seq_len
    nq = seq_len // BLOCK
    qw = G_ATT * DH_ATT
    past = cache_k.shape[3]
    cache_spec = pl.BlockSpec((None, None, None, past, DH_ATT), lambda b, h, i: (b, layer, h, 0, 0))
    return pl.pallas_call(
        functools.partial(_lat_attn_kernel, seq_len=seq_len),
        grid=(batch, H_KV, nq),
        in_specs=[
            pl.BlockSpec(memory_space=pltpu.SMEM),
            pl.BlockSpec((BLOCK, qw), lambda b, h, i: (b * nq + i, OFF_QA // qw + h)),
            pl.BlockSpec((seq_len, DH_ATT), lambda b, h, i: (b, OFF_KA // DH_ATT + h)),
            pl.BlockSpec((seq_len, DH_ATT), lambda b, h, i: (b, OFF_VA // DH_ATT + h)),
            cache_spec, cache_spec,
            pl.BlockSpec((seq_len, DH_ATT), lambda b, h, i: (0, 0)),
            pl.BlockSpec((seq_len, DH_ATT), lambda b, h, i: (0, 0)),
        ],
        out_specs=pl.BlockSpec((BLOCK, qw), lambda b, h, i: (b * nq + i, h)),
        out_shape=jax.ShapeDtypeStruct((n, D_ATT), BF16),
        scratch_shapes=[pltpu.VMEM((seq_len, DH_ATT), BF16)],
        compiler_params=_params(3),
        name="latent_attention",
    )(sink, proj, proj, proj, cache_k, cache_v, cos, sin_signed)


def _rope_tables(seq_len):
    rows = seq_len // GRID_W
    row = jnp.repeat(jnp.arange(rows), GRID_W)
    col = jnp.tile(jnp.arange(GRID_W), rows)
    quarter = DH_ATT // 4
    inv = ROPE_BASE ** (-jnp.arange(quarter, dtype=F32) / quarter)
    ang_r, ang_c = row[:, None] * inv, col[:, None] * inv
    cos = jnp.concatenate([jnp.cos(ang_r)] * 2 + [jnp.cos(ang_c)] * 2, axis=-1)
    sin_signed = jnp.concatenate([-jnp.sin(ang_r), jnp.sin(ang_r), -jnp.sin(ang_c), jnp.sin(ang_c)], axis=-1)
    return cos.astype(F32), sin_signed.astype(F32)


def _out_proj_body(r):
    @pl.when(pl.program_id(1) == 0)
    def _():
        r["y"][:, :D_RET] = r["y_r"][...]
        r["y"][:, D_RET:] = r["y_a"][...]

    mix = jnp.dot(r["y"][...], _tile(r, "w"), preferred_element_type=F32)
    r["out"][...] = r["x"][...] + r["gate"][...] * mix


def _out_project(y_r, y_a, x, mod4, w, layer, tiles, rounding, out=None):
    n, d = x.shape
    tm = TOKEN_TILE
    tn = OUT_COLS_ROUNDING if rounding else OUT_COLS
    w_in, w_out = _weight_tile("w", w, layer, (D_RET + D_ATT, tn), lambda i, j: (0, j), rounding)
    ins = [("y_r", pl.BlockSpec((tm, D_RET), tiles.rows(D_RET, D_RET)), y_r),
           ("y_a", pl.BlockSpec((tm, D_ATT), tiles.rows(D_ATT, D_ATT)), y_a),
           w_in,
           ("x", pl.BlockSpec((tm, tn), tiles.rows(tn, d)), x),
           ("gate", tiles.mod(mod4, layer, 2, tn, lambda j: j), mod4)]
    outs = [("out", pl.BlockSpec((tm, tn), tiles.rows(tn, d)), jax.ShapeDtypeStruct((n, d), F32))]
    if w_out:
        outs.append(w_out)
    return _call(_out_proj_body, "out_proj", (tiles.count, d // tn), ins, outs,
                 [("y", pltpu.VMEM((tm, D_RET + D_ATT), BF16))], [("out", out)])


def _ffn_body(r, *, seq_len):
    j = pl.program_id(1)
    x_ref, o_ref, h_ref = r["x"], r["out"], r["h"]
    rows = x_ref.shape[0]

    cr = CONV_ROWS
    n_chunks = rows // cr
    w_gate, w_val, w_down = _tile(r, "wg"), _tile(r, "wv"), _tile(r, "wd")

    def step(first):
        t = lax.broadcasted_iota(jnp.int32, (cr, w_gate.shape[1]), 0)
        ups = []
        for m in range(n_chunks):
            rs = slice(m * cr, (m + 1) * cr)
            if first:
                for c in range(m * cr // NORM_ROWS, (m + 1) * cr // NORM_ROWS):
                    ns = slice(c * NORM_ROWS, (c + 1) * NORM_ROWS)
                    h_ref[ns, :] = _modulated_norm(x_ref[ns, :], r["scale"], r["shift"])
            hm = h_ref[rs, :]
            ups.append(tuple(jnp.dot(hm, w, preferred_element_type=F32) for w in (w_gate, w_val)))

        for m in range(n_chunks):
            seq_pos = (t + (m * cr) % seq_len) % seq_len
            seq_start, seq_end = seq_pos == 0, seq_pos == seq_len - 1

            def conv_branch(b, cw_ref, cb_ref):
                up = ups[m][b]
                prev = pltpu.roll(up, 1, axis=0)
                nxt = pltpu.roll(up, cr - 1, axis=0)
                if (m * cr) % seq_len:
                    prev = jnp.where(t == 0, ups[m - 1][b][cr - 1:cr, :], prev)
                if ((m + 1) * cr) % seq_len:
                    nxt = jnp.where(t == cr - 1, ups[m + 1][b][0:1, :], nxt)
                prev = jnp.where(seq_start, 0.0, prev)
                nxt = jnp.where(seq_end, 0.0, nxt)
                return cb_ref[...] + prev * cw_ref[0:1, :] + up * cw_ref[1:2, :] + nxt * cw_ref[2:3, :]

            act = jax.nn.silu(conv_branch(0, r["cwg"], r["cbg"])) * conv_branch(1, r["cwv"], r["cbv"])
            down = jnp.dot(act.astype(BF16), w_down, preferred_element_type=F32)
            rs = slice(m * cr, (m + 1) * cr)
            if first:
                o_ref[rs, :] = down
            else:
                o_ref[rs, :] += down

    pl.when(j == 0)(lambda: step(True))
    pl.when(j > 0)(lambda: step(False))

    @pl.when(j == pl.num_programs(1) - 1)
    def _():
        if "gain" not in r:
            o_ref[...] = x_ref[...] + r["gate"][...] * o_ref[...]
            return
        gate, gain = r["gate"][...], r["gain"][...]

        def body(c, carry):
            rs = pl.ds(pl.multiple_of(c * NORM_ROWS, NORM_ROWS), NORM_ROWS)
            y = x_ref[rs, :] + gate * o_ref[rs, :]
            o_ref[rs, :] = y * lax.rsqrt(jnp.mean(y * y, axis=-1, keepdims=True) + EPS) * gain
            return carry

        lax.fori_loop(0, rows // NORM_ROWS, body, 0)


def _conv_ffn(x, mod4, w_gate, w_val, conv_w, conv_b, w_down, layer, tiles, rounding, seq_len,
              gain=None, out=None):
    n, d = x.shape
    d_ff = w_down.shape[-2]
    tm = TOKEN_TILE
    tf = FFN_COLS_ROUNDING if rounding else FFN_COLS
    nf = d_ff // tf
    val0 = nf if rounding else 0
    wg_in, wg_out = _weight_tile("wg", w_gate, layer, (d, tf), lambda i, j: (0, j), rounding)
    wv_in, wv_out = _weight_tile("wv", w_val, layer, (d, tf), lambda i, j: (0, val0 + j), rounding)
    wd_in, wd_out = _weight_tile("wd", w_down, layer, (tf, d), lambda i, j: (j, 0), rounding)
    if rounding:
        half = jax.ShapeDtypeStruct((d, d_ff), BF16)
        wg_out = (wg_out[0], wg_out[1], half)
        wv_out = (wv_out[0], pl.BlockSpec((d, tf), lambda i, j: (0, j)), half)
    conv_b3 = conv_b.reshape(conv_b.shape[0], 1, 2 * d_ff)
    x_spec = pl.BlockSpec((tm, d), tiles.rows(d, d), **({"pipeline_mode": pl.Buffered(1)} if rounding else {}))
    ins = [("x", x_spec, x),
           ("shift", tiles.mod(mod4, layer, 3, d), mod4),
           ("scale", tiles.mod(mod4, layer, 4, d), mod4),
           ("gate", tiles.mod(mod4, layer, 5, d), mod4),
           wg_in, wv_in,
           ("cwg", pl.BlockSpec((None, CONV_W, tf), lambda i, j: (layer, 0, j)), conv_w),
           ("cwv", pl.BlockSpec((None, CONV_W, tf), lambda i, j: (layer, 0, nf + j)), conv_w),
           ("cbg", pl.BlockSpec((None, 1, tf), lambda i, j: (layer, 0, j)), conv_b3),
           ("cbv", pl.BlockSpec((None, 1, tf), lambda i, j: (layer, 0, nf + j)), conv_b3),
           wd_in]
    if gain is not None:
        ins.append(("gain", pl.BlockSpec((1, d), lambda i, j: (0, 0)), gain.reshape(1, d)))
    outs = [("out", pl.BlockSpec((tm, d), tiles.rows(d, d)), jax.ShapeDtypeStruct((n, d), F32))]
    outs += [o for o in (wg_out, wv_out, wd_out) if o]
    return _call(_ffn_body, "conv_ffn", (tiles.count, nf), ins, outs,
                 [("h", pltpu.VMEM((tm, d), BF16))], [("out", out)], seq_len=seq_len)


def kernel(x_prompt, x_sample, cache_k, cache_v, state_ret, c, c_ctx, w_mod, b_mod, w_in, w_out,
           ret_log_decay, ret_gn, att_sink, w_up, conv_w, conv_b, w_down, final_gain):
    batch, seq, d = x_prompt.shape
    dec_batch, dec_seq, _ = x_sample.shape
    depth = w_in.shape[0]
    assert TOKEN_TILE % seq == 0 and dec_seq == TOKEN_TILE and 1 + dec_batch <= COND_ROWS
    assert w_in.shape[-1] == D_IN and d == D_RET + D_ATT

    cond = jnp.zeros((COND_ROWS, d), F32).at[0].set(c_ctx).at[1:1 + dec_batch].set(c)
    mod = _modulation(cond, w_mod, b_mod)
    mod4 = mod.reshape(depth, COND_ROWS, 1, N_MOD * d)
    cos, sin_signed = _rope_tables(dec_seq)

    xp = x_prompt.reshape(batch * seq, d)
    xs = x_sample.reshape(dec_batch * dec_seq, d)
    n_ctx_tiles = batch * seq // TOKEN_TILE
    ctx_first = _Tiles(0, 1, 0, 0)
    ctx_rest = _Tiles(1, n_ctx_tiles - 1, 0, 0)
    lat = _Tiles(0, dec_batch, 1, 1)

    new_k = new_v = new_s = None
    for l in range(depth):
        o = _project(xp, mod4, w_in, l, ctx_first, True, cache=(seq, depth, new_k, new_v))
        w_in_l = o["w_bf"]
        o = _project(xp, mod4, w_in_l, l, ctx_rest, False, proj=o["proj"], cache=(seq, depth, o["k"], o["v"]))
        proj, new_k, new_v = o["proj"], o["k"], o["v"]
        y_r, new_s = _retention(proj, ret_log_decay[l], ret_gn[l], seq, l, state_out=(depth, new_s))
        y_a = _context_attention(proj, att_sink[l], seq)
        o = _out_project(y_r, y_a, xp, mod4, w_out, l, ctx_first, True)
        w_out_l = o["w_bf"]
        x1 = _out_project(y_r, y_a, xp, mod4, w_out_l, l, ctx_rest, False, out=o["out"])["out"]
        gain = final_gain if l == depth - 1 else None
        o = _conv_ffn(x1, mod4, w_up, w_up, conv_w, conv_b, w_down, l, ctx_first, True, seq, gain=gain)
        w_gate_l, w_val_l, w_down_l = o["wg_bf"], o["wv_bf"], o["wd_bf"]
        xp = _conv_ffn(x1, mod4, w_gate_l, w_val_l, conv_w, conv_b, w_down_l, l, ctx_rest, False, seq,
                       gain=gain, out=o["out"])["out"]
        proj = _project(xs, mod4, w_in_l, l, lat, False)["proj"]
        (y_r,) = _retention(proj, ret_log_decay[l], ret_gn[l], dec_seq, l, state=state_ret)
        y_a = _latent_attention(proj, att_sink[l], cache_k, cache_v, l, dec_seq, cos, sin_signed)
        x1 = _out_project(y_r, y_a, xs, mod4, w_out_l, l, lat, False)["out"]
        xs = _conv_ffn(x1, mod4, w_gate_l, w_val_l, conv_w, conv_b, w_down_l, l, lat, False, dec_seq,
                       gain=gain)["out"]

    return (xp.reshape(batch, seq, d), xs.reshape(dec_batch, dec_seq, d), new_k, new_v, new_s)
```

```python
import functools

import jax
import jax.numpy as jnp
from jax import lax
from jax.experimental import pallas as pl
from jax.experimental.pallas import tpu as pltpu

F32 = jnp.float32
BF16 = jnp.bfloat16

GRID_W = 64
H_RET = 8
DK_RET = 128
DV_RET = 128
D_RET = H_RET * DV_RET
H_ATT = 8
H_KV = 2
G_ATT = H_ATT // H_KV
DH_ATT = 128
D_ATT = H_ATT * DH_ATT
WINDOW = 128
BLOCK = 128
CONV_W = 3
N_MOD = 6
ROPE_BASE = 10000.0
EPS = 1e-6
NEG_INF = -1e30

OFF_QR = 0
OFF_KR = OFF_QR + H_RET * DK_RET
OFF_VR = OFF_KR + H_RET * DK_RET
OFF_GR = OFF_VR + H_RET * DV_RET
OFF_QA = OFF_GR + H_RET * DV_RET
OFF_KA = OFF_QA + H_ATT * DH_ATT
OFF_VA = OFF_KA + H_KV * DH_ATT
D_IN = OFF_VA + H_KV * DH_ATT
D_KV = 2 * H_KV * DH_ATT

COND_ROWS = 8
TOKEN_TILE = 1024
NORM_ROWS = 256
CONV_ROWS = 512
RET_CHUNK = 256
PROJ_COLS, PROJ_COLS_ROUNDING = 1408, 512
OUT_COLS, OUT_COLS_ROUNDING = 1024, 512
FFN_COLS, FFN_COLS_ROUNDING = 512, 256
V7X_VMEM_LIMIT = 58 * 1024 * 1024


def _params(n_axes):
    return pltpu.CompilerParams(dimension_semantics=("arbitrary",) * n_axes,
                                vmem_limit_bytes=V7X_VMEM_LIMIT)


def _call(body, name, grid, ins, outs, scratch, carried=(), **static):
    in_names = [n for n, _, _ in ins]
    in_specs = [s for _, s, _ in ins]
    args = [a for _, _, a in ins]
    out_names = [n for n, _, _ in outs]
    aliases = {}
    for out_name, arr in carried:
        if arr is not None:
            aliases[len(args)] = out_names.index(out_name)
            in_names.append("carried_" + out_name)
            in_specs.append(pl.BlockSpec(memory_space=pl.ANY))
            args.append(arr)
    names = tuple(in_names + out_names + [n for n, _ in scratch])

    def kern(*refs):
        body(dict(zip(names, refs)), **static)

    res = pl.pallas_call(
        kern,
        grid=grid,
        in_specs=in_specs,
        out_specs=[s for _, s, _ in outs],
        out_shape=[s for _, _, s in outs],
        input_output_aliases=aliases,
        scratch_shapes=[s for _, s in scratch],
        compiler_params=_params(len(grid)),
        name=name,
    )(*args)
    return dict(zip(out_names, res))


def _mod_kernel(cond_ref, w_ref, b_ref, o_ref):
    a = jax.nn.silu(cond_ref[...]).astype(BF16)
    o_ref[...] = jnp.dot(a, w_ref[...].astype(BF16), preferred_element_type=F32) + b_ref[...]


def _modulation(cond, w_mod, b_mod):
    depth, d, n = w_mod.shape
    tn = 1024
    return pl.pallas_call(
        _mod_kernel,
        grid=(depth, n // tn),
        in_specs=[
            pl.BlockSpec((COND_ROWS, d), lambda l, j: (0, 0)),
            pl.BlockSpec((None, d, tn), lambda l, j: (l, 0, j)),
            pl.BlockSpec((None, 1, tn), lambda l, j: (l, 0, j)),
        ],
        out_specs=pl.BlockSpec((None, COND_ROWS, tn), lambda l, j: (l, 0, j)),
        out_shape=jax.ShapeDtypeStruct((depth, COND_ROWS, n), F32),
        compiler_params=_params(2),
        name="modulation",
    )(cond, w_mod, b_mod.reshape(depth, 1, n))


class _Tiles:
    def __init__(self, first, count, row0, row_step):
        self.first, self.count, self.row0, self.row_step = first, count, row0, row_step

    def rows(self, width, full_width=None):
        if width == full_width:
            return lambda i, j: (self.first + i, 0)
        return lambda i, j: (self.first + i, j)

    def mod(self, mod4, layer, chunk, width, col_block=lambda j: 0):
        per_chunk = mod4.shape[-1] // N_MOD // width
        return pl.BlockSpec(
            (None, None, 1, width),
            lambda i, j: (layer, self.row0 + self.row_step * (self.first + i), 0, chunk * per_chunk + col_block(j)))


def _weight_tile(name, w, layer, block, index, rounding):
    if not rounding:
        return (name, pl.BlockSpec(block, index), w), None
    spec = pl.BlockSpec((None,) + block, lambda i, j: (layer,) + index(i, j))
    out = (name + "_bf", pl.BlockSpec(block, index), jax.ShapeDtypeStruct(w.shape[1:], BF16))
    return (name, spec, w), out


def _tile(r, name):
    w = r[name][...]
    if name + "_bf" in r:
        w = w.astype(BF16)
        r[name + "_bf"][...] = w
    return w


def _modulated_norm(x, scale_ref, shift_ref):
    inv = lax.rsqrt(jnp.mean(x * x, axis=-1, keepdims=True) + EPS)
    return ((x * inv) * (1.0 + scale_ref[...]) + shift_ref[...]).astype(BF16)


def _proj_body(r, *, kv_step):
    j = pl.program_id(1)
    h_ref, proj_ref = r["h"], r["proj"]

    @pl.when(j == 0)
    def _():
        w = _tile(r, "w")
        for c in range(h_ref.shape[0] // NORM_ROWS):
            rs = slice(c * NORM_ROWS, (c + 1) * NORM_ROWS)
            h = _modulated_norm(r["x"][rs, :], r["scale"], r["shift"])
            h_ref[rs, :] = h
            proj_ref[rs, :] = jnp.dot(h, w, preferred_element_type=F32).astype(BF16)

    @pl.when(j > 0)
    def _():
        res = jnp.dot(h_ref[...], _tile(r, "w"), preferred_element_type=F32)
        proj_ref[...] = res.astype(BF16)
        if "k" in r:
            seqs, _, seq_len, _ = r["k"].shape
            kv0 = res.shape[1] - D_KV

            @pl.when(j == kv_step)
            def _():
                for dst, off in ((r["k"], kv0), (r["v"], kv0 + H_KV * DH_ATT)):
                    for s in range(seqs):
                        for hh in range(H_KV):
                            dst[s, hh] = res[s * seq_len:(s + 1) * seq_len,
                                             off + hh * DH_ATT:off + (hh + 1) * DH_ATT]


def _project(x, mod4, w, layer, tiles, rounding, proj=None, cache=None):
    n, d = x.shape
    d_in = w.shape[-1]
    tm = TOKEN_TILE
    tn = PROJ_COLS_ROUNDING if rounding else PROJ_COLS
    assert tn >= D_KV and d_in % tn == 0
    w_in, w_out = _weight_tile("w", w, layer, (d, tn), lambda i, j: (0, j), rounding)
    ins = [("x", pl.BlockSpec((tm, d), tiles.rows(d, d)), x),
           ("shift", tiles.mod(mod4, layer, 0, d), mod4),
           ("scale", tiles.mod(mod4, layer, 1, d), mod4),
           w_in]
    outs = [("proj", pl.BlockSpec((tm, tn), tiles.rows(tn, d_in)), jax.ShapeDtypeStruct((n, d_in), BF16))]
    carried = [("proj", proj)]
    if cache is not None:
        seq_len, depth, k_buf, v_buf = cache
        seqs = tm // seq_len
        shape = jax.ShapeDtypeStruct((n // seq_len, depth, H_KV, seq_len, DH_ATT), F32)
        spec = pl.BlockSpec((seqs, None, H_KV, seq_len, DH_ATT), lambda i, j: (tiles.first + i, layer, 0, 0, 0))
        outs += [("k", spec, shape), ("v", spec, shape)]
        carried += [("k", k_buf), ("v", v_buf)]
    if w_out:
        outs.append(w_out)
    return _call(_proj_body, "in_proj", (tiles.count, d_in // tn), ins, outs,
                 [("h", pltpu.VMEM((tm, d), BF16))], carried, kv_step=d_in // tn - 1)


def _ret_kernel(*refs, seq_len, chunk, has_state, emit_state, carried):
    it = iter(refs)
    ld_ref = next(it)
    q_ref, k_ref, v_ref, g_ref, gn_ref = (next(it) for _ in range(5))
    s0_ref = next(it) if has_state else None
    if carried:
        next(it)
    y_ref = next(it)
    snew_ref = next(it) if emit_state else None
    o_ref, intra_ref, qdf_ref, kdf_ref, qdb_ref, kdb_ref = (next(it) for _ in range(6))

    n_chunks = seq_len // chunk
    k_scale = DK_RET ** -0.5

    @pl.when(pl.program_id(0) == 0)
    def _():
        row = lax.broadcasted_iota(jnp.int32, (chunk, chunk), 0).astype(F32)
        col = lax.broadcasted_iota(jnp.int32, (chunk, chunk), 1).astype(F32)
        diff = row - col
        pos = lax.broadcasted_iota(jnp.int32, (chunk, DK_RET), 0).astype(F32)
        for h in range(H_RET):
            lg_f = ld_ref[0, h]
            lg_b = ld_ref[1, h]
            intra_ref[h] = (jnp.where(diff >= 0, jnp.exp(lg_f * jnp.maximum(diff, 0.0)), 0.0)
                            + jnp.where(diff <= 0, jnp.exp(lg_b * jnp.maximum(-diff, 0.0)), 0.0)) * k_scale
            qdf_ref[h] = jnp.exp(lg_f * (pos + 1.0))
            kdf_ref[h] = jnp.exp(lg_f * (chunk - 1.0 - pos)) * k_scale
            qdb_ref[h] = jnp.exp(lg_b * (chunk - pos))
            kdb_ref[h] = jnp.exp(lg_b * pos) * k_scale

    def rows(n):
        return pl.ds(n * chunk, chunk)

    def kv_update(k_b, dec, v_b):
        kd = (k_b.astype(F32) * dec).T.astype(BF16)
        return jnp.dot(kd, v_b, preferred_element_type=F32)

    for h in range(H_RET):
        cols = slice(h * DK_RET, (h + 1) * DK_RET)
        c_dec_f = jnp.exp(jnp.full((DK_RET, DV_RET), ld_ref[0, h] * chunk, F32))
        c_dec_b = jnp.exp(jnp.full((DK_RET, DV_RET), ld_ref[1, h] * chunk, F32))

        s_f = s0_ref[0, h] if has_state else jnp.zeros((DK_RET, DV_RET), F32)
        for n in range(n_chunks):
            q_b = q_ref[rows(n), cols]
            k_b = k_ref[rows(n), cols]
            v_b = v_ref[rows(n), cols]
            sc = lax.dot_general(q_b, k_b, (((1,), (1,)), ((), ())),
                                 preferred_element_type=F32) * intra_ref[h]
            o = jnp.dot(sc.astype(BF16), v_b, preferred_element_type=F32)
            if has_state or n > 0:
                o = o + jnp.dot(q_b, s_f.astype(BF16), preferred_element_type=F32) * qdf_ref[h]
            o_ref[rows(n), :] = o
            s_f = c_dec_f * s_f + kv_update(k_b, kdf_ref[h], v_b)

        s_b = s0_ref[1, h] if has_state else jnp.zeros((DK_RET, DV_RET), F32)
        for n in reversed(range(n_chunks)):
            q_b = q_ref[rows(n), cols]
            k_b = k_ref[rows(n), cols]
            v_b = v_ref[rows(n), cols]
            if has_state or n < n_chunks - 1:
                o_ref[rows(n), :] += jnp.dot(q_b, s_b.astype(BF16), preferred_element_type=F32) * qdb_ref[h]
            s_b = c_dec_b * s_b + kv_update(k_b, kdb_ref[h], v_b)

        if emit_state:
            snew_ref[0, h] = s_f
            snew_ref[1, h] = s_b

        gn = gn_ref[:, cols]
        for n in range(n_chunks):
            o = o_ref[rows(n), :]
            mu = jnp.mean(o, axis=-1, keepdims=True)
            cen = o - mu
            var = jnp.mean(cen * cen, axis=-1, keepdims=True)
            on = cen * lax.rsqrt(var + EPS) * gn
            y_ref[rows(n), cols] = (jax.nn.silu(g_ref[rows(n), cols].astype(F32)) * on).astype(y_ref.dtype)


def _retention(proj, log_decay, gn, seq_len, layer, state=None, state_out=None):
    n = proj.shape[0]
    batch = n // seq_len
    has_state = state is not None
    chunk = min(RET_CHUNK, seq_len)
    blk = lambda off: pl.BlockSpec((seq_len, D_RET), lambda b: (b, off // D_RET))
    state_spec = pl.BlockSpec((None, None, 2, H_RET, DK_RET, DV_RET), lambda b: (b, layer, 0, 0, 0, 0))
    in_specs = [
        pl.BlockSpec(memory_space=pltpu.SMEM),
        blk(OFF_QR), blk(OFF_KR), blk(OFF_VR), blk(OFF_GR),
        pl.BlockSpec((1, D_RET), lambda b: (0, 0)),
    ]
    args = [log_decay, proj, proj, proj, proj, gn.reshape(1, D_RET)]
    if has_state:
        in_specs.append(state_spec)
        args.append(state)
    out_specs = [pl.BlockSpec((seq_len, D_RET), lambda b: (b, 0))]
    out_shape = [jax.ShapeDtypeStruct((n, D_RET), BF16)]
    aliases = {}
    if state_out is not None:
        depth, buf = state_out
        out_specs.append(state_spec)
        out_shape.append(jax.ShapeDtypeStruct((batch, depth, 2, H_RET, DK_RET, DV_RET), F32))
        if buf is not None:
            aliases[len(args)] = 1
            in_specs.append(pl.BlockSpec(memory_space=pl.ANY))
            args.append(buf)
    kern = functools.partial(_ret_kernel, seq_len=seq_len, chunk=chunk, has_state=has_state,
                             emit_state=state_out is not None, carried=bool(aliases))
    return pl.pallas_call(
        kern,
        grid=(batch,),
        in_specs=in_specs,
        out_specs=out_specs,
        out_shape=out_shape,
        input_output_aliases=aliases,
        scratch_shapes=[
            pltpu.VMEM((seq_len, DV_RET), F32),
            pltpu.VMEM((H_RET, chunk, chunk), F32),
            pltpu.VMEM((H_RET, chunk, DK_RET), F32),
            pltpu.VMEM((H_RET, chunk, DK_RET), F32),
            pltpu.VMEM((H_RET, chunk, DK_RET), F32),
            pltpu.VMEM((H_RET, chunk, DK_RET), F32),
        ],
        compiler_params=_params(1),
        name="retention",
    )(*args)


def _softmax_parts(scores, sink):
    m = jnp.maximum(sink, functools.reduce(
        jnp.maximum, [jnp.max(s, axis=-1, keepdims=True) for s in scores]))
    ps = [jnp.exp(s - m) for s in scores]
    denom = jnp.exp(sink - m) + functools.reduce(
        lambda a, b: a + b, [jnp.sum(p, axis=-1, keepdims=True) for p in ps])
    return ps, denom


def _ctx_attn_kernel(sink_ref, q_ref, k_ref, v_ref, o_ref):
    scale = DH_ATT ** -0.5
    for head in range(H_ATT):
        cols = slice(head * DH_ATT, (head + 1) * DH_ATT)
        kv_cols = slice(head // G_ATT * DH_ATT, (head // G_ATT + 1) * DH_ATT)
        s = lax.dot_general(q_ref[:, cols], k_ref[:, kv_cols], (((1,), (1,)), ((), ())),
                            preferred_element_type=F32) * scale
        (p,), denom = _softmax_parts([s], sink_ref[head])
        o = jnp.dot(p.astype(BF16), v_ref[:, kv_cols], preferred_element_type=F32)
        o_ref[:, cols] = (o / denom).astype(o_ref.dtype)


def _context_attention(proj, sink, seq_len):
    n = proj.shape[0]
    kvw = H_KV * DH_ATT
    return pl.pallas_call(
        _ctx_attn_kernel,
        grid=(n // seq_len,),
        in_specs=[
            pl.BlockSpec(memory_space=pltpu.SMEM),
            pl.BlockSpec((seq_len, D_ATT), lambda b: (b, OFF_QA // D_ATT)),
            pl.BlockSpec((seq_len, kvw), lambda b: (b, OFF_KA // kvw)),
            pl.BlockSpec((seq_len, kvw), lambda b: (b, OFF_VA // kvw)),
        ],
        out_specs=pl.BlockSpec((seq_len, D_ATT), lambda b: (b, 0)),
        out_shape=jax.ShapeDtypeStruct((n, D_ATT), BF16),
        compiler_params=_params(1),
        name="context_attention",
    )(sink, proj, proj, proj)


def _rope(x, cos, sin_signed):
    quarter = DH_ATT // 4
    lane = lax.broadcasted_iota(jnp.int32, x.shape, 1)
    first = (lane % (2 * quarter)) < quarter
    partner = jnp.where(first, pltpu.roll(x, DH_ATT - quarter, axis=1), pltpu.roll(x, quarter, axis=1))
    return x * cos + partner * sin_signed


def _lat_attn_kernel(sink_ref, q_ref, k_ref, v_ref, ck_ref, cv_ref, cos_ref, sin_ref, o_ref,
                     kr_ref, *, seq_len):
    kv_head = pl.program_id(1)
    qb = pl.program_id(2)
    scale = DH_ATT ** -0.5
    span = BLOCK + 2 * WINDOW

    @pl.when(qb == 0)
    def _():
        kr_ref[...] = _rope(k_ref[...].astype(F32), cos_ref[...], sin_ref[...]).astype(BF16)

    q_rows = pl.ds(pl.multiple_of(qb * BLOCK, BLOCK), BLOCK)
    cos_q = cos_ref[q_rows, :]
    sin_q = sin_ref[q_rows, :]
    q4 = jnp.concatenate(
        [_rope(q_ref[:, g * DH_ATT:(g + 1) * DH_ATT].astype(F32), cos_q, sin_q) for g in range(G_ATT)],
        axis=0).astype(BF16)

    start = pl.multiple_of(jnp.clip(qb * BLOCK - WINDOW, 0, seq_len - span), BLOCK)
    k_win = kr_ref[pl.ds(start, span), :]
    v_win = v_ref[pl.ds(start, span), :]
    q_pos = qb * BLOCK + lax.broadcasted_iota(jnp.int32, (BLOCK, span), 0)
    k_pos = start + lax.broadcasted_iota(jnp.int32, (BLOCK, span), 1)
    band = jnp.abs(k_pos - q_pos) <= WINDOW

    nt = (((1,), (1,)), ((), ()))
    s_loc = lax.dot_general(q4, k_win, nt, preferred_element_type=F32) * scale
    s_loc = jnp.where(band[None], s_loc.reshape(G_ATT, BLOCK, span), NEG_INF).reshape(G_ATT * BLOCK, span)
    s_ctx = lax.dot_general(q4, ck_ref[...].astype(BF16), nt, preferred_element_type=F32) * scale
    row_head = lax.broadcasted_iota(jnp.int32, (G_ATT * BLOCK, 1), 0) // BLOCK
    sink4 = jnp.full((G_ATT * BLOCK, 1), sink_ref[kv_head * G_ATT], F32)
    for g in range(1, G_ATT):
        sink4 = jnp.where(row_head == g, sink_ref[kv_head * G_ATT + g], sink4)
    (p_loc, p_ctx), denom = _softmax_parts([s_loc, s_ctx], sink4)
    o4 = (jnp.dot(p_loc.astype(BF16), v_win, preferred_element_type=F32)
          + jnp.dot(p_ctx.astype(BF16), cv_ref[...].astype(BF16), preferred_element_type=F32)) / denom
    for g in range(G_ATT):
        o_ref[:, g * DH_ATT:(g + 1) * DH_ATT] = o4[g * BLOCK:(g + 1) * BLOCK, :].astype(o_ref.dtype)


def _latent_attention(proj, sink, cache_k, cache_v, layer, seq_len, cos, sin_signed):
    n = proj.shape[0]
    batch = n // seq_len
    nq = seq_len // BLOCK
    qw = G_ATT * DH_ATT
    past = cache_k.shape[3]
    cache_spec = pl.BlockSpec((None, None, None, past, DH_ATT), lambda b, h, i: (b, layer, h, 0, 0))
    return pl.pallas_call(
        functools.partial(_lat_attn_kernel, seq_len=seq_len),
        grid=(batch, H_KV, nq),
        in_specs=[
            pl.BlockSpec(memory_space=pltpu.SMEM),
            pl.BlockSpec((BLOCK, qw), lambda b, h, i: (b * nq + i, OFF_QA // qw + h)),
            pl.BlockSpec((seq_len, DH_ATT), lambda b, h, i: (b, OFF_KA // DH_ATT + h)),
            pl.BlockSpec((seq_len, DH_ATT), lambda b, h, i: (b, OFF_VA // DH_ATT + h)),
            cache_spec, cache_spec,
            pl.BlockSpec((seq_len, DH_ATT), lambda b, h, i: (0, 0)),
            pl.BlockSpec((seq_len, DH_ATT), lambda b, h, i: (0, 0)),
        ],
        out_specs=pl.BlockSpec((BLOCK, qw), lambda b, h, i: (b * nq + i, h)),
        out_shape=jax.ShapeDtypeStruct((n, D_ATT), BF16),
        scratch_shapes=[pltpu.VMEM((seq_len, DH_ATT), BF16)],
        compiler_params=_params(3),
        name="latent_attention",
    )(sink, proj, proj, proj, cache_k, cache_v, cos, sin_signed)


def _rope_tables(seq_len):
    rows = seq_len // GRID_W
    row = jnp.repeat(jnp.arange(rows), GRID_W)
    col = jnp.tile(jnp.arange(GRID_W), rows)
    quarter = DH_ATT // 4
    inv = ROPE_BASE ** (-jnp.arange(quarter, dtype=F32) / quarter)
    ang_r, ang_c = row[:, None] * inv, col[:, None] * inv
    cos = jnp.concatenate([jnp.cos(ang_r)] * 2 + [jnp.cos(ang_c)] * 2, axis=-1)
    sin_signed = jnp.concatenate([-jnp.sin(ang_r), jnp.sin(ang_r), -jnp.sin(ang_c), jnp.sin(ang_c)], axis=-1)
    return cos.astype(F32), sin_signed.astype(F32)


def _out_proj_body(r):
    @pl.when(pl.program_id(1) == 0)
    def _():
        r["y"][:, :D_RET] = r["y_r"][...]
        r["y"][:, D_RET:] = r["y_a"][...]

    mix = jnp.dot(r["y"][...], _tile(r, "w"), preferred_element_type=F32)
    r["out"][...] = r["x"][...] + r["gate"][...] * mix


def _out_project(y_r, y_a, x, mod4, w, layer, tiles, rounding, out=None):
    n, d = x.shape
    tm = TOKEN_TILE
    tn = OUT_COLS_ROUNDING if rounding else OUT_COLS
    w_in, w_out = _weight_tile("w", w, layer, (D_RET + D_ATT, tn), lambda i, j: (0, j), rounding)
    ins = [("y_r", pl.BlockSpec((tm, D_RET), tiles.rows(D_RET, D_RET)), y_r),
           ("y_a", pl.BlockSpec((tm, D_ATT), tiles.rows(D_ATT, D_ATT)), y_a),
           w_in,
           ("x", pl.BlockSpec((tm, tn), tiles.rows(tn, d)), x),
           ("gate", tiles.mod(mod4, layer, 2, tn, lambda j: j), mod4)]
    outs = [("out", pl.BlockSpec((tm, tn), tiles.rows(tn, d)), jax.ShapeDtypeStruct((n, d), F32))]
    if w_out:
        outs.append(w_out)
    return _call(_out_proj_body, "out_proj", (tiles.count, d // tn), ins, outs,
                 [("y", pltpu.VMEM((tm, D_RET + D_ATT), BF16))], [("out", out)])


def _ffn_body(r, *, seq_len):
    j = pl.program_id(1)
    x_ref, o_ref, h_ref = r["x"], r["out"], r["h"]
    rows = x_ref.shape[0]

    cr = CONV_ROWS
    n_chunks = rows // cr

    def step(first):
        w_gate, w_val, w_down = _tile(r, "wg"), _tile(r, "wv"), _tile(r, "wd")
        t = lax.broadcasted_iota(jnp.int32, (cr, w_gate.shape[1]), 0)
        ups = []
        for m in range(n_chunks):
            rs = slice(m * cr, (m + 1) * cr)
            if first:
                for c in range(m * cr // NORM_ROWS, (m + 1) * cr // NORM_ROWS):
                    ns = slice(c * NORM_ROWS, (c + 1) * NORM_ROWS)
                    h_ref[ns, :] = _modulated_norm(x_ref[ns, :], r["scale"], r["shift"])
            hm = h_ref[rs, :]
            ups.append(tuple(jnp.dot(hm, w, preferred_element_type=F32) for w in (w_gate, w_val)))

        for m in range(n_chunks):
            seq_pos = (t + (m * cr) % seq_len) % seq_len
            seq_start, seq_end = seq_pos == 0, seq_pos == seq_len - 1

            def conv_branch(b, cw_ref, cb_ref):
                up = ups[m][b]
                prev = pltpu.roll(up, 1, axis=0)
                nxt = pltpu.roll(up, cr - 1, axis=0)
                if (m * cr) % seq_len:
                    prev = jnp.where(t == 0, ups[m - 1][b][cr - 1:cr, :], prev)
                if ((m + 1) * cr) % seq_len:
                    nxt = jnp.where(t == cr - 1, ups[m + 1][b][0:1, :], nxt)
                prev = jnp.where(seq_start, 0.0, prev)
                nxt = jnp.where(seq_end, 0.0, nxt)
                return cb_ref[...] + prev * cw_ref[0:1, :] + up * cw_ref[1:2, :] + nxt * cw_ref[2:3, :]

            act = jax.nn.silu(conv_branch(0, r["cwg"], r["cbg"])) * conv_branch(1, r["cwv"], r["cbv"])
            down = jnp.dot(act.astype(BF16), w_down, preferred_element_type=F32)
            rs = slice(m * cr, (m + 1) * cr)
            if first:
                o_ref[rs, :] = down
            else:
                o_ref[rs, :] += down

    pl.when(j == 0)(lambda: step(True))
    pl.when(j > 0)(lambda: step(False))

    @pl.when(j == pl.num_programs(1) - 1)
    def _():
        if "gain" not in r:
            o_ref[...] = x_ref[...] + r["gate"][...] * o_ref[...]
            return
        gate, gain = r["gate"][...], r["gain"][...]

        def body(c, carry):
            rs = pl.ds(pl.multiple_of(c * NORM_ROWS, NORM_ROWS), NORM_ROWS)
            y = x_ref[rs, :] + gate * o_ref[rs, :]
            o_ref[rs, :] = y * lax.rsqrt(jnp.mean(y * y, axis=-1, keepdims=True) + EPS) * gain
            return carry

        lax.fori_loop(0, rows // NORM_ROWS, body, 0)


def _conv_ffn(x, mod4, w_gate, w_val, conv_w, conv_b, w_down, layer, tiles, rounding, seq_len,
              gain=None, out=None):
    n, d = x.shape
    d_ff = w_down.shape[-2]
    tm = TOKEN_TILE
    tf = FFN_COLS_ROUNDING if rounding else FFN_COLS
    nf = d_ff // tf
    val0 = nf if rounding else 0
    wg_in, wg_out = _weight_tile("wg", w_gate, layer, (d, tf), lambda i, j: (0, j), rounding)
    wv_in, wv_out = _weight_tile("wv", w_val, layer, (d, tf), lambda i, j: (0, val0 + j), rounding)
    wd_in, wd_out = _weight_tile("wd", w_down, layer, (tf, d), lambda i, j: (j, 0), rounding)
    if rounding:
        half = jax.ShapeDtypeStruct((d, d_ff), BF16)
        wg_out = (wg_out[0], wg_out[1], half)
        wv_out = (wv_out[0], pl.BlockSpec((d, tf), lambda i, j: (0, j)), half)
    conv_b3 = conv_b.reshape(conv_b.shape[0], 1, 2 * d_ff)
    x_spec = pl.BlockSpec((tm, d), tiles.rows(d, d), **({"pipeline_mode": pl.Buffered(1)} if rounding else {}))
    ins = [("x", x_spec, x),
           ("shift", tiles.mod(mod4, layer, 3, d), mod4),
           ("scale", tiles.mod(mod4, layer, 4, d), mod4),
           ("gate", tiles.mod(mod4, layer, 5, d), mod4),
           wg_in, wv_in,
           ("cwg", pl.BlockSpec((None, CONV_W, tf), lambda i, j: (layer, 0, j)), conv_w),
           ("cwv", pl.BlockSpec((None, CONV_W, tf), lambda i, j: (layer, 0, nf + j)), conv_w),
           ("cbg", pl.BlockSpec((None, 1, tf), lambda i, j: (layer, 0, j)), conv_b3),
           ("cbv", pl.BlockSpec((None, 1, tf), lambda i, j: (layer, 0, nf + j)), conv_b3),
           wd_in]
    if gain is not None:
        ins.append(("gain", pl.BlockSpec((1, d), lambda i, j: (0, 0)), gain.reshape(1, d)))
    outs = [("out", pl.BlockSpec((tm, d), tiles.rows(d, d)), jax.ShapeDtypeStruct((n, d), F32))]
    outs += [o for o in (wg_out, wv_out, wd_out) if o]
    return _call(_ffn_body, "conv_ffn", (tiles.count, nf), ins, outs,
                 [("h", pltpu.VMEM((tm, d), BF16))], [("out", out)], seq_len=seq_len)


def kernel(x_prompt, x_sample, cache_k, cache_v, state_ret, c, c_ctx, w_mod, b_mod, w_in, w_out,
           ret_log_decay, ret_gn, att_sink, w_up, conv_w, conv_b, w_down, final_gain):
    batch, seq, d = x_prompt.shape
    dec_batch, dec_seq, _ = x_sample.shape
    depth = w_in.shape[0]
    assert TOKEN_TILE % seq == 0 and dec_seq == TOKEN_TILE and 1 + dec_batch <= COND_ROWS
    assert w_in.shape[-1] == D_IN and d == D_RET + D_ATT

    cond = jnp.zeros((COND_ROWS, d), F32).at[0].set(c_ctx).at[1:1 + dec_batch].set(c)
    mod = _modulation(cond, w_mod, b_mod)
    mod4 = mod.reshape(depth, COND_ROWS, 1, N_MOD * d)
    cos, sin_signed = _rope_tables(dec_seq)

    xp = x_prompt.reshape(batch * seq, d)
    xs = x_sample.reshape(dec_batch * dec_seq, d)
    n_ctx_tiles = batch * seq // TOKEN_TILE
    ctx_first = _Tiles(0, 1, 0, 0)
    ctx_rest = _Tiles(1, n_ctx_tiles - 1, 0, 0)
    lat = _Tiles(0, dec_batch, 1, 1)

    new_k = new_v = new_s = None
    for l in range(depth):
        o = _project(xp, mod4, w_in, l, ctx_first, True, cache=(seq, depth, new_k, new_v))
        w_in_l = o["w_bf"]
        o = _project(xp, mod4, w_in_l, l, ctx_rest, False, proj=o["proj"], cache=(seq, depth, o["k"], o["v"]))
        proj, new_k, new_v = o["proj"], o["k"], o["v"]
        y_r, new_s = _retention(proj, ret_log_decay[l], ret_gn[l], seq, l, state_out=(depth, new_s))
        y_a = _context_attention(proj, att_sink[l], seq)
        o = _out_project(y_r, y_a, xp, mod4, w_out, l, ctx_first, True)
        w_out_l = o["w_bf"]
        x1 = _out_project(y_r, y_a, xp, mod4, w_out_l, l, ctx_rest, False, out=o["out"])["out"]
        gain = final_gain if l == depth - 1 else None
        o = _conv_ffn(x1, mod4, w_up, w_up, conv_w, conv_b, w_down, l, ctx_first, True, seq, gain=gain)
        w_gate_l, w_val_l, w_down_l = o["wg_bf"], o["wv_bf"], o["wd_bf"]
        xp = _conv_ffn(x1, mod4, w_gate_l, w_val_l, conv_w, conv_b, w_down_l, l, ctx_rest, False, seq,
                       gain=gain, out=o["out"])["out"]
        proj = _project(xs, mod4, w_in_l, l, lat, False)["proj"]
        (y_r,) = _retention(proj, ret_log_decay[l], ret_gn[l], dec_seq, l, state=state_ret)
        y_a = _latent_attention(proj, att_sink[l], cache_k, cache_v, l, dec_seq, cos, sin_signed)
        x1 = _out_project(y_r, y_a, xs, mod4, w_out_l, l, lat, False)["out"]
        xs = _conv_ffn(x1, mod4, w_gate_l, w_val_l, conv_w, conv_b, w_down_l, l, lat, False, dec_seq,
                       gain=gain)["out"]

    return (xp.reshape(batch, seq, d), xs.reshape(dec_batch, dec_seq, d), new_k, new_v, new_s)
```

```python
import functools

import jax
import jax.numpy as jnp
from jax import lax
from jax.experimental import pallas as pl
from jax.experimental.pallas import tpu as pltpu

F32 = jnp.float32
BF16 = jnp.bfloat16

GRID_W = 64
H_RET = 8
DK_RET = 128
DV_RET = 128
D_RET = H_RET * DV_RET
H_ATT = 8
H_KV = 2
G_ATT = H_ATT // H_KV
DH_ATT = 128
D_ATT = H_ATT * DH_ATT
WINDOW = 128
BLOCK = 128
CONV_W = 3
N_MOD = 6
ROPE_BASE = 10000.0
EPS = 1e-6
NEG_INF = -1e30

OFF_QR = 0
OFF_KR = OFF_QR + H_RET * DK_RET
OFF_VR = OFF_KR + H_RET * DK_RET
OFF_GR = OFF_VR + H_RET * DV_RET
OFF_QA = OFF_GR + H_RET * DV_RET
OFF_KA = OFF_QA + H_ATT * DH_ATT
OFF_VA = OFF_KA + H_KV * DH_ATT
D_IN = OFF_VA + H_KV * DH_ATT
D_KV = 2 * H_KV * DH_ATT

COND_ROWS = 8
TOKEN_TILE = 1024
NORM_ROWS = 256
CONV_ROWS = 512
RET_CHUNK = 256
PROJ_COLS, PROJ_COLS_ROUNDING = 1408, 512
OUT_COLS, OUT_COLS_ROUNDING = 1024, 512
FFN_COLS, FFN_COLS_ROUNDING = 512, 256
V7X_VMEM_LIMIT = 58 * 1024 * 1024


def _params(n_axes):
    return pltpu.CompilerParams(dimension_semantics=("arbitrary",) * n_axes,
                                vmem_limit_bytes=V7X_VMEM_LIMIT)


def _call(body, name, grid, ins, outs, scratch, carried=(), **static):
    in_names = [n for n, _, _ in ins]
    in_specs = [s for _, s, _ in ins]
    args = [a for _, _, a in ins]
    out_names = [n for n, _, _ in outs]
    aliases = {}
    for out_name, arr in carried:
        if arr is not None:
            aliases[len(args)] = out_names.index(out_name)
            in_names.append("carried_" + out_name)
            in_specs.append(pl.BlockSpec(memory_space=pl.ANY))
            args.append(arr)
    names = tuple(in_names + out_names + [n for n, _ in scratch])

    def kern(*refs):
        body(dict(zip(names, refs)), **static)

    res = pl.pallas_call(
        kern,
        grid=grid,
        in_specs=in_specs,
        out_specs=[s for _, s, _ in outs],
        out_shape=[s for _, _, s in outs],
        input_output_aliases=aliases,
        scratch_shapes=[s for _, s in scratch],
        compiler_params=_params(len(grid)),
        name=name,
    )(*args)
    return dict(zip(out_names, res))


def _mod_kernel(cond_ref, w_ref, b_ref, o_ref):
    a = jax.nn.silu(cond_ref[...]).astype(BF16)
    o_ref[...] = jnp.dot(a, w_ref[...].astype(BF16), preferred_element_type=F32) + b_ref[...]


def _modulation(cond, w_mod, b_mod):
    depth, d, n = w_mod.shape
    tn = 1024
    return pl.pallas_call(
        _mod_kernel,
        grid=(depth, n // tn),
        in_specs=[
            pl.BlockSpec((COND_ROWS, d), lambda l, j: (0, 0)),
            pl.BlockSpec((None, d, tn), lambda l, j: (l, 0, j)),
            pl.BlockSpec((None, 1, tn), lambda l, j: (l, 0, j)),
        ],
        out_specs=pl.BlockSpec((None, COND_ROWS, tn), lambda l, j: (l, 0, j)),
        out_shape=jax.ShapeDtypeStruct((depth, COND_ROWS, n), F32),
        compiler_params=_params(2),
        name="modulation",
    )(cond, w_mod, b_mod.reshape(depth, 1, n))


class _Tiles:
    def __init__(self, first, count, row0, row_step):
        self.first, self.count, self.row0, self.row_step = first, count, row0, row_step

    def rows(self, width, full_width=None):
        if width == full_width:
            return lambda i, j: (self.first + i, 0)
        return lambda i, j: (self.first + i, j)

    def mod(self, mod4, layer, chunk, width, col_block=lambda j: 0):
        per_chunk = mod4.shape[-1] // N_MOD // width
        return pl.BlockSpec(
            (None, None, 1, width),
            lambda i, j: (layer, self.row0 + self.row_step * (self.first + i), 0, chunk * per_chunk + col_block(j)))


def _weight_tile(name, w, layer, block, index, rounding):
    if not rounding:
        return (name, pl.BlockSpec(block, index), w), None
    spec = pl.BlockSpec((None,) + block, lambda i, j: (layer,) + index(i, j))
    out = (name + "_bf", pl.BlockSpec(block, index), jax.ShapeDtypeStruct(w.shape[1:], BF16))
    return (name, spec, w), out


def _tile(r, name):
    w = r[name][...]
    if name + "_bf" in r:
        w = w.astype(BF16)
        r[name + "_bf"][...] = w
    return w


def _modulated_norm(x, scale_ref, shift_ref):
    inv = lax.rsqrt(jnp.mean(x * x, axis=-1, keepdims=True) + EPS)
    return ((x * inv) * (1.0 + scale_ref[...]) + shift_ref[...]).astype(BF16)


def _proj_body(r, *, kv_step):
    j = pl.program_id(1)
    h_ref, proj_ref = r["h"], r["proj"]

    @pl.when(j == 0)
    def _():
        w = _tile(r, "w")
        for c in range(h_ref.shape[0] // NORM_ROWS):
            rs = slice(c * NORM_ROWS, (c + 1) * NORM_ROWS)
            h = _modulated_norm(r["x"][rs, :], r["scale"], r["shift"])
            h_ref[rs, :] = h
            proj_ref[rs, :] = jnp.dot(h, w, preferred_element_type=F32).astype(BF16)

    @pl.when(j > 0)
    def _():
        res = jnp.dot(h_ref[...], _tile(r, "w"), preferred_element_type=F32)
        proj_ref[...] = res.astype(BF16)
        if "k" in r:
            seqs, _, seq_len, _ = r["k"].shape
            kv0 = res.shape[1] - D_KV

            @pl.when(j == kv_step)
            def _():
                for dst, off in ((r["k"], kv0), (r["v"], kv0 + H_KV * DH_ATT)):
                    for s in range(seqs):
                        for hh in range(H_KV):
                            dst[s, hh] = res[s * seq_len:(s + 1) * seq_len,
                                             off + hh * DH_ATT:off + (hh + 1) * DH_ATT]


def _project(x, mod4, w, layer, tiles, rounding, proj=None, cache=None):
    n, d = x.shape
    d_in = w.shape[-1]
    tm = TOKEN_TILE
    tn = PROJ_COLS_ROUNDING if rounding else PROJ_COLS
    assert tn >= D_KV and d_in % tn == 0
    w_in, w_out = _weight_tile("w", w, layer, (d, tn), lambda i, j: (0, j), rounding)
    ins = [("x", pl.BlockSpec((tm, d), tiles.rows(d, d)), x),
           ("shift", tiles.mod(mod4, layer, 0, d), mod4),
           ("scale", tiles.mod(mod4, layer, 1, d), mod4),
           w_in]
    outs = [("proj", pl.BlockSpec((tm, tn), tiles.rows(tn, d_in)), jax.ShapeDtypeStruct((n, d_in), BF16))]
    carried = [("proj", proj)]
    if cache is not None:
        seq_len, depth, k_buf, v_buf = cache
        seqs = tm // seq_len
        shape = jax.ShapeDtypeStruct((n // seq_len, depth, H_KV, seq_len, DH_ATT), F32)
        spec = pl.BlockSpec((seqs, None, H_KV, seq_len, DH_ATT), lambda i, j: (tiles.first + i, layer, 0, 0, 0))
        outs += [("k", spec, shape), ("v", spec, shape)]
        carried += [("k", k_buf), ("v", v_buf)]
    if w_out:
        outs.append(w_out)
    return _call(_proj_body, "in_proj", (tiles.count, d_in // tn), ins, outs,
                 [("h", pltpu.VMEM((tm, d), BF16))], carried, kv_step=d_in // tn - 1)


def _ret_kernel(*refs, seq_len, chunk, has_state, emit_state, carried):
    it = iter(refs)
    ld_ref = next(it)
    q_ref, k_ref, v_ref, g_ref, gn_ref = (next(it) for _ in range(5))
    s0_ref = next(it) if has_state else None
    if carried:
        next(it)
    y_ref = next(it)
    snew_ref = next(it) if emit_state else None
    o_ref, intra_ref, qdf_ref, kdf_ref, qdb_ref, kdb_ref = (next(it) for _ in range(6))

    n_chunks = seq_len // chunk
    k_scale = DK_RET ** -0.5

    @pl.when(pl.program_id(0) == 0)
    def _():
        row = lax.broadcasted_iota(jnp.int32, (chunk, chunk), 0).astype(F32)
        col = lax.broadcasted_iota(jnp.int32, (chunk, chunk), 1).astype(F32)
        diff = row - col
        pos = lax.broadcasted_iota(jnp.int32, (chunk, DK_RET), 0).astype(F32)
        for h in range(H_RET):
            lg_f = ld_ref[0, h]
            lg_b = ld_ref[1, h]
            intra_ref[h] = (jnp.where(diff >= 0, jnp.exp(lg_f * jnp.maximum(diff, 0.0)), 0.0)
                            + jnp.where(diff <= 0, jnp.exp(lg_b * jnp.maximum(-diff, 0.0)), 0.0)) * k_scale
            qdf_ref[h] = jnp.exp(lg_f * (pos + 1.0))
            kdf_ref[h] = jnp.exp(lg_f * (chunk - 1.0 - pos)) * k_scale
            qdb_ref[h] = jnp.exp(lg_b * (chunk - pos))
            kdb_ref[h] = jnp.exp(lg_b * pos) * k_scale

    def rows(n):
        return pl.ds(n * chunk, chunk)

    def kv_update(k_b, dec, v_b):
        kd = (k_b.astype(F32) * dec).T.astype(BF16)
        return jnp.dot(kd, v_b, preferred_element_type=F32)

    for h in range(H_RET):
        cols = slice(h * DK_RET, (h + 1) * DK_RET)
        c_dec_f = jnp.exp(jnp.full((DK_RET, DV_RET), ld_ref[0, h] * chunk, F32))
        c_dec_b = jnp.exp(jnp.full((DK_RET, DV_RET), ld_ref[1, h] * chunk, F32))

        s_f = s0_ref[0, h] if has_state else jnp.zeros((DK_RET, DV_RET), F32)
        for n in range(n_chunks):
            q_b = q_ref[rows(n), cols]
            k_b = k_ref[rows(n), cols]
            v_b = v_ref[rows(n), cols]
            sc = lax.dot_general(q_b, k_b, (((1,), (1,)), ((), ())),
                                 preferred_element_type=F32) * intra_ref[h]
            o = jnp.dot(sc.astype(BF16), v_b, preferred_element_type=F32)
            if has_state or n > 0:
                o = o + jnp.dot(q_b, s_f.astype(BF16), preferred_element_type=F32) * qdf_ref[h]
            o_ref[rows(n), :] = o
            s_f = c_dec_f * s_f + kv_update(k_b, kdf_ref[h], v_b)

        s_b = s0_ref[1, h] if has_state else jnp.zeros((DK_RET, DV_RET), F32)
        for n in reversed(range(n_chunks)):
            q_b = q_ref[rows(n), cols]
            k_b = k_ref[rows(n), cols]
            v_b = v_ref[rows(n), cols]
            if has_state or n < n_chunks - 1:
                o_ref[rows(n), :] += jnp.dot(q_b, s_b.astype(BF16), preferred_element_type=F32) * qdb_ref[h]
            s_b = c_dec_b * s_b + kv_update(k_b, kdb_ref[h], v_b)

        if emit_state:
            snew_ref[0, h] = s_f
            snew_ref[1, h] = s_b

        gn = gn_ref[:, cols]
        for n in range(n_chunks):
            o = o_ref[rows(n), :]
            mu = jnp.mean(o, axis=-1, keepdims=True)
            cen = o - mu
            var = jnp.mean(cen * cen, axis=-1, keepdims=True)
            on = cen * lax.rsqrt(var + EPS) * gn
            y_ref[rows(n), cols] = (jax.nn.silu(g_ref[rows(n), cols].astype(F32)) * on).astype(y_ref.dtype)


def _retention(proj, log_decay, gn, seq_len, layer, state=None, state_out=None):
    n = proj.shape[0]
    batch = n // seq_len
    has_state = state is not None
    chunk = min(RET_CHUNK, seq_len)
    blk = lambda off: pl.BlockSpec((seq_len, D_RET), lambda b: (b, off // D_RET))
    state_spec = pl.BlockSpec((None, None, 2, H_RET, DK_RET, DV_RET), lambda b: (b, layer, 0, 0, 0, 0))
    in_specs = [
        pl.BlockSpec(memory_space=pltpu.SMEM),
        blk(OFF_QR), blk(OFF_KR), blk(OFF_VR), blk(OFF_GR),
        pl.BlockSpec((1, D_RET), lambda b: (0, 0)),
    ]
    args = [log_decay, proj, proj, proj, proj, gn.reshape(1, D_RET)]
    if has_state:
        in_specs.append(state_spec)
        args.append(state)
    out_specs = [pl.BlockSpec((seq_len, D_RET), lambda b: (b, 0))]
    out_shape = [jax.ShapeDtypeStruct((n, D_RET), BF16)]
    aliases = {}
    if state_out is not None:
        depth, buf = state_out
        out_specs.append(state_spec)
        out_shape.append(jax.ShapeDtypeStruct((batch, depth, 2, H_RET, DK_RET, DV_RET), F32))
        if buf is not None:
            aliases[len(args)] = 1
            in_specs.append(pl.BlockSpec(memory_space=pl.ANY))
            args.append(buf)
    kern = functools.partial(_ret_kernel, seq_len=seq_len, chunk=chunk, has_state=has_state,
                             emit_state=state_out is not None, carried=bool(aliases))
    return pl.pallas_call(
        kern,
        grid=(batch,),
        in_specs=in_specs,
        out_specs=out_specs,
        out_shape=out_shape,
        input_output_aliases=aliases,
        scratch_shapes=[
            pltpu.VMEM((seq_len, DV_RET), F32),
            pltpu.VMEM((H_RET, chunk, chunk), F32),
            pltpu.VMEM((H_RET, chunk, DK_RET), F32),
            pltpu.VMEM((H_RET, chunk, DK_RET), F32),
            pltpu.VMEM((H_RET, chunk, DK_RET), F32),
            pltpu.VMEM((H_RET, chunk, DK_RET), F32),
        ],
        compiler_params=_params(1),
        name="retention",
    )(*args)


def _softmax_parts(scores_t, sink_row):
    m = jnp.maximum(sink_row, functools.reduce(
        jnp.maximum, [jnp.max(s, axis=0, keepdims=True) for s in scores_t]))
    ps = [jnp.exp(s - m) for s in scores_t]
    denom = jnp.exp(sink_row - m) + functools.reduce(
        lambda a, b: a + b, [jnp.sum(p, axis=0, keepdims=True) for p in ps])
    return ps, denom


def _sink_row(sink_ref, head0, width):
    lane_head = lax.broadcasted_iota(jnp.int32, (1, G_ATT * width), 1) // width
    row = jnp.full((1, G_ATT * width), sink_ref[head0], F32)
    for g in range(1, G_ATT):
        row = jnp.where(lane_head == g, sink_ref[head0 + g], row)
    return row


def _transposed(x):
    return x.astype(F32).T.astype(BF16)


_NT = (((1,), (1,)), ((), ()))


def _ctx_attn_kernel(sink_ref, q_ref, k_ref, v_ref, o_ref):
    seq_len = q_ref.shape[0]
    scale = DH_ATT ** -0.5
    for kv_head in range(H_KV):
        kv_cols = slice(kv_head * DH_ATT, (kv_head + 1) * DH_ATT)
        heads = range(kv_head * G_ATT, (kv_head + 1) * G_ATT)
        q4 = jnp.concatenate([q_ref[:, h * DH_ATT:(h + 1) * DH_ATT] for h in heads], axis=0)
        s_t = lax.dot_general(k_ref[:, kv_cols], q4, _NT, preferred_element_type=F32) * scale
        (p,), denom = _softmax_parts([s_t], _sink_row(sink_ref, kv_head * G_ATT, seq_len))
        o_t = jnp.dot(_transposed(v_ref[:, kv_cols]), p.astype(BF16), preferred_element_type=F32) / denom
        for g, h in enumerate(heads):
            o_ref[:, h * DH_ATT:(h + 1) * DH_ATT] = o_t[:, g * seq_len:(g + 1) * seq_len].T.astype(o_ref.dtype)


def _context_attention(proj, sink, seq_len):
    n = proj.shape[0]
    kvw = H_KV * DH_ATT
    return pl.pallas_call(
        _ctx_attn_kernel,
        grid=(n // seq_len,),
        in_specs=[
            pl.BlockSpec(memory_space=pltpu.SMEM),
            pl.BlockSpec((seq_len, D_ATT), lambda b: (b, OFF_QA // D_ATT)),
            pl.BlockSpec((seq_len, kvw), lambda b: (b, OFF_KA // kvw)),
            pl.BlockSpec((seq_len, kvw), lambda b: (b, OFF_VA // kvw)),
        ],
        out_specs=pl.BlockSpec((seq_len, D_ATT), lambda b: (b, 0)),
        out_shape=jax.ShapeDtypeStruct((n, D_ATT), BF16),
        compiler_params=_params(1),
        name="context_attention",
    )(sink, proj, proj, proj)


def _rope(x, cos, sin_signed):
    quarter = DH_ATT // 4
    lane = lax.broadcasted_iota(jnp.int32, x.shape, 1)
    first = (lane % (2 * quarter)) < quarter
    partner = jnp.where(first, pltpu.roll(x, DH_ATT - quarter, axis=1), pltpu.roll(x, quarter, axis=1))
    return x * cos + partner * sin_signed


def _lat_attn_kernel(sink_ref, q_ref, k_ref, v_ref, ck_ref, cv_ref, cos_ref, sin_ref, o_ref,
                     kr_ref, vt_ref, ckb_ref, cvt_ref, *, seq_len):
    kv_head = pl.program_id(1)
    qb = pl.program_id(2)
    scale = DH_ATT ** -0.5
    span_blocks = 1 + 2 * WINDOW // BLOCK
    span = span_blocks * BLOCK

    @pl.when(qb == 0)
    def _():
        kr_ref[...] = _rope(k_ref[...].astype(F32), cos_ref[...], sin_ref[...]).astype(BF16)
        for blk in range(seq_len // BLOCK):
            vt_ref[blk] = _transposed(v_ref[blk * BLOCK:(blk + 1) * BLOCK, :])
        ckb_ref[...] = ck_ref[...].astype(BF16)
        cvt_ref[...] = _transposed(cv_ref[...])

    q_rows = pl.ds(pl.multiple_of(qb * BLOCK, BLOCK), BLOCK)
    cos_q = cos_ref[q_rows, :]
    sin_q = sin_ref[q_rows, :]
    q4 = jnp.concatenate(
        [_rope(q_ref[:, g * DH_ATT:(g + 1) * DH_ATT].astype(F32), cos_q, sin_q) for g in range(G_ATT)],
        axis=0).astype(BF16)

    first_blk = jnp.clip(qb - WINDOW // BLOCK, 0, seq_len // BLOCK - span_blocks)
    start = pl.multiple_of(first_blk * BLOCK, BLOCK)
    k_win = kr_ref[pl.ds(start, span), :]
    vt_win = jnp.concatenate([vt_ref[first_blk + t] for t in range(span_blocks)], axis=1)
    k_pos = start + lax.broadcasted_iota(jnp.int32, (span, BLOCK), 0)
    q_pos = qb * BLOCK + lax.broadcasted_iota(jnp.int32, (span, BLOCK), 1)
    band = jnp.where(jnp.abs(k_pos - q_pos) <= WINDOW, 1.0, 0.0)
    band4 = jnp.concatenate([band] * G_ATT, axis=1)

    s_loc = lax.dot_general(k_win, q4, _NT, preferred_element_type=F32) * scale
    s_loc = jnp.where(band4 > 0.5, s_loc, NEG_INF)
    s_ctx = lax.dot_general(ckb_ref[...], q4, _NT, preferred_element_type=F32) * scale
    (p_loc, p_ctx), denom = _softmax_parts([s_loc, s_ctx], _sink_row(sink_ref, kv_head * G_ATT, BLOCK))
    o_t = (jnp.dot(vt_win, p_loc.astype(BF16), preferred_element_type=F32)
           + jnp.dot(cvt_ref[...], p_ctx.astype(BF16), preferred_element_type=F32)) / denom
    for g in range(G_ATT):
        o_ref[:, g * DH_ATT:(g + 1) * DH_ATT] = o_t[:, g * BLOCK:(g + 1) * BLOCK].T.astype(o_ref.dtype)


def _latent_attention(proj, sink, cache_k, cache_v, layer, seq_len, cos, sin_signed):
    n = proj.shape[0]
    batch = n // seq_len
    nq = seq_len // BLOCK
    qw = G_ATT * DH_ATT
    past = cache_k.shape[3]
    cache_spec = pl.BlockSpec((None, None, None, past, DH_ATT), lambda b, h, i: (b, layer, h, 0, 0))
    return pl.pallas_call(
        functools.partial(_lat_attn_kernel, seq_len=seq_len),
        grid=(batch, H_KV, nq),
        in_specs=[
            pl.BlockSpec(memory_space=pltpu.SMEM),
            pl.BlockSpec((BLOCK, qw), lambda b, h, i: (b * nq + i, OFF_QA // qw + h)),
            pl.BlockSpec((seq_len, DH_ATT), lambda b, h, i: (b, OFF_KA // DH_ATT + h)),
            pl.BlockSpec((seq_len, DH_ATT), lambda b, h, i: (b, OFF_VA // DH_ATT + h)),
            cache_spec, cache_spec,
            pl.BlockSpec((seq_len, DH_ATT), lambda b, h, i: (0, 0)),
            pl.BlockSpec((seq_len, DH_ATT), lambda b, h, i: (0, 0)),
        ],
        out_specs=pl.BlockSpec((BLOCK, qw), lambda b, h, i: (b * nq + i, h)),
        out_shape=jax.ShapeDtypeStruct((n, D_ATT), BF16),
        scratch_shapes=[pltpu.VMEM((seq_len, DH_ATT), BF16),
                        pltpu.VMEM((seq_len // BLOCK, DH_ATT, BLOCK), BF16),
                        pltpu.VMEM((past, DH_ATT), BF16),
                        pltpu.VMEM((DH_ATT, past), BF16)],
        compiler_params=_params(3),
        name="latent_attention",
    )(sink, proj, proj, proj, cache_k, cache_v, cos, sin_signed)


def _rope_tables(seq_len):
    rows = seq_len // GRID_W
    row = jnp.repeat(jnp.arange(rows), GRID_W)
    col = jnp.tile(jnp.arange(GRID_W), rows)
    quarter = DH_ATT // 4
    inv = ROPE_BASE ** (-jnp.arange(quarter, dtype=F32) / quarter)
    ang_r, ang_c = row[:, None] * inv, col[:, None] * inv
    cos = jnp.concatenate([jnp.cos(ang_r)] * 2 + [jnp.cos(ang_c)] * 2, axis=-1)
    sin_signed = jnp.concatenate([-jnp.sin(ang_r), jnp.sin(ang_r), -jnp.sin(ang_c), jnp.sin(ang_c)], axis=-1)
    return cos.astype(F32), sin_signed.astype(F32)


def _out_proj_body(r):
    @pl.when(pl.program_id(1) == 0)
    def _():
        r["y"][:, :D_RET] = r["y_r"][...]
        r["y"][:, D_RET:] = r["y_a"][...]

    mix = jnp.dot(r["y"][...], _tile(r, "w"), preferred_element_type=F32)
    r["out"][...] = r["x"][...] + r["gate"][...] * mix


def _out_project(y_r, y_a, x, mod4, w, layer, tiles, rounding, out=None):
    n, d = x.shape
    tm = TOKEN_TILE
    tn = OUT_COLS_ROUNDING if rounding else OUT_COLS
    w_in, w_out = _weight_tile("w", w, layer, (D_RET + D_ATT, tn), lambda i, j: (0, j), rounding)
    ins = [("y_r", pl.BlockSpec((tm, D_RET), tiles.rows(D_RET, D_RET)), y_r),
           ("y_a", pl.BlockSpec((tm, D_ATT), tiles.rows(D_ATT, D_ATT)), y_a),
           w_in,
           ("x", pl.BlockSpec((tm, tn), tiles.rows(tn, d)), x),
           ("gate", tiles.mod(mod4, layer, 2, tn, lambda j: j), mod4)]
    outs = [("out", pl.BlockSpec((tm, tn), tiles.rows(tn, d)), jax.ShapeDtypeStruct((n, d), F32))]
    if w_out:
        outs.append(w_out)
    return _call(_out_proj_body, "out_proj", (tiles.count, d // tn), ins, outs,
                 [("y", pltpu.VMEM((tm, D_RET + D_ATT), BF16))], [("out", out)])


def _ffn_body(r, *, seq_len):
    j = pl.program_id(1)
    x_ref, o_ref, h_ref = r["x"], r["out"], r["h"]
    rows = x_ref.shape[0]

    cr = CONV_ROWS
    n_chunks = rows // cr

    def step(first):
        w_gate, w_val, w_down = _tile(r, "wg"), _tile(r, "wv"), _tile(r, "wd")
        t = lax.broadcasted_iota(jnp.int32, (cr, w_gate.shape[1]), 0)
        ups = []
        for m in range(n_chunks):
            rs = slice(m * cr, (m + 1) * cr)
            if first:
                for c in range(m * cr // NORM_ROWS, (m + 1) * cr // NORM_ROWS):
                    ns = slice(c * NORM_ROWS, (c + 1) * NORM_ROWS)
                    h_ref[ns, :] = _modulated_norm(x_ref[ns, :], r["scale"], r["shift"])
            hm = h_ref[rs, :]
            ups.append(tuple(jnp.dot(hm, w, preferred_element_type=F32) for w in (w_gate, w_val)))

        for m in range(n_chunks):
            seq_pos = (t + (m * cr) % seq_len) % seq_len
            seq_start, seq_end = seq_pos == 0, seq_pos == seq_len - 1

            def conv_branch(b, cw_ref, cb_ref):
                up = ups[m][b]
                prev = pltpu.roll(up, 1, axis=0)
                nxt = pltpu.roll(up, cr - 1, axis=0)
                if (m * cr) % seq_len:
                    prev = jnp.where(t == 0, ups[m - 1][b][cr - 1:cr, :], prev)
                if ((m + 1) * cr) % seq_len:
                    nxt = jnp.where(t == cr - 1, ups[m + 1][b][0:1, :], nxt)
                prev = jnp.where(seq_start, 0.0, prev)
                nxt = jnp.where(seq_end, 0.0, nxt)
                return cb_ref[...] + prev * cw_ref[0:1, :] + up * cw_ref[1:2, :] + nxt * cw_ref[2:3, :]

            act = jax.nn.silu(conv_branch(0, r["cwg"], r["cbg"])) * conv_branch(1, r["cwv"], r["cbv"])
            down = jnp.dot(act.astype(BF16), w_down, preferred_element_type=F32)
            rs = slice(m * cr, (m + 1) * cr)
            if first:
                o_ref[rs, :] = down
            else:
                o_ref[rs, :] += down

    pl.when(j == 0)(lambda: step(True))
    pl.when(j > 0)(lambda: step(False))

    @pl.when(j == pl.num_programs(1) - 1)
    def _():
        if "gain" not in r:
            o_ref[...] = x_ref[...] + r["gate"][...] * o_ref[...]
            return
        gate, gain = r["gate"][...], r["gain"][...]

        def body(c, carry):
            rs = pl.ds(pl.multiple_of(c * NORM_ROWS, NORM_ROWS), NORM_ROWS)
            y = x_ref[rs, :] + gate * o_ref[rs, :]
            o_ref[rs, :] = y * lax.rsqrt(jnp.mean(y * y, axis=-1, keepdims=True) + EPS) * gain
            return carry

        lax.fori_loop(0, rows // NORM_ROWS, body, 0)


def _conv_ffn(x, mod4, w_gate, w_val, conv_w, conv_b, w_down, layer, tiles, rounding, seq_len,
              gain=None, out=None):
    n, d = x.shape
    d_ff = w_down.shape[-2]
    tm = TOKEN_TILE
    tf = FFN_COLS_ROUNDING if rounding else FFN_COLS
    nf = d_ff // tf
    val0 = nf if rounding else 0
    wg_in, wg_out = _weight_tile("wg", w_gate, layer, (d, tf), lambda i, j: (0, j), rounding)
    wv_in, wv_out = _weight_tile("wv", w_val, layer, (d, tf), lambda i, j: (0, val0 + j), rounding)
    wd_in, wd_out = _weight_tile("wd", w_down, layer, (tf, d), lambda i, j: (j, 0), rounding)
    if rounding:
        half = jax.ShapeDtypeStruct((d, d_ff), BF16)
        wg_out = (wg_out[0], wg_out[1], half)
        wv_out = (wv_out[0], pl.BlockSpec((d, tf), lambda i, j: (0, j)), half)
    conv_b3 = conv_b.reshape(conv_b.shape[0], 1, 2 * d_ff)
    x_spec = pl.BlockSpec((tm, d), tiles.rows(d, d), **({"pipeline_mode": pl.Buffered(1)} if rounding else {}))
    ins = [("x", x_spec, x),
           ("shift", tiles.mod(mod4, layer, 3, d), mod4),
           ("scale", tiles.mod(mod4, layer, 4, d), mod4),
           ("gate", tiles.mod(mod4, layer, 5, d), mod4),
           wg_in, wv_in,
           ("cwg", pl.BlockSpec((None, CONV_W, tf), lambda i, j: (layer, 0, j)), conv_w),
           ("cwv", pl.BlockSpec((None, CONV_W, tf), lambda i, j: (layer, 0, nf + j)), conv_w),
           ("cbg", pl.BlockSpec((None, 1, tf), lambda i, j: (layer, 0, j)), conv_b3),
           ("cbv", pl.BlockSpec((None, 1, tf), lambda i, j: (layer, 0, nf + j)), conv_b3),
           wd_in]
    if gain is not None:
        ins.append(("gain", pl.BlockSpec((1, d), lambda i, j: (0, 0)), gain.reshape(1, d)))
    outs = [("out", pl.BlockSpec((tm, d), tiles.rows(d, d)), jax.ShapeDtypeStruct((n, d), F32))]
    outs += [o for o in (wg_out, wv_out, wd_out) if o]
    return _call(_ffn_body, "conv_ffn", (tiles.count, nf), ins, outs,
                 [("h", pltpu.VMEM((tm, d), BF16))], [("out", out)], seq_len=seq_len)


def kernel(x_prompt, x_sample, cache_k, cache_v, state_ret, c, c_ctx, w_mod, b_mod, w_in, w_out,
           ret_log_decay, ret_gn, att_sink, w_up, conv_w, conv_b, w_down, final_gain):
    batch, seq, d = x_prompt.shape
    dec_batch, dec_seq, _ = x_sample.shape
    depth = w_in.shape[0]
    assert TOKEN_TILE % seq == 0 and dec_seq == TOKEN_TILE and 1 + dec_batch <= COND_ROWS
    assert w_in.shape[-1] == D_IN and d == D_RET + D_ATT

    cond = jnp.zeros((COND_ROWS, d), F32).at[0].set(c_ctx).at[1:1 + dec_batch].set(c)
    mod = _modulation(cond, w_mod, b_mod)
    mod4 = mod.reshape(depth, COND_ROWS, 1, N_MOD * d)
    cos, sin_signed = _rope_tables(dec_seq)

    xp = x_prompt.reshape(batch * seq, d)
    xs = x_sample.reshape(dec_batch * dec_seq, d)
    n_ctx_tiles = batch * seq // TOKEN_TILE
    ctx_first = _Tiles(0, 1, 0, 0)
    ctx_rest = _Tiles(1, n_ctx_tiles - 1, 0, 0)
    lat = _Tiles(0, dec_batch, 1, 1)

    new_k = new_v = new_s = None
    for l in range(depth):
        o = _project(xp, mod4, w_in, l, ctx_first, True, cache=(seq, depth, new_k, new_v))
        w_in_l = o["w_bf"]
        o = _project(xp, mod4, w_in_l, l, ctx_rest, False, proj=o["proj"], cache=(seq, depth, o["k"], o["v"]))
        proj, new_k, new_v = o["proj"], o["k"], o["v"]
        y_r, new_s = _retention(proj, ret_log_decay[l], ret_gn[l], seq, l, state_out=(depth, new_s))
        y_a = _context_attention(proj, att_sink[l], seq)
        o = _out_project(y_r, y_a, xp, mod4, w_out, l, ctx_first, True)
        w_out_l = o["w_bf"]
        x1 = _out_project(y_r, y_a, xp, mod4, w_out_l, l, ctx_rest, False, out=o["out"])["out"]
        gain = final_gain if l == depth - 1 else None
        o = _conv_ffn(x1, mod4, w_up, w_up, conv_w, conv_b, w_down, l, ctx_first, True, seq, gain=gain)
        w_gate_l, w_val_l, w_down_l = o["wg_bf"], o["wv_bf"], o["wd_bf"]
        xp = _conv_ffn(x1, mod4, w_gate_l, w_val_l, conv_w, conv_b, w_down_l, l, ctx_rest, False, seq,
                       gain=gain, out=o["out"])["out"]
        proj = _project(xs, mod4, w_in_l, l, lat, False)["proj"]
        (y_r,) = _retention(proj, ret_log_decay[l], ret_gn[l], dec_seq, l, state=state_ret)
        y_a = _latent_attention(proj, att_sink[l], cache_k, cache_v, l, dec_seq, cos, sin_signed)
        x1 = _out_project(y_r, y_a, xs, mod4, w_out_l, l, lat, False)["out"]
        xs = _conv_ffn(x1, mod4, w_gate_l, w_val_l, conv_w, conv_b, w_down_l, l, lat, False, dec_seq,
                       gain=gain)["out"]

    return (xp.reshape(batch, seq, d), xs.reshape(dec_batch, dec_seq, d), new_k, new_v, new_s)
```

```python
import functools

import jax
import jax.numpy as jnp
from jax import lax
from jax.experimental import pallas as pl
from jax.experimental.pallas import tpu as pltpu

F32 = jnp.float32
BF16 = jnp.bfloat16

GRID_W = 64
H_RET = 8
DK_RET = 128
DV_RET = 128
D_RET = H_RET * DV_RET
H_ATT = 8
H_KV = 2
G_ATT = H_ATT // H_KV
DH_ATT = 128
D_ATT = H_ATT * DH_ATT
WINDOW = 128
BLOCK = 128
CONV_W = 3
N_MOD = 6
ROPE_BASE = 10000.0
EPS = 1e-6
NEG_INF = -1e30

OFF_QR = 0
OFF_KR = OFF_QR + H_RET * DK_RET
OFF_VR = OFF_KR + H_RET * DK_RET
OFF_GR = OFF_VR + H_RET * DV_RET
OFF_QA = OFF_GR + H_RET * DV_RET
OFF_KA = OFF_QA + H_ATT * DH_ATT
OFF_VA = OFF_KA + H_KV * DH_ATT
D_IN = OFF_VA + H_KV * DH_ATT
D_KV = 2 * H_KV * DH_ATT

COND_ROWS = 8
TOKEN_TILE = 1024
NORM_ROWS = 256
CONV_ROWS = 512
RET_CHUNK = 256
PROJ_COLS, PROJ_COLS_ROUNDING = 1408, 512
OUT_COLS, OUT_COLS_ROUNDING = 1024, 512
FFN_COLS = 512
V7X_VMEM_LIMIT = 58 * 1024 * 1024


def _params(n_axes):
    return pltpu.CompilerParams(dimension_semantics=("arbitrary",) * n_axes,
                                vmem_limit_bytes=V7X_VMEM_LIMIT)


def _call(body, name, grid, ins, outs, scratch, carried=(), **static):
    in_names = [n for n, _, _ in ins]
    in_specs = [s for _, s, _ in ins]
    args = [a for _, _, a in ins]
    out_names = [n for n, _, _ in outs]
    aliases = {}
    for out_name, arr in carried:
        if arr is not None:
            aliases[len(args)] = out_names.index(out_name)
            in_names.append("carried_" + out_name)
            in_specs.append(pl.BlockSpec(memory_space=pl.ANY))
            args.append(arr)
    names = tuple(in_names + out_names + [n for n, _ in scratch])

    def kern(*refs):
        body(dict(zip(names, refs)), **static)

    res = pl.pallas_call(
        kern,
        grid=grid,
        in_specs=in_specs,
        out_specs=[s for _, s, _ in outs],
        out_shape=[s for _, _, s in outs],
        input_output_aliases=aliases,
        scratch_shapes=[s for _, s in scratch],
        compiler_params=_params(len(grid)),
        name=name,
    )(*args)
    return dict(zip(out_names, res))


def _mod_kernel(cond_ref, w_ref, b_ref, o_ref):
    a = jax.nn.silu(cond_ref[...]).astype(BF16)
    o_ref[...] = jnp.dot(a, w_ref[...].astype(BF16), preferred_element_type=F32) + b_ref[...]


def _modulation(cond, w_mod, b_mod):
    depth, d, n = w_mod.shape
    tn = 1024
    return pl.pallas_call(
        _mod_kernel,
        grid=(depth, n // tn),
        in_specs=[
            pl.BlockSpec((COND_ROWS, d), lambda l, j: (0, 0)),
            pl.BlockSpec((None, d, tn), lambda l, j: (l, 0, j)),
            pl.BlockSpec((None, 1, tn), lambda l, j: (l, 0, j)),
        ],
        out_specs=pl.BlockSpec((None, COND_ROWS, tn), lambda l, j: (l, 0, j)),
        out_shape=jax.ShapeDtypeStruct((depth, COND_ROWS, n), F32),
        compiler_params=_params(2),
        name="modulation",
    )(cond, w_mod, b_mod.reshape(depth, 1, n))


class _Tiles:
    def __init__(self, first, count, row0, row_step):
        self.first, self.count, self.row0, self.row_step = first, count, row0, row_step

    def rows(self, width, full_width=None):
        if width == full_width:
            return lambda i, j: (self.first + i, 0)
        return lambda i, j: (self.first + i, j)

    def mod(self, mod4, layer, chunk, width, col_block=lambda j: 0):
        per_chunk = mod4.shape[-1] // N_MOD // width
        return pl.BlockSpec(
            (None, None, 1, width),
            lambda i, j: (layer, self.row0 + self.row_step * (self.first + i), 0, chunk * per_chunk + col_block(j)))


def _weight_tile(name, w, layer, block, index, rounding):
    if not rounding:
        return (name, pl.BlockSpec(block, index), w), None
    spec = pl.BlockSpec((None,) + block, lambda i, j: (layer,) + index(i, j))
    out = (name + "_bf", pl.BlockSpec(block, index), jax.ShapeDtypeStruct(w.shape[1:], BF16))
    return (name, spec, w), out


def _tile(r, name):
    w = r[name][...]
    if name + "_bf" in r:
        w = w.astype(BF16)
        r[name + "_bf"][...] = w
    return w


def _modulated_norm(x, scale_ref, shift_ref):
    inv = lax.rsqrt(jnp.mean(x * x, axis=-1, keepdims=True) + EPS)
    return ((x * inv) * (1.0 + scale_ref[...]) + shift_ref[...]).astype(BF16)


def _proj_body(r, *, kv_step):
    j = pl.program_id(1)
    h_ref, proj_ref = r["h"], r["proj"]

    @pl.when(j == 0)
    def _():
        w = _tile(r, "w")
        for c in range(h_ref.shape[0] // NORM_ROWS):
            rs = slice(c * NORM_ROWS, (c + 1) * NORM_ROWS)
            h = _modulated_norm(r["x"][rs, :], r["scale"], r["shift"])
            h_ref[rs, :] = h
            proj_ref[rs, :] = jnp.dot(h, w, preferred_element_type=F32).astype(BF16)

    @pl.when(j > 0)
    def _():
        res = jnp.dot(h_ref[...], _tile(r, "w"), preferred_element_type=F32)
        proj_ref[...] = res.astype(BF16)
        if "k" in r:
            seqs, _, seq_len, _ = r["k"].shape
            kv0 = res.shape[1] - D_KV

            @pl.when(j == kv_step)
            def _():
                for dst, off in ((r["k"], kv0), (r["v"], kv0 + H_KV * DH_ATT)):
                    for s in range(seqs):
                        for hh in range(H_KV):
                            dst[s, hh] = res[s * seq_len:(s + 1) * seq_len,
                                             off + hh * DH_ATT:off + (hh + 1) * DH_ATT]


def _project(x, mod4, w, layer, tiles, rounding, proj=None, cache=None):
    n, d = x.shape
    d_in = w.shape[-1]
    tm = TOKEN_TILE
    tn = PROJ_COLS_ROUNDING if rounding else PROJ_COLS
    assert tn >= D_KV and d_in % tn == 0
    w_in, w_out = _weight_tile("w", w, layer, (d, tn), lambda i, j: (0, j), rounding)
    ins = [("x", pl.BlockSpec((tm, d), tiles.rows(d, d)), x),
           ("shift", tiles.mod(mod4, layer, 0, d), mod4),
           ("scale", tiles.mod(mod4, layer, 1, d), mod4),
           w_in]
    outs = [("proj", pl.BlockSpec((tm, tn), tiles.rows(tn, d_in)), jax.ShapeDtypeStruct((n, d_in), BF16))]
    carried = [("proj", proj)]
    if cache is not None:
        seq_len, depth, k_buf, v_buf = cache
        seqs = tm // seq_len
        shape = jax.ShapeDtypeStruct((n // seq_len, depth, H_KV, seq_len, DH_ATT), F32)
        spec = pl.BlockSpec((seqs, None, H_KV, seq_len, DH_ATT), lambda i, j: (tiles.first + i, layer, 0, 0, 0))
        outs += [("k", spec, shape), ("v", spec, shape)]
        carried += [("k", k_buf), ("v", v_buf)]
    if w_out:
        outs.append(w_out)
    return _call(_proj_body, "in_proj", (tiles.count, d_in // tn), ins, outs,
                 [("h", pltpu.VMEM((tm, d), BF16))], carried, kv_step=d_in // tn - 1)


def _ret_kernel(*refs, seq_len, chunk, has_state, emit_state, carried, n_riders):
    it = iter(refs)
    ld_ref = next(it)
    q_ref, k_ref, v_ref, g_ref, gn_ref = (next(it) for _ in range(5))
    s0_ref = next(it) if has_state else None
    riders_in = [next(it) for _ in range(n_riders)]
    if carried:
        next(it)
    y_ref = next(it)
    snew_ref = next(it) if emit_state else None
    _round_riders(riders_in, [next(it) for _ in range(n_riders)])
    o_ref, intra_ref, qdf_ref, kdf_ref, qdb_ref, kdb_ref = (next(it) for _ in range(6))

    n_chunks = seq_len // chunk
    k_scale = DK_RET ** -0.5

    @pl.when(pl.program_id(0) == 0)
    def _():
        row = lax.broadcasted_iota(jnp.int32, (chunk, chunk), 0).astype(F32)
        col = lax.broadcasted_iota(jnp.int32, (chunk, chunk), 1).astype(F32)
        diff = row - col
        pos = lax.broadcasted_iota(jnp.int32, (chunk, DK_RET), 0).astype(F32)
        for h in range(H_RET):
            lg_f = ld_ref[0, h]
            lg_b = ld_ref[1, h]
            intra_ref[h] = (jnp.where(diff >= 0, jnp.exp(lg_f * jnp.maximum(diff, 0.0)), 0.0)
                            + jnp.where(diff <= 0, jnp.exp(lg_b * jnp.maximum(-diff, 0.0)), 0.0)) * k_scale
            qdf_ref[h] = jnp.exp(lg_f * (pos + 1.0))
            kdf_ref[h] = jnp.exp(lg_f * (chunk - 1.0 - pos)) * k_scale
            qdb_ref[h] = jnp.exp(lg_b * (chunk - pos))
            kdb_ref[h] = jnp.exp(lg_b * pos) * k_scale

    def rows(n):
        return pl.ds(n * chunk, chunk)

    def kv_update(k_b, dec, v_b):
        kd = (k_b.astype(F32) * dec).T.astype(BF16)
        return jnp.dot(kd, v_b, preferred_element_type=F32)

    for h in range(H_RET):
        cols = slice(h * DK_RET, (h + 1) * DK_RET)
        c_dec_f = jnp.exp(jnp.full((DK_RET, DV_RET), ld_ref[0, h] * chunk, F32))
        c_dec_b = jnp.exp(jnp.full((DK_RET, DV_RET), ld_ref[1, h] * chunk, F32))

        s_f = s0_ref[0, h] if has_state else jnp.zeros((DK_RET, DV_RET), F32)
        for n in range(n_chunks):
            q_b = q_ref[rows(n), cols]
            k_b = k_ref[rows(n), cols]
            v_b = v_ref[rows(n), cols]
            sc = lax.dot_general(q_b, k_b, (((1,), (1,)), ((), ())),
                                 preferred_element_type=F32) * intra_ref[h]
            o = jnp.dot(sc.astype(BF16), v_b, preferred_element_type=F32)
            if has_state or n > 0:
                o = o + jnp.dot(q_b, s_f.astype(BF16), preferred_element_type=F32) * qdf_ref[h]
            o_ref[rows(n), :] = o
            s_f = c_dec_f * s_f + kv_update(k_b, kdf_ref[h], v_b)

        s_b = s0_ref[1, h] if has_state else jnp.zeros((DK_RET, DV_RET), F32)
        for n in reversed(range(n_chunks)):
            q_b = q_ref[rows(n), cols]
            k_b = k_ref[rows(n), cols]
            v_b = v_ref[rows(n), cols]
            if has_state or n < n_chunks - 1:
                o_ref[rows(n), :] += jnp.dot(q_b, s_b.astype(BF16), preferred_element_type=F32) * qdb_ref[h]
            s_b = c_dec_b * s_b + kv_update(k_b, kdb_ref[h], v_b)

        if emit_state:
            snew_ref[0, h] = s_f
            snew_ref[1, h] = s_b

        gn = gn_ref[:, cols]
        for n in range(n_chunks):
            o = o_ref[rows(n), :]
            mu = jnp.mean(o, axis=-1, keepdims=True)
            cen = o - mu
            var = jnp.mean(cen * cen, axis=-1, keepdims=True)
            on = cen * lax.rsqrt(var + EPS) * gn
            y_ref[rows(n), cols] = (jax.nn.silu(g_ref[rows(n), cols].astype(F32)) * on).astype(y_ref.dtype)


def _retention(proj, log_decay, gn, seq_len, layer, state=None, state_out=None, riders=()):
    n = proj.shape[0]
    batch = n // seq_len
    has_state = state is not None
    chunk = min(RET_CHUNK, seq_len)
    blk = lambda off: pl.BlockSpec((seq_len, D_RET), lambda b: (b, off // D_RET))
    state_spec = pl.BlockSpec((None, None, 2, H_RET, DK_RET, DV_RET), lambda b: (b, layer, 0, 0, 0, 0))
    in_specs = [
        pl.BlockSpec(memory_space=pltpu.SMEM),
        blk(OFF_QR), blk(OFF_KR), blk(OFF_VR), blk(OFF_GR),
        pl.BlockSpec((1, D_RET), lambda b: (0, 0)),
    ]
    args = [log_decay, proj, proj, proj, proj, gn.reshape(1, D_RET)]
    if has_state:
        in_specs.append(state_spec)
        args.append(state)
    rider_specs = [_rider(w, w_layer, batch, cb) for w, w_layer, cb in riders]
    in_specs += [s[0] for s in rider_specs]
    args += [w for w, _, _ in riders]
    out_specs = [pl.BlockSpec((seq_len, D_RET), lambda b: (b, 0))]
    out_shape = [jax.ShapeDtypeStruct((n, D_RET), BF16)]
    aliases = {}
    if state_out is not None:
        depth, buf = state_out
        out_specs.append(state_spec)
        out_shape.append(jax.ShapeDtypeStruct((batch, depth, 2, H_RET, DK_RET, DV_RET), F32))
        if buf is not None:
            aliases[len(args)] = 1
            in_specs.append(pl.BlockSpec(memory_space=pl.ANY))
            args.append(buf)
    out_specs += [s[1] for s in rider_specs]
    out_shape += [s[2] for s in rider_specs]
    kern = functools.partial(_ret_kernel, seq_len=seq_len, chunk=chunk, has_state=has_state,
                             emit_state=state_out is not None, carried=bool(aliases),
                             n_riders=len(riders))
    return pl.pallas_call(
        kern,
        grid=(batch,),
        in_specs=in_specs,
        out_specs=out_specs,
        out_shape=out_shape,
        input_output_aliases=aliases,
        scratch_shapes=[
            pltpu.VMEM((seq_len, DV_RET), F32),
            pltpu.VMEM((H_RET, chunk, chunk), F32),
            pltpu.VMEM((H_RET, chunk, DK_RET), F32),
            pltpu.VMEM((H_RET, chunk, DK_RET), F32),
            pltpu.VMEM((H_RET, chunk, DK_RET), F32),
            pltpu.VMEM((H_RET, chunk, DK_RET), F32),
        ],
        compiler_params=_params(1),
        name="retention",
    )(*args)


def _softmax_parts(scores_t, sink_row):
    m = jnp.maximum(sink_row, functools.reduce(
        jnp.maximum, [jnp.max(s, axis=0, keepdims=True) for s in scores_t]))
    ps = [jnp.exp(s - m) for s in scores_t]
    denom = jnp.exp(sink_row - m) + functools.reduce(
        lambda a, b: a + b, [jnp.sum(p, axis=0, keepdims=True) for p in ps])
    return ps, denom


def _sink_row(sink_ref, head0, width):
    lane_head = lax.broadcasted_iota(jnp.int32, (1, G_ATT * width), 1) // width
    row = jnp.full((1, G_ATT * width), sink_ref[head0], F32)
    for g in range(1, G_ATT):
        row = jnp.where(lane_head == g, sink_ref[head0 + g], row)
    return row


def _transposed(x):
    return x.astype(F32).T.astype(BF16)


_NT = (((1,), (1,)), ((), ()))


def _rider(w, layer, steps, col_block=(0, None)):
    _, k, n = w.shape
    c, width = col_block[0], col_block[1] or n
    rows = k // steps
    assert rows * steps == k and rows % 16 == 0
    return (pl.BlockSpec((None, rows, width), lambda b: (layer, b, c)),
            pl.BlockSpec((rows, width), lambda b: (b, 0)),
            jax.ShapeDtypeStruct((k, width), BF16))


def _round_riders(ins, outs):
    for i, o in zip(ins, outs):
        o[...] = i[...].astype(BF16)


def _ctx_attn_kernel(sink_ref, q_ref, k_ref, v_ref, *refs):
    n_riders = len(refs) // 2
    o_ref = refs[n_riders]
    _round_riders(refs[:n_riders], refs[n_riders + 1:])
    seq_len = q_ref.shape[0]
    scale = DH_ATT ** -0.5
    for kv_head in range(H_KV):
        kv_cols = slice(kv_head * DH_ATT, (kv_head + 1) * DH_ATT)
        heads = range(kv_head * G_ATT, (kv_head + 1) * G_ATT)
        q4 = jnp.concatenate([q_ref[:, h * DH_ATT:(h + 1) * DH_ATT] for h in heads], axis=0)
        s_t = lax.dot_general(k_ref[:, kv_cols], q4, _NT, preferred_element_type=F32) * scale
        (p,), denom = _softmax_parts([s_t], _sink_row(sink_ref, kv_head * G_ATT, seq_len))
        o_t = jnp.dot(_transposed(v_ref[:, kv_cols]), p.astype(BF16), preferred_element_type=F32) / denom
        for g, h in enumerate(heads):
            o_ref[:, h * DH_ATT:(h + 1) * DH_ATT] = o_t[:, g * seq_len:(g + 1) * seq_len].T.astype(o_ref.dtype)


def _context_attention(proj, sink, seq_len, riders=()):
    n = proj.shape[0]
    kvw = H_KV * DH_ATT
    steps = n // seq_len
    rider_specs = [_rider(w, layer, steps, cb) for w, layer, cb in riders]
    return pl.pallas_call(
        _ctx_attn_kernel,
        grid=(steps,),
        in_specs=[
            pl.BlockSpec(memory_space=pltpu.SMEM),
            pl.BlockSpec((seq_len, D_ATT), lambda b: (b, OFF_QA // D_ATT)),
            pl.BlockSpec((seq_len, kvw), lambda b: (b, OFF_KA // kvw)),
            pl.BlockSpec((seq_len, kvw), lambda b: (b, OFF_VA // kvw)),
        ] + [s[0] for s in rider_specs],
        out_specs=[pl.BlockSpec((seq_len, D_ATT), lambda b: (b, 0))] + [s[1] for s in rider_specs],
        out_shape=[jax.ShapeDtypeStruct((n, D_ATT), BF16)] + [s[2] for s in rider_specs],
        compiler_params=_params(1),
        name="context_attention",
    )(sink, proj, proj, proj, *[w for w, _, _ in riders])


def _rope(x, cos, sin_signed):
    quarter = DH_ATT // 4
    lane = lax.broadcasted_iota(jnp.int32, x.shape, 1)
    first = (lane % (2 * quarter)) < quarter
    partner = jnp.where(first, pltpu.roll(x, DH_ATT - quarter, axis=1), pltpu.roll(x, quarter, axis=1))
    return x * cos + partner * sin_signed


def _lat_attn_kernel(sink_ref, q_ref, k_ref, v_ref, ck_ref, cv_ref, cos_ref, sin_ref, o_ref,
                     kr_ref, vt_ref, ckb_ref, cvt_ref, *, seq_len):
    kv_head = pl.program_id(1)
    qb = pl.program_id(2)
    scale = DH_ATT ** -0.5
    span_blocks = 1 + 2 * WINDOW // BLOCK
    span = span_blocks * BLOCK

    @pl.when(qb == 0)
    def _():
        kr_ref[...] = _rope(k_ref[...].astype(F32), cos_ref[...], sin_ref[...]).astype(BF16)
        for blk in range(seq_len // BLOCK):
            vt_ref[blk] = _transposed(v_ref[blk * BLOCK:(blk + 1) * BLOCK, :])
        ckb_ref[...] = ck_ref[...].astype(BF16)
        cvt_ref[...] = _transposed(cv_ref[...])

    q_rows = pl.ds(pl.multiple_of(qb * BLOCK, BLOCK), BLOCK)
    cos_q = cos_ref[q_rows, :]
    sin_q = sin_ref[q_rows, :]
    q4 = jnp.concatenate(
        [_rope(q_ref[:, g * DH_ATT:(g + 1) * DH_ATT].astype(F32), cos_q, sin_q) for g in range(G_ATT)],
        axis=0).astype(BF16)

    first_blk = jnp.clip(qb - WINDOW // BLOCK, 0, seq_len // BLOCK - span_blocks)
    start = pl.multiple_of(first_blk * BLOCK, BLOCK)
    k_win = kr_ref[pl.ds(start, span), :]
    vt_win = jnp.concatenate([vt_ref[first_blk + t] for t in range(span_blocks)], axis=1)
    k_pos = start + lax.broadcasted_iota(jnp.int32, (span, BLOCK), 0)
    q_pos = qb * BLOCK + lax.broadcasted_iota(jnp.int32, (span, BLOCK), 1)
    band = jnp.where(jnp.abs(k_pos - q_pos) <= WINDOW, 1.0, 0.0)
    band4 = jnp.concatenate([band] * G_ATT, axis=1)

    s_loc = lax.dot_general(k_win, q4, _NT, preferred_element_type=F32) * scale
    s_loc = jnp.where(band4 > 0.5, s_loc, NEG_INF)
    s_ctx = lax.dot_general(ckb_ref[...], q4, _NT, preferred_element_type=F32) * scale
    (p_loc, p_ctx), denom = _softmax_parts([s_loc, s_ctx], _sink_row(sink_ref, kv_head * G_ATT, BLOCK))
    o_t = (jnp.dot(vt_win, p_loc.astype(BF16), preferred_element_type=F32)
           + jnp.dot(cvt_ref[...], p_ctx.astype(BF16), preferred_element_type=F32)) / denom
    for g in range(G_ATT):
        o_ref[:, g * DH_ATT:(g + 1) * DH_ATT] = o_t[:, g * BLOCK:(g + 1) * BLOCK].T.astype(o_ref.dtype)


def _latent_attention(proj, sink, cache_k, cache_v, layer, seq_len, cos, sin_signed):
    n = proj.shape[0]
    batch = n // seq_len
    nq = seq_len // BLOCK
    qw = G_ATT * DH_ATT
    past = cache_k.shape[3]
    cache_spec = pl.BlockSpec((None, None, None, past, DH_ATT), lambda b, h, i: (b, layer, h, 0, 0))
    return pl.pallas_call(
        functools.partial(_lat_attn_kernel, seq_len=seq_len),
        grid=(batch, H_KV, nq),
        in_specs=[
            pl.BlockSpec(memory_space=pltpu.SMEM),
            pl.BlockSpec((BLOCK, qw), lambda b, h, i: (b * nq + i, OFF_QA // qw + h)),
            pl.BlockSpec((seq_len, DH_ATT), lambda b, h, i: (b, OFF_KA // DH_ATT + h)),
            pl.BlockSpec((seq_len, DH_ATT), lambda b, h, i: (b, OFF_VA // DH_ATT + h)),
            cache_spec, cache_spec,
            pl.BlockSpec((seq_len, DH_ATT), lambda b, h, i: (0, 0)),
            pl.BlockSpec((seq_len, DH_ATT), lambda b, h, i: (0, 0)),
        ],
        out_specs=pl.BlockSpec((BLOCK, qw), lambda b, h, i: (b * nq + i, h)),
        out_shape=jax.ShapeDtypeStruct((n, D_ATT), BF16),
        scratch_shapes=[pltpu.VMEM((seq_len, DH_ATT), BF16),
                        pltpu.VMEM((seq_len // BLOCK, DH_ATT, BLOCK), BF16),
                        pltpu.VMEM((past, DH_ATT), BF16),
                        pltpu.VMEM((DH_ATT, past), BF16)],
        compiler_params=_params(3),
        name="latent_attention",
    )(sink, proj, proj, proj, cache_k, cache_v, cos, sin_signed)


def _rope_tables(seq_len):
    rows = seq_len // GRID_W
    row = jnp.repeat(jnp.arange(rows), GRID_W)
    col = jnp.tile(jnp.arange(GRID_W), rows)
    quarter = DH_ATT // 4
    inv = ROPE_BASE ** (-jnp.arange(quarter, dtype=F32) / quarter)
    ang_r, ang_c = row[:, None] * inv, col[:, None] * inv
    cos = jnp.concatenate([jnp.cos(ang_r)] * 2 + [jnp.cos(ang_c)] * 2, axis=-1)
    sin_signed = jnp.concatenate([-jnp.sin(ang_r), jnp.sin(ang_r), -jnp.sin(ang_c), jnp.sin(ang_c)], axis=-1)
    return cos.astype(F32), sin_signed.astype(F32)


def _out_proj_body(r):
    @pl.when(pl.program_id(1) == 0)
    def _():
        r["y"][:, :D_RET] = r["y_r"][...]
        r["y"][:, D_RET:] = r["y_a"][...]

    mix = jnp.dot(r["y"][...], _tile(r, "w"), preferred_element_type=F32)
    r["out"][...] = r["x"][...] + r["gate"][...] * mix


def _out_project(y_r, y_a, x, mod4, w, layer, tiles, rounding, out=None):
    n, d = x.shape
    tm = TOKEN_TILE
    tn = OUT_COLS_ROUNDING if rounding else OUT_COLS
    w_in, w_out = _weight_tile("w", w, layer, (D_RET + D_ATT, tn), lambda i, j: (0, j), rounding)
    ins = [("y_r", pl.BlockSpec((tm, D_RET), tiles.rows(D_RET, D_RET)), y_r),
           ("y_a", pl.BlockSpec((tm, D_ATT), tiles.rows(D_ATT, D_ATT)), y_a),
           w_in,
           ("x", pl.BlockSpec((tm, tn), tiles.rows(tn, d)), x),
           ("gate", tiles.mod(mod4, layer, 2, tn, lambda j: j), mod4)]
    outs = [("out", pl.BlockSpec((tm, tn), tiles.rows(tn, d)), jax.ShapeDtypeStruct((n, d), F32))]
    if w_out:
        outs.append(w_out)
    return _call(_out_proj_body, "out_proj", (tiles.count, d // tn), ins, outs,
                 [("y", pltpu.VMEM((tm, D_RET + D_ATT), BF16))], [("out", out)])


def _ffn_body(r, *, seq_len):
    j = pl.program_id(1)
    x_ref, o_ref, h_ref = r["x"], r["out"], r["h"]
    rows = x_ref.shape[0]

    cr = CONV_ROWS
    n_chunks = rows // cr

    def step(first):
        w_gate, w_val, w_down = _tile(r, "wg"), _tile(r, "wv"), _tile(r, "wd")
        t = lax.broadcasted_iota(jnp.int32, (cr, w_gate.shape[1]), 0)
        ups = []
        for m in range(n_chunks):
            rs = slice(m * cr, (m + 1) * cr)
            if first:
                for c in range(m * cr // NORM_ROWS, (m + 1) * cr // NORM_ROWS):
                    ns = slice(c * NORM_ROWS, (c + 1) * NORM_ROWS)
                    h_ref[ns, :] = _modulated_norm(x_ref[ns, :], r["scale"], r["shift"])
            hm = h_ref[rs, :]
            ups.append(tuple(jnp.dot(hm, w, preferred_element_type=F32) for w in (w_gate, w_val)))

        for m in range(n_chunks):
            seq_pos = (t + (m * cr) % seq_len) % seq_len
            seq_start, seq_end = seq_pos == 0, seq_pos == seq_len - 1

            def conv_branch(b, cw_ref, cb_ref):
                up = ups[m][b]
                prev = pltpu.roll(up, 1, axis=0)
                nxt = pltpu.roll(up, cr - 1, axis=0)
                if (m * cr) % seq_len:
                    prev = jnp.where(t == 0, ups[m - 1][b][cr - 1:cr, :], prev)
                if ((m + 1) * cr) % seq_len:
                    nxt = jnp.where(t == cr - 1, ups[m + 1][b][0:1, :], nxt)
                prev = jnp.where(seq_start, 0.0, prev)
                nxt = jnp.where(seq_end, 0.0, nxt)
                return cb_ref[...] + prev * cw_ref[0:1, :] + up * cw_ref[1:2, :] + nxt * cw_ref[2:3, :]

            act = jax.nn.silu(conv_branch(0, r["cwg"], r["cbg"])) * conv_branch(1, r["cwv"], r["cbv"])
            down = jnp.dot(act.astype(BF16), w_down, preferred_element_type=F32)
            rs = slice(m * cr, (m + 1) * cr)
            if first:
                o_ref[rs, :] = down
            else:
                o_ref[rs, :] += down

    pl.when(j == 0)(lambda: step(True))
    pl.when(j > 0)(lambda: step(False))

    @pl.when(j == pl.num_programs(1) - 1)
    def _():
        if "gain" not in r:
            o_ref[...] = x_ref[...] + r["gate"][...] * o_ref[...]
            return
        gate, gain = r["gate"][...], r["gain"][...]

        def body(c, carry):
            rs = pl.ds(pl.multiple_of(c * NORM_ROWS, NORM_ROWS), NORM_ROWS)
            y = x_ref[rs, :] + gate * o_ref[rs, :]
            o_ref[rs, :] = y * lax.rsqrt(jnp.mean(y * y, axis=-1, keepdims=True) + EPS) * gain
            return carry

        lax.fori_loop(0, rows // NORM_ROWS, body, 0)


def _conv_ffn(x, mod4, w_gate, w_val, conv_w, conv_b, w_down, layer, tiles, seq_len, gain=None):
    n, d = x.shape
    d_ff = w_down.shape[0]
    tm, tf = TOKEN_TILE, FFN_COLS
    nf = d_ff // tf
    conv_b3 = conv_b.reshape(conv_b.shape[0], 1, 2 * d_ff)
    ins = [("x", pl.BlockSpec((tm, d), tiles.rows(d, d)), x),
           ("shift", tiles.mod(mod4, layer, 3, d), mod4),
           ("scale", tiles.mod(mod4, layer, 4, d), mod4),
           ("gate", tiles.mod(mod4, layer, 5, d), mod4),
           ("wg", pl.BlockSpec((d, tf), lambda i, j: (0, j)), w_gate),
           ("wv", pl.BlockSpec((d, tf), lambda i, j: (0, j)), w_val),
           ("cwg", pl.BlockSpec((None, CONV_W, tf), lambda i, j: (layer, 0, j)), conv_w),
           ("cwv", pl.BlockSpec((None, CONV_W, tf), lambda i, j: (layer, 0, nf + j)), conv_w),
           ("cbg", pl.BlockSpec((None, 1, tf), lambda i, j: (layer, 0, j)), conv_b3),
           ("cbv", pl.BlockSpec((None, 1, tf), lambda i, j: (layer, 0, nf + j)), conv_b3),
           ("wd", pl.BlockSpec((tf, d), lambda i, j: (j, 0)), w_down)]
    if gain is not None:
        ins.append(("gain", pl.BlockSpec((1, d), lambda i, j: (0, 0)), gain.reshape(1, d)))
    outs = [("out", pl.BlockSpec((tm, d), tiles.rows(d, d)), jax.ShapeDtypeStruct((n, d), F32))]
    return _call(_ffn_body, "conv_ffn", (tiles.count, nf), ins, outs,
                 [("h", pltpu.VMEM((tm, d), BF16))], seq_len=seq_len)["out"]


def kernel(x_prompt, x_sample, cache_k, cache_v, state_ret, c, c_ctx, w_mod, b_mod, w_in, w_out,
           ret_log_decay, ret_gn, att_sink, w_up, conv_w, conv_b, w_down, final_gain):
    batch, seq, d = x_prompt.shape
    dec_batch, dec_seq, _ = x_sample.shape
    depth = w_in.shape[0]
    assert TOKEN_TILE % seq == 0 and dec_seq == TOKEN_TILE and 1 + dec_batch <= COND_ROWS
    assert w_in.shape[-1] == D_IN and d == D_RET + D_ATT

    cond = jnp.zeros((COND_ROWS, d), F32).at[0].set(c_ctx).at[1:1 + dec_batch].set(c)
    mod = _modulation(cond, w_mod, b_mod)
    mod4 = mod.reshape(depth, COND_ROWS, 1, N_MOD * d)
    cos, sin_signed = _rope_tables(dec_seq)

    xp = x_prompt.reshape(batch * seq, d)
    xs = x_sample.reshape(dec_batch * dec_seq, d)
    n_ctx_tiles = batch * seq // TOKEN_TILE
    d_ff = w_down.shape[1]
    ctx_all = _Tiles(0, n_ctx_tiles, 0, 0)
    ctx_first = _Tiles(0, 1, 0, 0)
    ctx_rest = _Tiles(1, n_ctx_tiles - 1, 0, 0)
    lat = _Tiles(0, dec_batch, 1, 1)

    new_k = new_v = new_s = None
    for l in range(depth):
        o = _project(xp, mod4, w_in, l, ctx_first, True, cache=(seq, depth, new_k, new_v))
        w_in_l = o["w_bf"]
        o = _project(xp, mod4, w_in_l, l, ctx_rest, False, proj=o["proj"], cache=(seq, depth, o["k"], o["v"]))
        proj, new_k, new_v = o["proj"], o["k"], o["v"]
        y_r, new_s, w_down_l = _retention(proj, ret_log_decay[l], ret_gn[l], seq, l, state_out=(depth, new_s),
                                          riders=[(w_down, l, (0, None))])
        y_a, w_gate_l, w_val_l = _context_attention(proj, att_sink[l], seq,
                                                    riders=[(w_up, l, (0, d_ff)), (w_up, l, (1, d_ff))])
        o = _out_project(y_r, y_a, xp, mod4, w_out, l, ctx_first, True)
        w_out_l = o["w_bf"]
        x1 = _out_project(y_r, y_a, xp, mod4, w_out_l, l, ctx_rest, False, out=o["out"])["out"]
        gain = final_gain if l == depth - 1 else None
        xp = _conv_ffn(x1, mod4, w_gate_l, w_val_l, conv_w, conv_b, w_down_l, l, ctx_all, seq, gain=gain)
        proj = _project(xs, mod4, w_in_l, l, lat, False)["proj"]
        (y_r,) = _retention(proj, ret_log_decay[l], ret_gn[l], dec_seq, l, state=state_ret)
        y_a = _latent_attention(proj, att_sink[l], cache_k, cache_v, l, dec_seq, cos, sin_signed)
        x1 = _out_project(y_r, y_a, xs, mod4, w_out_l, l, lat, False)["out"]
        xs = _conv_ffn(x1, mod4, w_gate_l, w_val_l, conv_w, conv_b, w_down_l, l, lat, dec_seq, gain=gain)

    return (xp.reshape(batch, seq, d), xs.reshape(dec_batch, dec_seq, d), new_k, new_v, new_s)
```

```python
import functools

import jax
import jax.numpy as jnp
from jax import lax
from jax.experimental import pallas as pl
from jax.experimental.pallas import tpu as pltpu

F32 = jnp.float32
BF16 = jnp.bfloat16

GRID_W = 64
H_RET = 8
DK_RET = 128
DV_RET = 128
D_RET = H_RET * DV_RET
H_ATT = 8
H_KV = 2
G_ATT = H_ATT // H_KV
DH_ATT = 128
D_ATT = H_ATT * DH_ATT
WINDOW = 128
BLOCK = 128
CONV_W = 3
N_MOD = 6
ROPE_BASE = 10000.0
EPS = 1e-6
NEG_INF = -1e30

OFF_QR = 0
OFF_KR = OFF_QR + H_RET * DK_RET
OFF_VR = OFF_KR + H_RET * DK_RET
OFF_GR = OFF_VR + H_RET * DV_RET
OFF_QA = OFF_GR + H_RET * DV_RET
OFF_KA = OFF_QA + H_ATT * DH_ATT
OFF_VA = OFF_KA + H_KV * DH_ATT
D_IN = OFF_VA + H_KV * DH_ATT
D_KV = 2 * H_KV * DH_ATT

COND_ROWS = 8
TOKEN_TILE = 1024
NORM_ROWS = 256
CONV_ROWS = 512
RET_CHUNK = 256
PROJ_COLS, PROJ_COLS_ROUNDING = 1408, 512
OUT_COLS, OUT_COLS_ROUNDING = 1024, 512
FFN_COLS, FFN_COLS_ROUNDING = 512, 256
V7X_VMEM_LIMIT = 58 * 1024 * 1024


def _params(n_axes):
    return pltpu.CompilerParams(dimension_semantics=("arbitrary",) * n_axes,
                                vmem_limit_bytes=V7X_VMEM_LIMIT)


def _call(body, name, grid, ins, outs, scratch, carried=(), **static):
    in_names = [n for n, _, _ in ins]
    in_specs = [s for _, s, _ in ins]
    args = [a for _, _, a in ins]
    out_names = [n for n, _, _ in outs]
    aliases = {}
    for out_name, arr in carried:
        if arr is not None:
            aliases[len(args)] = out_names.index(out_name)
            in_names.append("carried_" + out_name)
            in_specs.append(pl.BlockSpec(memory_space=pl.ANY))
            args.append(arr)
    names = tuple(in_names + out_names + [n for n, _ in scratch])

    def kern(*refs):
        body(dict(zip(names, refs)), **static)

    res = pl.pallas_call(
        kern,
        grid=grid,
        in_specs=in_specs,
        out_specs=[s for _, s, _ in outs],
        out_shape=[s for _, _, s in outs],
        input_output_aliases=aliases,
        scratch_shapes=[s for _, s in scratch],
        compiler_params=_params(len(grid)),
        name=name,
    )(*args)
    return dict(zip(out_names, res))


def _mod_kernel(cond_ref, w_ref, b_ref, o_ref):
    a = jax.nn.silu(cond_ref[...]).astype(BF16)
    o_ref[...] = jnp.dot(a, w_ref[...].astype(BF16), preferred_element_type=F32) + b_ref[...]


def _modulation(cond, w_mod, b_mod):
    depth, d, n = w_mod.shape
    tn = 1024
    return pl.pallas_call(
        _mod_kernel,
        grid=(depth, n // tn),
        in_specs=[
            pl.BlockSpec((COND_ROWS, d), lambda l, j: (0, 0)),
            pl.BlockSpec((None, d, tn), lambda l, j: (l, 0, j)),
            pl.BlockSpec((None, 1, tn), lambda l, j: (l, 0, j)),
        ],
        out_specs=pl.BlockSpec((None, COND_ROWS, tn), lambda l, j: (l, 0, j)),
        out_shape=jax.ShapeDtypeStruct((depth, COND_ROWS, n), F32),
        compiler_params=_params(2),
        name="modulation",
    )(cond, w_mod, b_mod.reshape(depth, 1, n))


class _Tiles:
    def __init__(self, first, count, row0, row_step):
        self.first, self.count, self.row0, self.row_step = first, count, row0, row_step

    def rows(self, width, full_width=None):
        if width == full_width:
            return lambda i, j: (self.first + i, 0)
        return lambda i, j: (self.first + i, j)

    def mod(self, mod4, layer, chunk, width, col_block=lambda j: 0):
        per_chunk = mod4.shape[-1] // N_MOD // width
        return pl.BlockSpec(
            (None, None, 1, width),
            lambda i, j: (layer, self.row0 + self.row_step * (self.first + i), 0, chunk * per_chunk + col_block(j)))


def _weight_tile(name, w, layer, block, index, rounding):
    if not rounding:
        return (name, pl.BlockSpec(block, index), w), None
    spec = pl.BlockSpec((None,) + block, lambda i, j: (layer,) + index(i, j))
    out = (name + "_bf", pl.BlockSpec(block, index), jax.ShapeDtypeStruct(w.shape[1:], BF16))
    return (name, spec, w), out


def _tile(r, name):
    w = r[name][...]
    if name + "_bf" in r:
        w = w.astype(BF16)
        r[name + "_bf"][...] = w
    return w


def _modulated_norm(x, scale_ref, shift_ref):
    inv = lax.rsqrt(jnp.mean(x * x, axis=-1, keepdims=True) + EPS)
    return ((x * inv) * (1.0 + scale_ref[...]) + shift_ref[...]).astype(BF16)


def _proj_body(r, *, kv_step):
    j = pl.program_id(1)
    h_ref, proj_ref = r["h"], r["proj"]

    @pl.when(j == 0)
    def _():
        w = _tile(r, "w")
        for c in range(h_ref.shape[0] // NORM_ROWS):
            rs = slice(c * NORM_ROWS, (c + 1) * NORM_ROWS)
            h = _modulated_norm(r["x"][rs, :], r["scale"], r["shift"])
            h_ref[rs, :] = h
            proj_ref[rs, :] = jnp.dot(h, w, preferred_element_type=F32).astype(BF16)

    @pl.when(j > 0)
    def _():
        res = jnp.dot(h_ref[...], _tile(r, "w"), preferred_element_type=F32)
        proj_ref[...] = res.astype(BF16)
        if "k" in r:
            seqs, _, seq_len, _ = r["k"].shape
            kv0 = res.shape[1] - D_KV

            @pl.when(j == kv_step)
            def _():
                for dst, off in ((r["k"], kv0), (r["v"], kv0 + H_KV * DH_ATT)):
                    for s in range(seqs):
                        for hh in range(H_KV):
                            dst[s, hh] = res[s * seq_len:(s + 1) * seq_len,
                                             off + hh * DH_ATT:off + (hh + 1) * DH_ATT]


def _project(x, mod4, w, layer, tiles, rounding, proj=None, cache=None):
    n, d = x.shape
    d_in = w.shape[-1]
    tm = TOKEN_TILE
    tn = PROJ_COLS_ROUNDING if rounding else PROJ_COLS
    assert tn >= D_KV and d_in % tn == 0
    w_in, w_out = _weight_tile("w", w, layer, (d, tn), lambda i, j: (0, j), rounding)
    ins = [("x", pl.BlockSpec((tm, d), tiles.rows(d, d)), x),
           ("shift", tiles.mod(mod4, layer, 0, d), mod4),
           ("scale", tiles.mod(mod4, layer, 1, d), mod4),
           w_in]
    outs = [("proj", pl.BlockSpec((tm, tn), tiles.rows(tn, d_in)), jax.ShapeDtypeStruct((n, d_in), BF16))]
    carried = [("proj", proj)]
    if cache is not None:
        seq_len, depth, k_buf, v_buf = cache
        seqs = tm // seq_len
        shape = jax.ShapeDtypeStruct((n // seq_len, depth, H_KV, seq_len, DH_ATT), F32)
        spec = pl.BlockSpec((seqs, None, H_KV, seq_len, DH_ATT), lambda i, j: (tiles.first + i, layer, 0, 0, 0))
        outs += [("k", spec, shape), ("v", spec, shape)]
        carried += [("k", k_buf), ("v", v_buf)]
    if w_out:
        outs.append(w_out)
    return _call(_proj_body, "in_proj", (tiles.count, d_in // tn), ins, outs,
                 [("h", pltpu.VMEM((tm, d), BF16))], carried, kv_step=d_in // tn - 1)


def _ret_kernel(*refs, seq_len, chunk, has_state, emit_state, carried):
    it = iter(refs)
    ld_ref = next(it)
    q_ref, k_ref, v_ref, g_ref, gn_ref = (next(it) for _ in range(5))
    s0_ref = next(it) if has_state else None
    if carried:
        next(it)
    y_ref = next(it)
    snew_ref = next(it) if emit_state else None
    o_ref, intra_ref, qdf_ref, kdf_ref, qdb_ref, kdb_ref = (next(it) for _ in range(6))

    n_chunks = seq_len // chunk
    k_scale = DK_RET ** -0.5

    @pl.when(pl.program_id(0) == 0)
    def _():
        row = lax.broadcasted_iota(jnp.int32, (chunk, chunk), 0).astype(F32)
        col = lax.broadcasted_iota(jnp.int32, (chunk, chunk), 1).astype(F32)
        diff = row - col
        pos = lax.broadcasted_iota(jnp.int32, (chunk, DK_RET), 0).astype(F32)
        for h in range(H_RET):
            lg_f = ld_ref[0, h]
            lg_b = ld_ref[1, h]
            intra_ref[h] = (jnp.where(diff >= 0, jnp.exp(lg_f * jnp.maximum(diff, 0.0)), 0.0)
                            + jnp.where(diff <= 0, jnp.exp(lg_b * jnp.maximum(-diff, 0.0)), 0.0)) * k_scale
            qdf_ref[h] = jnp.exp(lg_f * (pos + 1.0))
            kdf_ref[h] = jnp.exp(lg_f * (chunk - 1.0 - pos)) * k_scale
            qdb_ref[h] = jnp.exp(lg_b * (chunk - pos))
            kdb_ref[h] = jnp.exp(lg_b * pos) * k_scale

    def rows(n):
        return pl.ds(n * chunk, chunk)

    def kv_update(k_b, dec, v_b):
        kd = (k_b.astype(F32) * dec).T.astype(BF16)
        return jnp.dot(kd, v_b, preferred_element_type=F32)

    for h in range(H_RET):
        cols = slice(h * DK_RET, (h + 1) * DK_RET)
        c_dec_f = jnp.exp(jnp.full((DK_RET, DV_RET), ld_ref[0, h] * chunk, F32))
        c_dec_b = jnp.exp(jnp.full((DK_RET, DV_RET), ld_ref[1, h] * chunk, F32))

        s_f = s0_ref[0, h] if has_state else jnp.zeros((DK_RET, DV_RET), F32)
        for n in range(n_chunks):
            q_b = q_ref[rows(n), cols]
            k_b = k_ref[rows(n), cols]
            v_b = v_ref[rows(n), cols]
            sc = lax.dot_general(q_b, k_b, (((1,), (1,)), ((), ())),
                                 preferred_element_type=F32) * intra_ref[h]
            o = jnp.dot(sc.astype(BF16), v_b, preferred_element_type=F32)
            if has_state or n > 0:
                o = o + jnp.dot(q_b, s_f.astype(BF16), preferred_element_type=F32) * qdf_ref[h]
            o_ref[rows(n), :] = o
            s_f = c_dec_f * s_f + kv_update(k_b, kdf_ref[h], v_b)

        s_b = s0_ref[1, h] if has_state else jnp.zeros((DK_RET, DV_RET), F32)
        for n in reversed(range(n_chunks)):
            q_b = q_ref[rows(n), cols]
            k_b = k_ref[rows(n), cols]
            v_b = v_ref[rows(n), cols]
            if has_state or n < n_chunks - 1:
                o_ref[rows(n), :] += jnp.dot(q_b, s_b.astype(BF16), preferred_element_type=F32) * qdb_ref[h]
            s_b = c_dec_b * s_b + kv_update(k_b, kdb_ref[h], v_b)

        if emit_state:
            snew_ref[0, h] = s_f
            snew_ref[1, h] = s_b

        gn = gn_ref[:, cols]
        for n in range(n_chunks):
            o = o_ref[rows(n), :]
            mu = jnp.mean(o, axis=-1, keepdims=True)
            cen = o - mu
            var = jnp.mean(cen * cen, axis=-1, keepdims=True)
            on = cen * lax.rsqrt(var + EPS) * gn
            y_ref[rows(n), cols] = (jax.nn.silu(g_ref[rows(n), cols].astype(F32)) * on).astype(y_ref.dtype)


def _retention(proj, log_decay, gn, seq_len, layer, state=None, state_out=None):
    n = proj.shape[0]
    batch = n // seq_len
    has_state = state is not None
    chunk = min(RET_CHUNK, seq_len)
    blk = lambda off: pl.BlockSpec((seq_len, D_RET), lambda b: (b, off // D_RET))
    state_spec = pl.BlockSpec((None, None, 2, H_RET, DK_RET, DV_RET), lambda b: (b, layer, 0, 0, 0, 0))
    in_specs = [
        pl.BlockSpec(memory_space=pltpu.SMEM),
        blk(OFF_QR), blk(OFF_KR), blk(OFF_VR), blk(OFF_GR),
        pl.BlockSpec((1, D_RET), lambda b: (0, 0)),
    ]
    args = [log_decay, proj, proj, proj, proj, gn.reshape(1, D_RET)]
    if has_state:
        in_specs.append(state_spec)
        args.append(state)
    out_specs = [pl.BlockSpec((seq_len, D_RET), lambda b: (b, 0))]
    out_shape = [jax.ShapeDtypeStruct((n, D_RET), BF16)]
    aliases = {}
    if state_out is not None:
        depth, buf = state_out
        out_specs.append(state_spec)
        out_shape.append(jax.ShapeDtypeStruct((batch, depth, 2, H_RET, DK_RET, DV_RET), F32))
        if buf is not None:
            aliases[len(args)] = 1
            in_specs.append(pl.BlockSpec(memory_space=pl.ANY))
            args.append(buf)
    kern = functools.partial(_ret_kernel, seq_len=seq_len, chunk=chunk, has_state=has_state,
                             emit_state=state_out is not None, carried=bool(aliases))
    return pl.pallas_call(
        kern,
        grid=(batch,),
        in_specs=in_specs,
        out_specs=out_specs,
        out_shape=out_shape,
        input_output_aliases=aliases,
        scratch_shapes=[
            pltpu.VMEM((seq_len, DV_RET), F32),
            pltpu.VMEM((H_RET, chunk, chunk), F32),
            pltpu.VMEM((H_RET, chunk, DK_RET), F32),
            pltpu.VMEM((H_RET, chunk, DK_RET), F32),
            pltpu.VMEM((H_RET, chunk, DK_RET), F32),
            pltpu.VMEM((H_RET, chunk, DK_RET), F32),
        ],
        compiler_params=_params(1),
        name="retention",
    )(*args)


def _softmax_parts(scores, sink):
    m = jnp.maximum(sink, functools.reduce(
        jnp.maximum, [jnp.max(s, axis=-1, keepdims=True) for s in scores]))
    ps = [jnp.exp(s - m) for s in scores]
    denom = jnp.exp(sink - m) + functools.reduce(
        lambda a, b: a + b, [jnp.sum(p, axis=-1, keepdims=True) for p in ps])
    return ps, denom


def _ctx_attn_kernel(sink_ref, q_ref, k_ref, v_ref, o_ref):
    scale = DH_ATT ** -0.5
    for head in range(H_ATT):
        cols = slice(head * DH_ATT, (head + 1) * DH_ATT)
        kv_cols = slice(head // G_ATT * DH_ATT, (head // G_ATT + 1) * DH_ATT)
        s = lax.dot_general(q_ref[:, cols], k_ref[:, kv_cols], (((1,), (1,)), ((), ())),
                            preferred_element_type=F32) * scale
        (p,), denom = _softmax_parts([s], sink_ref[head])
        o = jnp.dot(p.astype(BF16), v_ref[:, kv_cols], preferred_element_type=F32)
        o_ref[:, cols] = (o / denom).astype(o_ref.dtype)


def _context_attention(proj, sink, seq_len):
    n = proj.shape[0]
    kvw = H_KV * DH_ATT
    return pl.pallas_call(
        _ctx_attn_kernel,
        grid=(n // seq_len,),
        in_specs=[
            pl.BlockSpec(memory_space=pltpu.SMEM),
            pl.BlockSpec((seq_len, D_ATT), lambda b: (b, OFF_QA // D_ATT)),
            pl.BlockSpec((seq_len, kvw), lambda b: (b, OFF_KA // kvw)),
            pl.BlockSpec((seq_len, kvw), lambda b: (b, OFF_VA // kvw)),
        ],
        out_specs=pl.BlockSpec((seq_len, D_ATT), lambda b: (b, 0)),
        out_shape=jax.ShapeDtypeStruct((n, D_ATT), BF16),
        compiler_params=_params(1),
        name="context_attention",
    )(sink, proj, proj, proj)


def _rope(x, cos, sin_signed):
    quarter = DH_ATT // 4
    lane = lax.broadcasted_iota(jnp.int32, x.shape, 1)
    first = (lane % (2 * quarter)) < quarter
    partner = jnp.where(first, pltpu.roll(x, DH_ATT - quarter, axis=1), pltpu.roll(x, quarter, axis=1))
    return x * cos + partner * sin_signed


def _lat_attn_kernel(sink_ref, q_ref, k_ref, v_ref, ck_ref, cv_ref, cos_ref, sin_ref, o_ref,
                     kr_ref, *, seq_len):
    qb = pl.program_id(1)
    scale = DH_ATT ** -0.5
    span = BLOCK + 2 * WINDOW

    @pl.when(qb == 0)
    def _():
        for kv_head in range(H_KV):
            kv_cols = slice(kv_head * DH_ATT, (kv_head + 1) * DH_ATT)
            kr_ref[:, kv_cols] = _rope(k_ref[:, kv_cols].astype(F32), cos_ref[...], sin_ref[...]).astype(BF16)

    q_rows = pl.ds(pl.multiple_of(qb * BLOCK, BLOCK), BLOCK)
    cos_q = cos_ref[q_rows, :]
    sin_q = sin_ref[q_rows, :]
    start = pl.multiple_of(jnp.clip(qb * BLOCK - WINDOW, 0, seq_len - span), BLOCK)
    q_pos = qb * BLOCK + lax.broadcasted_iota(jnp.int32, (BLOCK, span), 0)
    k_pos = start + lax.broadcasted_iota(jnp.int32, (BLOCK, span), 1)
    band = jnp.abs(k_pos - q_pos) <= WINDOW
    row_head = lax.broadcasted_iota(jnp.int32, (G_ATT * BLOCK, 1), 0) // BLOCK
    nt = (((1,), (1,)), ((), ()))

    for kv_head in range(H_KV):
        kv_cols = slice(kv_head * DH_ATT, (kv_head + 1) * DH_ATT)
        heads = range(kv_head * G_ATT, (kv_head + 1) * G_ATT)
        q4 = jnp.concatenate(
            [_rope(q_ref[:, h * DH_ATT:(h + 1) * DH_ATT].astype(F32), cos_q, sin_q) for h in heads],
            axis=0).astype(BF16)
        k_win = kr_ref[pl.ds(start, span), kv_cols]
        v_win = v_ref[pl.ds(start, span), kv_cols]
        s_loc = lax.dot_general(q4, k_win, nt, preferred_element_type=F32) * scale
        s_loc = jnp.where(band[None], s_loc.reshape(G_ATT, BLOCK, span), NEG_INF).reshape(G_ATT * BLOCK, span)
        s_ctx = lax.dot_general(q4, ck_ref[kv_head].astype(BF16), nt, preferred_element_type=F32) * scale
        sink4 = jnp.full((G_ATT * BLOCK, 1), sink_ref[kv_head * G_ATT], F32)
        for g in range(1, G_ATT):
            sink4 = jnp.where(row_head == g, sink_ref[kv_head * G_ATT + g], sink4)
        (p_loc, p_ctx), denom = _softmax_parts([s_loc, s_ctx], sink4)
        o4 = (jnp.dot(p_loc.astype(BF16), v_win, preferred_element_type=F32)
              + jnp.dot(p_ctx.astype(BF16), cv_ref[kv_head].astype(BF16), preferred_element_type=F32)) / denom
        for g, h in enumerate(heads):
            o_ref[:, h * DH_ATT:(h + 1) * DH_ATT] = o4[g * BLOCK:(g + 1) * BLOCK, :].astype(o_ref.dtype)


def _latent_attention(proj, sink, cache_k, cache_v, layer, seq_len, cos, sin_signed):
    n = proj.shape[0]
    batch = n // seq_len
    nq = seq_len // BLOCK
    kvw = H_KV * DH_ATT
    past = cache_k.shape[3]
    cache_spec = pl.BlockSpec((None, None, H_KV, past, DH_ATT), lambda b, i: (b, layer, 0, 0, 0))
    return pl.pallas_call(
        functools.partial(_lat_attn_kernel, seq_len=seq_len),
        grid=(batch, nq),
        in_specs=[
            pl.BlockSpec(memory_space=pltpu.SMEM),
            pl.BlockSpec((BLOCK, D_ATT), lambda b, i: (b * nq + i, OFF_QA // D_ATT)),
            pl.BlockSpec((seq_len, kvw), lambda b, i: (b, OFF_KA // kvw)),
            pl.BlockSpec((seq_len, kvw), lambda b, i: (b, OFF_VA // kvw)),
            cache_spec, cache_spec,
            pl.BlockSpec((seq_len, DH_ATT), lambda b, i: (0, 0)),
            pl.BlockSpec((seq_len, DH_ATT), lambda b, i: (0, 0)),
        ],
        out_specs=pl.BlockSpec((BLOCK, D_ATT), lambda b, i: (b * nq + i, 0)),
        out_shape=jax.ShapeDtypeStruct((n, D_ATT), BF16),
        scratch_shapes=[pltpu.VMEM((seq_len, kvw), BF16)],
        compiler_params=_params(2),
        name="latent_attention",
    )(sink, proj, proj, proj, cache_k, cache_v, cos, sin_signed)


def _rope_tables(seq_len):
    rows = seq_len // GRID_W
    row = jnp.repeat(jnp.arange(rows), GRID_W)
    col = jnp.tile(jnp.arange(GRID_W), rows)
    quarter = DH_ATT // 4
    inv = ROPE_BASE ** (-jnp.arange(quarter, dtype=F32) / quarter)
    ang_r, ang_c = row[:, None] * inv, col[:, None] * inv
    cos = jnp.concatenate([jnp.cos(ang_r)] * 2 + [jnp.cos(ang_c)] * 2, axis=-1)
    sin_signed = jnp.concatenate([-jnp.sin(ang_r), jnp.sin(ang_r), -jnp.sin(ang_c), jnp.sin(ang_c)], axis=-1)
    return cos.astype(F32), sin_signed.astype(F32)


def _out_proj_body(r):
    @pl.when(pl.program_id(1) == 0)
    def _():
        r["y"][:, :D_RET] = r["y_r"][...]
        r["y"][:, D_RET:] = r["y_a"][...]

    mix = jnp.dot(r["y"][...], _tile(r, "w"), preferred_element_type=F32)
    r["out"][...] = r["x"][...] + r["gate"][...] * mix


def _out_project(y_r, y_a, x, mod4, w, layer, tiles, rounding, out=None):
    n, d = x.shape
    tm = TOKEN_TILE
    tn = OUT_COLS_ROUNDING if rounding else OUT_COLS
    w_in, w_out = _weight_tile("w", w, layer, (D_RET + D_ATT, tn), lambda i, j: (0, j), rounding)
    ins = [("y_r", pl.BlockSpec((tm, D_RET), tiles.rows(D_RET, D_RET)), y_r),
           ("y_a", pl.BlockSpec((tm, D_ATT), tiles.rows(D_ATT, D_ATT)), y_a),
           w_in,
           ("x", pl.BlockSpec((tm, tn), tiles.rows(tn, d)), x),
           ("gate", tiles.mod(mod4, layer, 2, tn, lambda j: j), mod4)]
    outs = [("out", pl.BlockSpec((tm, tn), tiles.rows(tn, d)), jax.ShapeDtypeStruct((n, d), F32))]
    if w_out:
        outs.append(w_out)
    return _call(_out_proj_body, "out_proj", (tiles.count, d // tn), ins, outs,
                 [("y", pltpu.VMEM((tm, D_RET + D_ATT), BF16))], [("out", out)])


def _ffn_body(r, *, seq_len):
    j = pl.program_id(1)
    x_ref, o_ref, h_ref = r["x"], r["out"], r["h"]
    rows = x_ref.shape[0]

    cr = CONV_ROWS
    n_chunks = rows // cr

    def step(first):
        w_gate, w_val, w_down = _tile(r, "wg"), _tile(r, "wv"), _tile(r, "wd")
        t = lax.broadcasted_iota(jnp.int32, (cr, w_gate.shape[1]), 0)
        ups = []
        for m in range(n_chunks):
            rs = slice(m * cr, (m + 1) * cr)
            if first:
                for c in range(m * cr // NORM_ROWS, (m + 1) * cr // NORM_ROWS):
                    ns = slice(c * NORM_ROWS, (c + 1) * NORM_ROWS)
                    h_ref[ns, :] = _modulated_norm(x_ref[ns, :], r["scale"], r["shift"])
            hm = h_ref[rs, :]
            ups.append(tuple(jnp.dot(hm, w, preferred_element_type=F32) for w in (w_gate, w_val)))

        for m in range(n_chunks):
            seq_pos = (t + (m * cr) % seq_len) % seq_len
            seq_start, seq_end = seq_pos == 0, seq_pos == seq_len - 1

            def conv_branch(b, cw_ref, cb_ref):
                up = ups[m][b]
                prev = pltpu.roll(up, 1, axis=0)
                nxt = pltpu.roll(up, cr - 1, axis=0)
                if (m * cr) % seq_len:
                    prev = jnp.where(t == 0, ups[m - 1][b][cr - 1:cr, :], prev)
                if ((m + 1) * cr) % seq_len:
                    nxt = jnp.where(t == cr - 1, ups[m + 1][b][0:1, :], nxt)
                prev = jnp.where(seq_start, 0.0, prev)
                nxt = jnp.where(seq_end, 0.0, nxt)
                return cb_ref[...] + prev * cw_ref[0:1, :] + up * cw_ref[1:2, :] + nxt * cw_ref[2:3, :]

            act = jax.nn.silu(conv_branch(0, r["cwg"], r["cbg"])) * conv_branch(1, r["cwv"], r["cbv"])
            down = jnp.dot(act.astype(BF16), w_down, preferred_element_type=F32)
            rs = slice(m * cr, (m + 1) * cr)
            if first:
                o_ref[rs, :] = down
            else:
                o_ref[rs, :] += down

    pl.when(j == 0)(lambda: step(True))
    pl.when(j > 0)(lambda: step(False))

    @pl.when(j == pl.num_programs(1) - 1)
    def _():
        if "gain" not in r:
            o_ref[...] = x_ref[...] + r["gate"][...] * o_ref[...]
            return
        gate, gain = r["gate"][...], r["gain"][...]

        def body(c, carry):
            rs = pl.ds(pl.multiple_of(c * NORM_ROWS, NORM_ROWS), NORM_ROWS)
            y = x_ref[rs, :] + gate * o_ref[rs, :]
            o_ref[rs, :] = y * lax.rsqrt(jnp.mean(y * y, axis=-1, keepdims=True) + EPS) * gain
            return carry

        lax.fori_loop(0, rows // NORM_ROWS, body, 0)


def _conv_ffn(x, mod4, w_gate, w_val, conv_w, conv_b, w_down, layer, tiles, rounding, seq_len,
              gain=None, out=None):
    n, d = x.shape
    d_ff = w_down.shape[-2]
    tm = TOKEN_TILE
    tf = FFN_COLS_ROUNDING if rounding else FFN_COLS
    nf = d_ff // tf
    val0 = nf if rounding else 0
    wg_in, wg_out = _weight_tile("wg", w_gate, layer, (d, tf), lambda i, j: (0, j), rounding)
    wv_in, wv_out = _weight_tile("wv", w_val, layer, (d, tf), lambda i, j: (0, val0 + j), rounding)
    wd_in, wd_out = _weight_tile("wd", w_down, layer, (tf, d), lambda i, j: (j, 0), rounding)
    if rounding:
        half = jax.ShapeDtypeStruct((d, d_ff), BF16)
        wg_out = (wg_out[0], wg_out[1], half)
        wv_out = (wv_out[0], pl.BlockSpec((d, tf), lambda i, j: (0, j)), half)
    conv_b3 = conv_b.reshape(conv_b.shape[0], 1, 2 * d_ff)
    x_spec = pl.BlockSpec((tm, d), tiles.rows(d, d), **({"pipeline_mode": pl.Buffered(1)} if rounding else {}))
    ins = [("x", x_spec, x),
           ("shift", tiles.mod(mod4, layer, 3, d), mod4),
           ("scale", tiles.mod(mod4, layer, 4, d), mod4),
           ("gate", tiles.mod(mod4, layer, 5, d), mod4),
           wg_in, wv_in,
           ("cwg", pl.BlockSpec((None, CONV_W, tf), lambda i, j: (layer, 0, j)), conv_w),
           ("cwv", pl.BlockSpec((None, CONV_W, tf), lambda i, j: (layer, 0, nf + j)), conv_w),
           ("cbg", pl.BlockSpec((None, 1, tf), lambda i, j: (layer, 0, j)), conv_b3),
           ("cbv", pl.BlockSpec((None, 1, tf), lambda i, j: (layer, 0, nf + j)), conv_b3),
           wd_in]
    if gain is not None:
        ins.append(("gain", pl.BlockSpec((1, d), lambda i, j: (0, 0)), gain.reshape(1, d)))
    outs = [("out", pl.BlockSpec((tm, d), tiles.rows(d, d)), jax.ShapeDtypeStruct((n, d), F32))]
    outs += [o for o in (wg_out, wv_out, wd_out) if o]
    return _call(_ffn_body, "conv_ffn", (tiles.count, nf), ins, outs,
                 [("h", pltpu.VMEM((tm, d), BF16))], [("out", out)], seq_len=seq_len)


def kernel(x_prompt, x_sample, cache_k, cache_v, state_ret, c, c_ctx, w_mod, b_mod, w_in, w_out,
           ret_log_decay, ret_gn, att_sink, w_up, conv_w, conv_b, w_down, final_gain):
    batch, seq, d = x_prompt.shape
    dec_batch, dec_seq, _ = x_sample.shape
    depth = w_in.shape[0]
    assert TOKEN_TILE % seq == 0 and dec_seq == TOKEN_TILE and 1 + dec_batch <= COND_ROWS
    assert w_in.shape[-1] == D_IN and d == D_RET + D_ATT

    cond = jnp.zeros((COND_ROWS, d), F32).at[0].set(c_ctx).at[1:1 + dec_batch].set(c)
    mod = _modulation(cond, w_mod, b_mod)
    mod4 = mod.reshape(depth, COND_ROWS, 1, N_MOD * d)
    cos, sin_signed = _rope_tables(dec_seq)

    xp = x_prompt.reshape(batch * seq, d)
    xs = x_sample.reshape(dec_batch * dec_seq, d)
    n_ctx_tiles = batch * seq // TOKEN_TILE
    ctx_first = _Tiles(0, 1, 0, 0)
    ctx_rest = _Tiles(1, n_ctx_tiles - 1, 0, 0)
    lat = _Tiles(0, dec_batch, 1, 1)

    new_k = new_v = new_s = None
    for l in range(depth):
        o = _project(xp, mod4, w_in, l, ctx_first, True, cache=(seq, depth, new_k, new_v))
        w_in_l = o["w_bf"]
        o = _project(xp, mod4, w_in_l, l, ctx_rest, False, proj=o["proj"], cache=(seq, depth, o["k"], o["v"]))
        proj, new_k, new_v = o["proj"], o["k"], o["v"]
        y_r, new_s = _retention(proj, ret_log_decay[l], ret_gn[l], seq, l, state_out=(depth, new_s))
        y_a = _context_attention(proj, att_sink[l], seq)
        o = _out_project(y_r, y_a, xp, mod4, w_out, l, ctx_first, True)
        w_out_l = o["w_bf"]
        x1 = _out_project(y_r, y_a, xp, mod4, w_out_l, l, ctx_rest, False, out=o["out"])["out"]
        gain = final_gain if l == depth - 1 else None
        o = _conv_ffn(x1, mod4, w_up, w_up, conv_w, conv_b, w_down, l, ctx_first, True, seq, gain=gain)
        w_gate_l, w_val_l, w_down_l = o["wg_bf"], o["wv_bf"], o["wd_bf"]
        xp = _conv_ffn(x1, mod4, w_gate_l, w_val_l, conv_w, conv_b, w_down_l, l, ctx_rest, False, seq,
                       gain=gain, out=o["out"])["out"]
        proj = _project(xs, mod4, w_in_l, l, lat, False)["proj"]
        (y_r,) = _retention(proj, ret_log_decay[l], ret_gn[l], dec_seq, l, state=state_ret)
        y_a = _latent_attention(proj, att_sink[l], cache_k, cache_v, l, dec_seq, cos, sin_signed)
        x1 = _out_project(y_r, y_a, xs, mod4, w_out_l, l, lat, False)["out"]
        xs = _conv_ffn(x1, mod4, w_gate_l, w_val_l, conv_w, conv_b, w_down_l, l, lat, False, dec_seq,
                       gain=gain)["out"]

    return (xp.reshape(batch, seq, d), xs.reshape(dec_batch, dec_seq, d), new_k, new_v, new_s)
```

```python
import functools

import jax
import jax.numpy as jnp
from jax import lax
from jax.experimental import pallas as pl
from jax.experimental.pallas import tpu as pltpu

F32 = jnp.float32
BF16 = jnp.bfloat16

GRID_W = 64
H_RET = 8
DK_RET = 128
DV_RET = 128
D_RET = H_RET * DV_RET
H_ATT = 8
H_KV = 2
G_ATT = H_ATT // H_KV
DH_ATT = 128
D_ATT = H_ATT * DH_ATT
WINDOW = 128
BLOCK = 128
CONV_W = 3
N_MOD = 6
ROPE_BASE = 10000.0
EPS = 1e-6
NEG_INF = -1e30

OFF_QR = 0
OFF_KR = OFF_QR + H_RET * DK_RET
OFF_VR = OFF_KR + H_RET * DK_RET
OFF_GR = OFF_VR + H_RET * DV_RET
OFF_QA = OFF_GR + H_RET * DV_RET
OFF_KA = OFF_QA + H_ATT * DH_ATT
OFF_VA = OFF_KA + H_KV * DH_ATT
D_IN = OFF_VA + H_KV * DH_ATT
D_KV = 2 * H_KV * DH_ATT

COND_ROWS = 8
TOKEN_TILE = 1024
NORM_ROWS = 256
CONV_ROWS = 512
RET_CHUNK = 256
CTX_SEQS = 2
LAT_QBLOCKS = 2
PROJ_COLS, PROJ_COLS_ROUNDING = 1408, 512
OUT_COLS, OUT_COLS_ROUNDING = 1024, 512
FFN_COLS, FFN_COLS_ROUNDING = 512, 256
V7X_VMEM_LIMIT = 58 * 1024 * 1024


def _params(n_axes):
    return pltpu.CompilerParams(dimension_semantics=("arbitrary",) * n_axes,
                                vmem_limit_bytes=V7X_VMEM_LIMIT)


def _call(body, name, grid, ins, outs, scratch, carried=(), **static):
    in_names = [n for n, _, _ in ins]
    in_specs = [s for _, s, _ in ins]
    args = [a for _, _, a in ins]
    out_names = [n for n, _, _ in outs]
    aliases = {}
    for out_name, arr in carried:
        if arr is not None:
            aliases[len(args)] = out_names.index(out_name)
            in_names.append("carried_" + out_name)
            in_specs.append(pl.BlockSpec(memory_space=pl.ANY))
            args.append(arr)
    names = tuple(in_names + out_names + [n for n, _ in scratch])

    def kern(*refs):
        body(dict(zip(names, refs)), **static)

    res = pl.pallas_call(
        kern,
        grid=grid,
        in_specs=in_specs,
        out_specs=[s for _, s, _ in outs],
        out_shape=[s for _, _, s in outs],
        input_output_aliases=aliases,
        scratch_shapes=[s for _, s in scratch],
        compiler_params=_params(len(grid)),
        name=name,
    )(*args)
    return dict(zip(out_names, res))


def _mod_kernel(cond_ref, w_ref, b_ref, o_ref):
    a = jax.nn.silu(cond_ref[...]).astype(BF16)
    o_ref[...] = jnp.dot(a, w_ref[...].astype(BF16), preferred_element_type=F32) + b_ref[...]


def _modulation(cond, w_mod, b_mod):
    depth, d, n = w_mod.shape
    tn = 1024
    return pl.pallas_call(
        _mod_kernel,
        grid=(depth, n // tn),
        in_specs=[
            pl.BlockSpec((COND_ROWS, d), lambda l, j: (0, 0)),
            pl.BlockSpec((None, d, tn), lambda l, j: (l, 0, j)),
            pl.BlockSpec((None, 1, tn), lambda l, j: (l, 0, j)),
        ],
        out_specs=pl.BlockSpec((None, COND_ROWS, tn), lambda l, j: (l, 0, j)),
        out_shape=jax.ShapeDtypeStruct((depth, COND_ROWS, n), F32),
        compiler_params=_params(2),
        name="modulation",
    )(cond, w_mod, b_mod.reshape(depth, 1, n))


class _Tiles:
    def __init__(self, first, count, row0, row_step):
        self.first, self.count, self.row0, self.row_step = first, count, row0, row_step

    def rows(self, width, full_width=None):
        if width == full_width:
            return lambda i, j: (self.first + i, 0)
        return lambda i, j: (self.first + i, j)

    def mod(self, mod4, layer, chunk, width, col_block=lambda j: 0):
        per_chunk = mod4.shape[-1] // N_MOD // width
        return pl.BlockSpec(
            (None, None, 1, width),
            lambda i, j: (layer, self.row0 + self.row_step * (self.first + i), 0, chunk * per_chunk + col_block(j)))


def _weight_tile(name, w, layer, block, index, rounding):
    if not rounding:
        return (name, pl.BlockSpec(block, index), w), None
    spec = pl.BlockSpec((None,) + block, lambda i, j: (layer,) + index(i, j))
    out = (name + "_bf", pl.BlockSpec(block, index), jax.ShapeDtypeStruct(w.shape[1:], BF16))
    return (name, spec, w), out


def _tile(r, name):
    w = r[name][...]
    if name + "_bf" in r:
        w = w.astype(BF16)
        r[name + "_bf"][...] = w
    return w


def _modulated_norm(x, scale_ref, shift_ref):
    inv = lax.rsqrt(jnp.mean(x * x, axis=-1, keepdims=True) + EPS)
    return ((x * inv) * (1.0 + scale_ref[...]) + shift_ref[...]).astype(BF16)


def _proj_body(r, *, kv_step):
    j = pl.program_id(1)
    h_ref, proj_ref = r["h"], r["proj"]

    @pl.when(j == 0)
    def _():
        w = _tile(r, "w")
        for c in range(h_ref.shape[0] // NORM_ROWS):
            rs = slice(c * NORM_ROWS, (c + 1) * NORM_ROWS)
            h = _modulated_norm(r["x"][rs, :], r["scale"], r["shift"])
            h_ref[rs, :] = h
            proj_ref[rs, :] = jnp.dot(h, w, preferred_element_type=F32).astype(BF16)

    @pl.when(j > 0)
    def _():
        res = jnp.dot(h_ref[...], _tile(r, "w"), preferred_element_type=F32)
        proj_ref[...] = res.astype(BF16)
        if "k" in r:
            seqs, _, seq_len, _ = r["k"].shape
            kv0 = res.shape[1] - D_KV

            @pl.when(j == kv_step)
            def _():
                for dst, off in ((r["k"], kv0), (r["v"], kv0 + H_KV * DH_ATT)):
                    for s in range(seqs):
                        for hh in range(H_KV):
                            dst[s, hh] = res[s * seq_len:(s + 1) * seq_len,
                                             off + hh * DH_ATT:off + (hh + 1) * DH_ATT]


def _project(x, mod4, w, layer, tiles, rounding, proj=None, cache=None):
    n, d = x.shape
    d_in = w.shape[-1]
    tm = TOKEN_TILE
    tn = PROJ_COLS_ROUNDING if rounding else PROJ_COLS
    assert tn >= D_KV and d_in % tn == 0
    w_in, w_out = _weight_tile("w", w, layer, (d, tn), lambda i, j: (0, j), rounding)
    ins = [("x", pl.BlockSpec((tm, d), tiles.rows(d, d)), x),
           ("shift", tiles.mod(mod4, layer, 0, d), mod4),
           ("scale", tiles.mod(mod4, layer, 1, d), mod4),
           w_in]
    outs = [("proj", pl.BlockSpec((tm, tn), tiles.rows(tn, d_in)), jax.ShapeDtypeStruct((n, d_in), BF16))]
    carried = [("proj", proj)]
    if cache is not None:
        seq_len, depth, k_buf, v_buf = cache
        seqs = tm // seq_len
        shape = jax.ShapeDtypeStruct((n // seq_len, depth, H_KV, seq_len, DH_ATT), F32)
        spec = pl.BlockSpec((seqs, None, H_KV, seq_len, DH_ATT), lambda i, j: (tiles.first + i, layer, 0, 0, 0))
        outs += [("k", spec, shape), ("v", spec, shape)]
        carried += [("k", k_buf), ("v", v_buf)]
    if w_out:
        outs.append(w_out)
    return _call(_proj_body, "in_proj", (tiles.count, d_in // tn), ins, outs,
                 [("h", pltpu.VMEM((tm, d), BF16))], carried, kv_step=d_in // tn - 1)


def _ret_kernel(*refs, seq_len, chunk, has_state, emit_state, carried):
    it = iter(refs)
    ld_ref = next(it)
    q_ref, k_ref, v_ref, g_ref, gn_ref = (next(it) for _ in range(5))
    s0_ref = next(it) if has_state else None
    if carried:
        next(it)
    y_ref = next(it)
    snew_ref = next(it) if emit_state else None
    o_ref, intra_ref, qdf_ref, kdf_ref, qdb_ref, kdb_ref = (next(it) for _ in range(6))

    n_chunks = seq_len // chunk
    k_scale = DK_RET ** -0.5

    @pl.when(pl.program_id(0) == 0)
    def _():
        row = lax.broadcasted_iota(jnp.int32, (chunk, chunk), 0).astype(F32)
        col = lax.broadcasted_iota(jnp.int32, (chunk, chunk), 1).astype(F32)
        diff = row - col
        pos = lax.broadcasted_iota(jnp.int32, (chunk, DK_RET), 0).astype(F32)
        for h in range(H_RET):
            lg_f = ld_ref[0, h]
            lg_b = ld_ref[1, h]
            intra_ref[h] = (jnp.where(diff >= 0, jnp.exp(lg_f * jnp.maximum(diff, 0.0)), 0.0)
                            + jnp.where(diff <= 0, jnp.exp(lg_b * jnp.maximum(-diff, 0.0)), 0.0)) * k_scale
            qdf_ref[h] = jnp.exp(lg_f * (pos + 1.0))
            kdf_ref[h] = jnp.exp(lg_f * (chunk - 1.0 - pos)) * k_scale
            qdb_ref[h] = jnp.exp(lg_b * (chunk - pos))
            kdb_ref[h] = jnp.exp(lg_b * pos) * k_scale

    def rows(n):
        return pl.ds(n * chunk, chunk)

    def kv_update(k_b, dec, v_b):
        kd = (k_b.astype(F32) * dec).T.astype(BF16)
        return jnp.dot(kd, v_b, preferred_element_type=F32)

    for h in range(H_RET):
        cols = slice(h * DK_RET, (h + 1) * DK_RET)
        c_dec_f = jnp.exp(jnp.full((DK_RET, DV_RET), ld_ref[0, h] * chunk, F32))
        c_dec_b = jnp.exp(jnp.full((DK_RET, DV_RET), ld_ref[1, h] * chunk, F32))

        s_f = s0_ref[0, h] if has_state else jnp.zeros((DK_RET, DV_RET), F32)
        for n in range(n_chunks):
            q_b = q_ref[rows(n), cols]
            k_b = k_ref[rows(n), cols]
            v_b = v_ref[rows(n), cols]
            sc = lax.dot_general(q_b, k_b, (((1,), (1,)), ((), ())),
                                 preferred_element_type=F32) * intra_ref[h]
            o = jnp.dot(sc.astype(BF16), v_b, preferred_element_type=F32)
            if has_state or n > 0:
                o = o + jnp.dot(q_b, s_f.astype(BF16), preferred_element_type=F32) * qdf_ref[h]
            o_ref[rows(n), :] = o
            s_f = c_dec_f * s_f + kv_update(k_b, kdf_ref[h], v_b)

        s_b = s0_ref[1, h] if has_state else jnp.zeros((DK_RET, DV_RET), F32)
        for n in reversed(range(n_chunks)):
            q_b = q_ref[rows(n), cols]
            k_b = k_ref[rows(n), cols]
            v_b = v_ref[rows(n), cols]
            if has_state or n < n_chunks - 1:
                o_ref[rows(n), :] += jnp.dot(q_b, s_b.astype(BF16), preferred_element_type=F32) * qdb_ref[h]
            s_b = c_dec_b * s_b + kv_update(k_b, kdb_ref[h], v_b)

        if emit_state:
            snew_ref[0, h] = s_f
            snew_ref[1, h] = s_b

        gn = gn_ref[:, cols]
        for n in range(n_chunks):
            o = o_ref[rows(n), :]
            mu = jnp.mean(o, axis=-1, keepdims=True)
            cen = o - mu
            var = jnp.mean(cen * cen, axis=-1, keepdims=True)
            on = cen * lax.rsqrt(var + EPS) * gn
            y_ref[rows(n), cols] = (jax.nn.silu(g_ref[rows(n), cols].astype(F32)) * on).astype(y_ref.dtype)


def _retention(proj, log_decay, gn, seq_len, layer, state=None, state_out=None):
    n = proj.shape[0]
    batch = n // seq_len
    has_state = state is not None
    chunk = min(RET_CHUNK, seq_len)
    blk = lambda off: pl.BlockSpec((seq_len, D_RET), lambda b: (b, off // D_RET))
    state_spec = pl.BlockSpec((None, None, 2, H_RET, DK_RET, DV_RET), lambda b: (b, layer, 0, 0, 0, 0))
    in_specs = [
        pl.BlockSpec(memory_space=pltpu.SMEM),
        blk(OFF_QR), blk(OFF_KR), blk(OFF_VR), blk(OFF_GR),
        pl.BlockSpec((1, D_RET), lambda b: (0, 0)),
    ]
    args = [log_decay, proj, proj, proj, proj, gn.reshape(1, D_RET)]
    if has_state:
        in_specs.append(state_spec)
        args.append(state)
    out_specs = [pl.BlockSpec((seq_len, D_RET), lambda b: (b, 0))]
    out_shape = [jax.ShapeDtypeStruct((n, D_RET), BF16)]
    aliases = {}
    if state_out is not None:
        depth, buf = state_out
        out_specs.append(state_spec)
        out_shape.append(jax.ShapeDtypeStruct((batch, depth, 2, H_RET, DK_RET, DV_RET), F32))
        if buf is not None:
            aliases[len(args)] = 1
            in_specs.append(pl.BlockSpec(memory_space=pl.ANY))
            args.append(buf)
    kern = functools.partial(_ret_kernel, seq_len=seq_len, chunk=chunk, has_state=has_state,
                             emit_state=state_out is not None, carried=bool(aliases))
    return pl.pallas_call(
        kern,
        grid=(batch,),
        in_specs=in_specs,
        out_specs=out_specs,
        out_shape=out_shape,
        input_output_aliases=aliases,
        scratch_shapes=[
            pltpu.VMEM((seq_len, DV_RET), F32),
            pltpu.VMEM((H_RET, chunk, chunk), F32),
            pltpu.VMEM((H_RET, chunk, DK_RET), F32),
            pltpu.VMEM((H_RET, chunk, DK_RET), F32),
            pltpu.VMEM((H_RET, chunk, DK_RET), F32),
            pltpu.VMEM((H_RET, chunk, DK_RET), F32),
        ],
        compiler_params=_params(1),
        name="retention",
    )(*args)


def _softmax_parts(scores, sink):
    m = jnp.maximum(sink, functools.reduce(
        jnp.maximum, [jnp.max(s, axis=-1, keepdims=True) for s in scores]))
    ps = [jnp.exp(s - m) for s in scores]
    denom = jnp.exp(sink - m) + functools.reduce(
        lambda a, b: a + b, [jnp.sum(p, axis=-1, keepdims=True) for p in ps])
    return ps, denom


def _ctx_attn_kernel(sink_ref, q_ref, k_ref, v_ref, o_ref, *, seq_len):
    scale = DH_ATT ** -0.5
    for s0 in range(0, q_ref.shape[0], seq_len):
        rows = slice(s0, s0 + seq_len)
        for head in range(H_ATT):
            cols = slice(head * DH_ATT, (head + 1) * DH_ATT)
            kv_cols = slice(head // G_ATT * DH_ATT, (head // G_ATT + 1) * DH_ATT)
            s = lax.dot_general(q_ref[rows, cols], k_ref[rows, kv_cols], (((1,), (1,)), ((), ())),
                                preferred_element_type=F32) * scale
            (p,), denom = _softmax_parts([s], sink_ref[head])
            o = jnp.dot(p.astype(BF16), v_ref[rows, kv_cols], preferred_element_type=F32)
            o_ref[rows, cols] = (o / denom).astype(o_ref.dtype)


def _context_attention(proj, sink, seq_len):
    n = proj.shape[0]
    kvw = H_KV * DH_ATT
    rows = CTX_SEQS * seq_len
    return pl.pallas_call(
        functools.partial(_ctx_attn_kernel, seq_len=seq_len),
        grid=(n // rows,),
        in_specs=[
            pl.BlockSpec(memory_space=pltpu.SMEM),
            pl.BlockSpec((rows, D_ATT), lambda b: (b, OFF_QA // D_ATT)),
            pl.BlockSpec((rows, kvw), lambda b: (b, OFF_KA // kvw)),
            pl.BlockSpec((rows, kvw), lambda b: (b, OFF_VA // kvw)),
        ],
        out_specs=pl.BlockSpec((rows, D_ATT), lambda b: (b, 0)),
        out_shape=jax.ShapeDtypeStruct((n, D_ATT), BF16),
        compiler_params=_params(1),
        name="context_attention",
    )(sink, proj, proj, proj)


def _rope(x, cos, sin_signed):
    quarter = DH_ATT // 4
    lane = lax.broadcasted_iota(jnp.int32, x.shape, 1)
    first = (lane % (2 * quarter)) < quarter
    partner = jnp.where(first, pltpu.roll(x, DH_ATT - quarter, axis=1), pltpu.roll(x, quarter, axis=1))
    return x * cos + partner * sin_signed


def _lat_attn_kernel(sink_ref, q_ref, k_ref, v_ref, ck_ref, cv_ref, cos_ref, sin_ref, o_ref,
                     kr_ref, *, seq_len):
    step = pl.program_id(1)
    scale = DH_ATT ** -0.5
    span = BLOCK + 2 * WINDOW

    @pl.when(step == 0)
    def _():
        for kv_head in range(H_KV):
            kv_cols = slice(kv_head * DH_ATT, (kv_head + 1) * DH_ATT)
            kr_ref[:, kv_cols] = _rope(k_ref[:, kv_cols].astype(F32), cos_ref[...], sin_ref[...]).astype(BF16)

    row_head = lax.broadcasted_iota(jnp.int32, (G_ATT * BLOCK, 1), 0) // BLOCK
    nt = (((1,), (1,)), ((), ()))

    for local in range(LAT_QBLOCKS):
        qb = step * LAT_QBLOCKS + local
        out_rows = slice(local * BLOCK, (local + 1) * BLOCK)
        q_rows = pl.ds(pl.multiple_of(qb * BLOCK, BLOCK), BLOCK)
        cos_q = cos_ref[q_rows, :]
        sin_q = sin_ref[q_rows, :]
        start = pl.multiple_of(jnp.clip(qb * BLOCK - WINDOW, 0, seq_len - span), BLOCK)
        q_pos = qb * BLOCK + lax.broadcasted_iota(jnp.int32, (BLOCK, span), 0)
        k_pos = start + lax.broadcasted_iota(jnp.int32, (BLOCK, span), 1)
        band = jnp.abs(k_pos - q_pos) <= WINDOW
        for kv_head in range(H_KV):
            kv_cols = slice(kv_head * DH_ATT, (kv_head + 1) * DH_ATT)
            heads = range(kv_head * G_ATT, (kv_head + 1) * G_ATT)
            q4 = jnp.concatenate(
                [_rope(q_ref[out_rows, h * DH_ATT:(h + 1) * DH_ATT].astype(F32), cos_q, sin_q) for h in heads],
                axis=0).astype(BF16)
            k_win = kr_ref[pl.ds(start, span), kv_cols]
            v_win = v_ref[pl.ds(start, span), kv_cols]
            s_loc = lax.dot_general(q4, k_win, nt, preferred_element_type=F32) * scale
            s_loc = jnp.where(band[None], s_loc.reshape(G_ATT, BLOCK, span), NEG_INF).reshape(G_ATT * BLOCK, span)
            s_ctx = lax.dot_general(q4, ck_ref[kv_head].astype(BF16), nt, preferred_element_type=F32) * scale
            sink4 = jnp.full((G_ATT * BLOCK, 1), sink_ref[kv_head * G_ATT], F32)
            for g in range(1, G_ATT):
                sink4 = jnp.where(row_head == g, sink_ref[kv_head * G_ATT + g], sink4)
            (p_loc, p_ctx), denom = _softmax_parts([s_loc, s_ctx], sink4)
            o4 = (jnp.dot(p_loc.astype(BF16), v_win, preferred_element_type=F32)
                  + jnp.dot(p_ctx.astype(BF16), cv_ref[kv_head].astype(BF16), preferred_element_type=F32)) / denom
            for g, h in enumerate(heads):
                o_ref[out_rows, h * DH_ATT:(h + 1) * DH_ATT] = o4[g * BLOCK:(g + 1) * BLOCK, :].astype(o_ref.dtype)


def _latent_attention(proj, sink, cache_k, cache_v, layer, seq_len, cos, sin_signed):
    n = proj.shape[0]
    batch = n // seq_len
    nq = seq_len // (LAT_QBLOCKS * BLOCK)
    q_rows = LAT_QBLOCKS * BLOCK
    kvw = H_KV * DH_ATT
    past = cache_k.shape[3]
    cache_spec = pl.BlockSpec((None, None, H_KV, past, DH_ATT), lambda b, i: (b, layer, 0, 0, 0))
    return pl.pallas_call(
        functools.partial(_lat_attn_kernel, seq_len=seq_len),
        grid=(batch, nq),
        in_specs=[
            pl.BlockSpec(memory_space=pltpu.SMEM),
            pl.BlockSpec((q_rows, D_ATT), lambda b, i: (b * nq + i, OFF_QA // D_ATT)),
            pl.BlockSpec((seq_len, kvw), lambda b, i: (b, OFF_KA // kvw)),
            pl.BlockSpec((seq_len, kvw), lambda b, i: (b, OFF_VA // kvw)),
            cache_spec, cache_spec,
            pl.BlockSpec((seq_len, DH_ATT), lambda b, i: (0, 0)),
            pl.BlockSpec((seq_len, DH_ATT), lambda b, i: (0, 0)),
        ],
        out_specs=pl.BlockSpec((q_rows, D_ATT), lambda b, i: (b * nq + i, 0)),
        out_shape=jax.ShapeDtypeStruct((n, D_ATT), BF16),
        scratch_shapes=[pltpu.VMEM((seq_len, kvw), BF16)],
        compiler_params=_params(2),
        name="latent_attention",
    )(sink, proj, proj, proj, cache_k, cache_v, cos, sin_signed)


def _rope_tables(seq_len):
    rows = seq_len // GRID_W
    row = jnp.repeat(jnp.arange(rows), GRID_W)
    col = jnp.tile(jnp.arange(GRID_W), rows)
    quarter = DH_ATT // 4
    inv = ROPE_BASE ** (-jnp.arange(quarter, dtype=F32) / quarter)
    ang_r, ang_c = row[:, None] * inv, col[:, None] * inv
    cos = jnp.concatenate([jnp.cos(ang_r)] * 2 + [jnp.cos(ang_c)] * 2, axis=-1)
    sin_signed = jnp.concatenate([-jnp.sin(ang_r), jnp.sin(ang_r), -jnp.sin(ang_c), jnp.sin(ang_c)], axis=-1)
    return cos.astype(F32), sin_signed.astype(F32)


def _out_proj_body(r):
    @pl.when(pl.program_id(1) == 0)
    def _():
        r["y"][:, :D_RET] = r["y_r"][...]
        r["y"][:, D_RET:] = r["y_a"][...]

    mix = jnp.dot(r["y"][...], _tile(r, "w"), preferred_element_type=F32)
    r["out"][...] = r["x"][...] + r["gate"][...] * mix


def _out_project(y_r, y_a, x, mod4, w, layer, tiles, rounding, out=None):
    n, d = x.shape
    tm = TOKEN_TILE
    tn = OUT_COLS_ROUNDING if rounding else OUT_COLS
    w_in, w_out = _weight_tile("w", w, layer, (D_RET + D_ATT, tn), lambda i, j: (0, j), rounding)
    ins = [("y_r", pl.BlockSpec((tm, D_RET), tiles.rows(D_RET, D_RET)), y_r),
           ("y_a", pl.BlockSpec((tm, D_ATT), tiles.rows(D_ATT, D_ATT)), y_a),
           w_in,
           ("x", pl.BlockSpec((tm, tn), tiles.rows(tn, d)), x),
           ("gate", tiles.mod(mod4, layer, 2, tn, lambda j: j), mod4)]
    outs = [("out", pl.BlockSpec((tm, tn), tiles.rows(tn, d)), jax.ShapeDtypeStruct((n, d), F32))]
    if w_out:
        outs.append(w_out)
    return _call(_out_proj_body, "out_proj", (tiles.count, d // tn), ins, outs,
                 [("y", pltpu.VMEM((tm, D_RET + D_ATT), BF16))], [("out", out)])


def _ffn_body(r, *, seq_len):
    j = pl.program_id(1)
    x_ref, o_ref, h_ref = r["x"], r["out"], r["h"]
    rows = x_ref.shape[0]

    cr = CONV_ROWS
    n_chunks = rows // cr

    def step(first):
        w_gate, w_val, w_down = _tile(r, "wg"), _tile(r, "wv"), _tile(r, "wd")
        t = lax.broadcasted_iota(jnp.int32, (cr, w_gate.shape[1]), 0)
        ups = []
        for m in range(n_chunks):
            rs = slice(m * cr, (m + 1) * cr)
            if first:
                for c in range(m * cr // NORM_ROWS, (m + 1) * cr // NORM_ROWS):
                    ns = slice(c * NORM_ROWS, (c + 1) * NORM_ROWS)
                    h_ref[ns, :] = _modulated_norm(x_ref[ns, :], r["scale"], r["shift"])
            hm = h_ref[rs, :]
            ups.append(tuple(jnp.dot(hm, w, preferred_element_type=F32) for w in (w_gate, w_val)))

        for m in range(n_chunks):
            seq_pos = (t + (m * cr) % seq_len) % seq_len
            seq_start, seq_end = seq_pos == 0, seq_pos == seq_len - 1

            def conv_branch(b, cw_ref, cb_ref):
                up = ups[m][b]
                prev = pltpu.roll(up, 1, axis=0)
                nxt = pltpu.roll(up, cr - 1, axis=0)
                if (m * cr) % seq_len:
                    prev = jnp.where(t == 0, ups[m - 1][b][cr - 1:cr, :], prev)
                if ((m + 1) * cr) % seq_len:
                    nxt = jnp.where(t == cr - 1, ups[m + 1][b][0:1, :], nxt)
                prev = jnp.where(seq_start, 0.0, prev)
                nxt = jnp.where(seq_end, 0.0, nxt)
                return cb_ref[...] + prev * cw_ref[0:1, :] + up * cw_ref[1:2, :] + nxt * cw_ref[2:3, :]

            act = jax.nn.silu(conv_branch(0, r["cwg"], r["cbg"])) * conv_branch(1, r["cwv"], r["cbv"])
            down = jnp.dot(act.astype(BF16), w_down, preferred_element_type=F32)
            rs = slice(m * cr, (m + 1) * cr)
            if first:
                o_ref[rs, :] = down
            else:
                o_ref[rs, :] += down

    pl.when(j == 0)(lambda: step(True))
    pl.when(j > 0)(lambda: step(False))

    @pl.when(j == pl.num_programs(1) - 1)
    def _():
        if "gain" not in r:
            o_ref[...] = x_ref[...] + r["gate"][...] * o_ref[...]
            return
        gate, gain = r["gate"][...], r["gain"][...]

        def body(c, carry):
            rs = pl.ds(pl.multiple_of(c * NORM_ROWS, NORM_ROWS), NORM_ROWS)
            y = x_ref[rs, :] + gate * o_ref[rs, :]
            o_ref[rs, :] = y * lax.rsqrt(jnp.mean(y * y, axis=-1, keepdims=True) + EPS) * gain
            return carry

        lax.fori_loop(0, rows // NORM_ROWS, body, 0)


def _conv_ffn(x, mod4, w_gate, w_val, conv_w, conv_b, w_down, layer, tiles, rounding, seq_len,
              gain=None, out=None):
    n, d = x.shape
    d_ff = w_down.shape[-2]
    tm = TOKEN_TILE
    tf = FFN_COLS_ROUNDING if rounding else FFN_COLS
    nf = d_ff // tf
    val0 = nf if rounding else 0
    wg_in, wg_out = _weight_tile("wg", w_gate, layer, (d, tf), lambda i, j: (0, j), rounding)
    wv_in, wv_out = _weight_tile("wv", w_val, layer, (d, tf), lambda i, j: (0, val0 + j), rounding)
    wd_in, wd_out = _weight_tile("wd", w_down, layer, (tf, d), lambda i, j: (j, 0), rounding)
    if rounding:
        half = jax.ShapeDtypeStruct((d, d_ff), BF16)
        wg_out = (wg_out[0], wg_out[1], half)
        wv_out = (wv_out[0], pl.BlockSpec((d, tf), lambda i, j: (0, j)), half)
    conv_b3 = conv_b.reshape(conv_b.shape[0], 1, 2 * d_ff)
    x_spec = pl.BlockSpec((tm, d), tiles.rows(d, d), **({"pipeline_mode": pl.Buffered(1)} if rounding else {}))
    ins = [("x", x_spec, x),
           ("shift", tiles.mod(mod4, layer, 3, d), mod4),
           ("scale", tiles.mod(mod4, layer, 4, d), mod4),
           ("gate", tiles.mod(mod4, layer, 5, d), mod4),
           wg_in, wv_in,
           ("cwg", pl.BlockSpec((None, CONV_W, tf), lambda i, j: (layer, 0, j)), conv_w),
           ("cwv", pl.BlockSpec((None, CONV_W, tf), lambda i, j: (layer, 0, nf + j)), conv_w),
           ("cbg", pl.BlockSpec((None, 1, tf), lambda i, j: (layer, 0, j)), conv_b3),
           ("cbv", pl.BlockSpec((None, 1, tf), lambda i, j: (layer, 0, nf + j)), conv_b3),
           wd_in]
    if gain is not None:
        ins.append(("gain", pl.BlockSpec((1, d), lambda i, j: (0, 0)), gain.reshape(1, d)))
    outs = [("out", pl.BlockSpec((tm, d), tiles.rows(d, d)), jax.ShapeDtypeStruct((n, d), F32))]
    outs += [o for o in (wg_out, wv_out, wd_out) if o]
    return _call(_ffn_body, "conv_ffn", (tiles.count, nf), ins, outs,
                 [("h", pltpu.VMEM((tm, d), BF16))], [("out", out)], seq_len=seq_len)


def kernel(x_prompt, x_sample, cache_k, cache_v, state_ret, c, c_ctx, w_mod, b_mod, w_in, w_out,
           ret_log_decay, ret_gn, att_sink, w_up, conv_w, conv_b, w_down, final_gain):
    batch, seq, d = x_prompt.shape
    dec_batch, dec_seq, _ = x_sample.shape
    depth = w_in.shape[0]
    assert TOKEN_TILE % seq == 0 and dec_seq == TOKEN_TILE and 1 + dec_batch <= COND_ROWS
    assert w_in.shape[-1] == D_IN and d == D_RET + D_ATT

    cond = jnp.zeros((COND_ROWS, d), F32).at[0].set(c_ctx).at[1:1 + dec_batch].set(c)
    mod = _modulation(cond, w_mod, b_mod)
    mod4 = mod.reshape(depth, COND_ROWS, 1, N_MOD * d)
    cos, sin_signed = _rope_tables(dec_seq)

    xp = x_prompt.reshape(batch * seq, d)
    xs = x_sample.reshape(dec_batch * dec_seq, d)
    n_ctx_tiles = batch * seq // TOKEN_TILE
    ctx_first = _Tiles(0, 1, 0, 0)
    ctx_rest = _Tiles(1, n_ctx_tiles - 1, 0, 0)
    lat = _Tiles(0, dec_batch, 1, 1)

    new_k = new_v = new_s = None
    for l in range(depth):
        o = _project(xp, mod4, w_in, l, ctx_first, True, cache=(seq, depth, new_k, new_v))
        w_in_l = o["w_bf"]
        o = _project(xp, mod4, w_in_l, l, ctx_rest, False, proj=o["proj"], cache=(seq, depth, o["k"], o["v"]))
        proj, new_k, new_v = o["proj"], o["k"], o["v"]
        y_r, new_s = _retention(proj, ret_log_decay[l], ret_gn[l], seq, l, state_out=(depth, new_s))
        y_a = _context_attention(proj, att_sink[l], seq)
        o = _out_project(y_r, y_a, xp, mod4, w_out, l, ctx_first, True)
        w_out_l = o["w_bf"]
        x1 = _out_project(y_r, y_a, xp, mod4, w_out_l, l, ctx_rest, False, out=o["out"])["out"]
        gain = final_gain if l == depth - 1 else None
        o = _conv_ffn(x1, mod4, w_up, w_up, conv_w, conv_b, w_down, l, ctx_first, True, seq, gain=gain)
        w_gate_l, w_val_l, w_down_l = o["wg_bf"], o["wv_bf"], o["wd_bf"]
        xp = _conv_ffn(x1, mod4, w_gate_l, w_val_l, conv_w, conv_b, w_down_l, l, ctx_rest, False, seq,
                       gain=gain, out=o["out"])["out"]
        proj = _project(xs, mod4, w_in_l, l, lat, False)["proj"]
        (y_r,) = _retention(proj, ret_log_decay[l], ret_gn[l], dec_seq, l, state=state_ret)
        y_a = _latent_attention(proj, att_sink[l], cache_k, cache_v, l, dec_seq, cos, sin_signed)
        x1 = _out_project(y_r, y_a, xs, mod4, w_out_l, l, lat, False)["out"]
        xs = _conv_ffn(x1, mod4, w_gate_l, w_val_l, conv_w, conv_b, w_down_l, l, lat, False, dec_seq,
                       gain=gain)["out"]

    return (xp.reshape(batch, seq, d), xs.reshape(dec_batch, dec_seq, d), new_k, new_v, new_s)
```

```python
import functools

import jax
import jax.numpy as jnp
from jax import lax
from jax.experimental import pallas as pl
from jax.experimental.pallas import tpu as pltpu

F32 = jnp.float32
BF16 = jnp.bfloat16

GRID_W = 64
H_RET = 8
DK_RET = 128
DV_RET = 128
D_RET = H_RET * DV_RET
H_ATT = 8
H_KV = 2
G_ATT = H_ATT // H_KV
DH_ATT = 128
D_ATT = H_ATT * DH_ATT
WINDOW = 128
BLOCK = 128
CONV_W = 3
N_MOD = 6
ROPE_BASE = 10000.0
EPS = 1e-6
NEG_INF = -1e30

OFF_QR = 0
OFF_KR = OFF_QR + H_RET * DK_RET
OFF_VR = OFF_KR + H_RET * DK_RET
OFF_GR = OFF_VR + H_RET * DV_RET
OFF_QA = OFF_GR + H_RET * DV_RET
OFF_KA = OFF_QA + H_ATT * DH_ATT
OFF_VA = OFF_KA + H_KV * DH_ATT
D_IN = OFF_VA + H_KV * DH_ATT
D_KV = 2 * H_KV * DH_ATT

COND_ROWS = 8
TOKEN_TILE = 1024
NORM_ROWS = 256
CONV_ROWS = 512
RET_CHUNK = 256
CTX_SEQS = 4
LAT_QBLOCKS = 4
PROJ_COLS, PROJ_COLS_ROUNDING = 1408, 512
OUT_COLS, OUT_COLS_ROUNDING = 1024, 512
FFN_COLS, FFN_COLS_ROUNDING = 512, 256
V7X_VMEM_LIMIT = 58 * 1024 * 1024


def _params(n_axes):
    return pltpu.CompilerParams(dimension_semantics=("arbitrary",) * n_axes,
                                vmem_limit_bytes=V7X_VMEM_LIMIT)


def _call(body, name, grid, ins, outs, scratch, carried=(), **static):
    in_names = [n for n, _, _ in ins]
    in_specs = [s for _, s, _ in ins]
    args = [a for _, _, a in ins]
    out_names = [n for n, _, _ in outs]
    aliases = {}
    for out_name, arr in carried:
        if arr is not None:
            aliases[len(args)] = out_names.index(out_name)
            in_names.append("carried_" + out_name)
            in_specs.append(pl.BlockSpec(memory_space=pl.ANY))
            args.append(arr)
    names = tuple(in_names + out_names + [n for n, _ in scratch])

    def kern(*refs):
        body(dict(zip(names, refs)), **static)

    res = pl.pallas_call(
        kern,
        grid=grid,
        in_specs=in_specs,
        out_specs=[s for _, s, _ in outs],
        out_shape=[s for _, _, s in outs],
        input_output_aliases=aliases,
        scratch_shapes=[s for _, s in scratch],
        compiler_params=_params(len(grid)),
        name=name,
    )(*args)
    return dict(zip(out_names, res))


def _mod_kernel(cond_ref, w_ref, b_ref, o_ref):
    a = jax.nn.silu(cond_ref[...]).astype(BF16)
    o_ref[...] = jnp.dot(a, w_ref[...].astype(BF16), preferred_element_type=F32) + b_ref[...]


def _modulation(cond, w_mod, b_mod):
    depth, d, n = w_mod.shape
    tn = 1024
    return pl.pallas_call(
        _mod_kernel,
        grid=(depth, n // tn),
        in_specs=[
            pl.BlockSpec((COND_ROWS, d), lambda l, j: (0, 0)),
            pl.BlockSpec((None, d, tn), lambda l, j: (l, 0, j)),
            pl.BlockSpec((None, 1, tn), lambda l, j: (l, 0, j)),
        ],
        out_specs=pl.BlockSpec((None, COND_ROWS, tn), lambda l, j: (l, 0, j)),
        out_shape=jax.ShapeDtypeStruct((depth, COND_ROWS, n), F32),
        compiler_params=_params(2),
        name="modulation",
    )(cond, w_mod, b_mod.reshape(depth, 1, n))


class _Tiles:
    def __init__(self, first, count, row0, row_step):
        self.first, self.count, self.row0, self.row_step = first, count, row0, row_step

    def rows(self, width, full_width=None):
        if width == full_width:
            return lambda i, j: (self.first + i, 0)
        return lambda i, j: (self.first + i, j)

    def mod(self, mod4, layer, chunk, width, col_block=lambda j: 0):
        per_chunk = mod4.shape[-1] // N_MOD // width
        return pl.BlockSpec(
            (None, None, 1, width),
            lambda i, j: (layer, self.row0 + self.row_step * (self.first + i), 0, chunk * per_chunk + col_block(j)))


def _weight_tile(name, w, layer, block, index, rounding):
    if not rounding:
        return (name, pl.BlockSpec(block, index), w), None
    spec = pl.BlockSpec((None,) + block, lambda i, j: (layer,) + index(i, j))
    out = (name + "_bf", pl.BlockSpec(block, index), jax.ShapeDtypeStruct(w.shape[1:], BF16))
    return (name, spec, w), out


def _tile(r, name):
    w = r[name][...]
    if name + "_bf" in r:
        w = w.astype(BF16)
        r[name + "_bf"][...] = w
    return w


def _modulated_norm(x, scale_ref, shift_ref):
    inv = lax.rsqrt(jnp.mean(x * x, axis=-1, keepdims=True) + EPS)
    return ((x * inv) * (1.0 + scale_ref[...]) + shift_ref[...]).astype(BF16)


def _proj_body(r, *, kv_step):
    j = pl.program_id(1)
    h_ref, proj_ref = r["h"], r["proj"]

    @pl.when(j == 0)
    def _():
        w = _tile(r, "w")
        for c in range(h_ref.shape[0] // NORM_ROWS):
            rs = slice(c * NORM_ROWS, (c + 1) * NORM_ROWS)
            h = _modulated_norm(r["x"][rs, :], r["scale"], r["shift"])
            h_ref[rs, :] = h
            proj_ref[rs, :] = jnp.dot(h, w, preferred_element_type=F32).astype(BF16)

    @pl.when(j > 0)
    def _():
        res = jnp.dot(h_ref[...], _tile(r, "w"), preferred_element_type=F32)
        proj_ref[...] = res.astype(BF16)
        if "k" in r:
            seqs, _, seq_len, _ = r["k"].shape
            kv0 = res.shape[1] - D_KV

            @pl.when(j == kv_step)
            def _():
                for dst, off in ((r["k"], kv0), (r["v"], kv0 + H_KV * DH_ATT)):
                    for s in range(seqs):
                        for hh in range(H_KV):
                            dst[s, hh] = res[s * seq_len:(s + 1) * seq_len,
                                             off + hh * DH_ATT:off + (hh + 1) * DH_ATT]


def _project(x, mod4, w, layer, tiles, rounding, proj=None, cache=None):
    n, d = x.shape
    d_in = w.shape[-1]
    tm = TOKEN_TILE
    tn = PROJ_COLS_ROUNDING if rounding else PROJ_COLS
    assert tn >= D_KV and d_in % tn == 0
    w_in, w_out = _weight_tile("w", w, layer, (d, tn), lambda i, j: (0, j), rounding)
    ins = [("x", pl.BlockSpec((tm, d), tiles.rows(d, d)), x),
           ("shift", tiles.mod(mod4, layer, 0, d), mod4),
           ("scale", tiles.mod(mod4, layer, 1, d), mod4),
           w_in]
    outs = [("proj", pl.BlockSpec((tm, tn), tiles.rows(tn, d_in)), jax.ShapeDtypeStruct((n, d_in), BF16))]
    carried = [("proj", proj)]
    if cache is not None:
        seq_len, depth, k_buf, v_buf = cache
        seqs = tm // seq_len
        shape = jax.ShapeDtypeStruct((n // seq_len, depth, H_KV, seq_len, DH_ATT), F32)
        spec = pl.BlockSpec((seqs, None, H_KV, seq_len, DH_ATT), lambda i, j: (tiles.first + i, layer, 0, 0, 0))
        outs += [("k", spec, shape), ("v", spec, shape)]
        carried += [("k", k_buf), ("v", v_buf)]
    if w_out:
        outs.append(w_out)
    return _call(_proj_body, "in_proj", (tiles.count, d_in // tn), ins, outs,
                 [("h", pltpu.VMEM((tm, d), BF16))], carried, kv_step=d_in // tn - 1)


def _ret_kernel(*refs, seq_len, chunk, has_state, emit_state, carried):
    it = iter(refs)
    ld_ref = next(it)
    q_ref, k_ref, v_ref, g_ref, gn_ref = (next(it) for _ in range(5))
    s0_ref = next(it) if has_state else None
    if carried:
        next(it)
    y_ref = next(it)
    snew_ref = next(it) if emit_state else None
    o_ref, intra_ref, qdf_ref, kdf_ref, qdb_ref, kdb_ref = (next(it) for _ in range(6))

    n_chunks = seq_len // chunk
    k_scale = DK_RET ** -0.5

    @pl.when(pl.program_id(0) == 0)
    def _():
        row = lax.broadcasted_iota(jnp.int32, (chunk, chunk), 0).astype(F32)
        col = lax.broadcasted_iota(jnp.int32, (chunk, chunk), 1).astype(F32)
        diff = row - col
        pos = lax.broadcasted_iota(jnp.int32, (chunk, DK_RET), 0).astype(F32)
        for h in range(H_RET):
            lg_f = ld_ref[0, h]
            lg_b = ld_ref[1, h]
            intra_ref[h] = (jnp.where(diff >= 0, jnp.exp(lg_f * jnp.maximum(diff, 0.0)), 0.0)
                            + jnp.where(diff <= 0, jnp.exp(lg_b * jnp.maximum(-diff, 0.0)), 0.0)) * k_scale
            qdf_ref[h] = jnp.exp(lg_f * (pos + 1.0))
            kdf_ref[h] = jnp.exp(lg_f * (chunk - 1.0 - pos)) * k_scale
            qdb_ref[h] = jnp.exp(lg_b * (chunk - pos))
            kdb_ref[h] = jnp.exp(lg_b * pos) * k_scale

    def rows(n):
        return pl.ds(n * chunk, chunk)

    def kv_update(k_b, dec, v_b):
        kd = (k_b.astype(F32) * dec).T.astype(BF16)
        return jnp.dot(kd, v_b, preferred_element_type=F32)

    for h in range(H_RET):
        cols = slice(h * DK_RET, (h + 1) * DK_RET)
        c_dec_f = jnp.exp(jnp.full((DK_RET, DV_RET), ld_ref[0, h] * chunk, F32))
        c_dec_b = jnp.exp(jnp.full((DK_RET, DV_RET), ld_ref[1, h] * chunk, F32))

        s_f = s0_ref[0, h] if has_state else jnp.zeros((DK_RET, DV_RET), F32)
        for n in range(n_chunks):
            q_b = q_ref[rows(n), cols]
            k_b = k_ref[rows(n), cols]
            v_b = v_ref[rows(n), cols]
            sc = lax.dot_general(q_b, k_b, (((1,), (1,)), ((), ())),
                                 preferred_element_type=F32) * intra_ref[h]
            o = jnp.dot(sc.astype(BF16), v_b, preferred_element_type=F32)
            if has_state or n > 0:
                o = o + jnp.dot(q_b, s_f.astype(BF16), preferred_element_type=F32) * qdf_ref[h]
            o_ref[rows(n), :] = o
            s_f = c_dec_f * s_f + kv_update(k_b, kdf_ref[h], v_b)

        s_b = s0_ref[1, h] if has_state else jnp.zeros((DK_RET, DV_RET), F32)
        for n in reversed(range(n_chunks)):
            q_b = q_ref[rows(n), cols]
            k_b = k_ref[rows(n), cols]
            v_b = v_ref[rows(n), cols]
            if has_state or n < n_chunks - 1:
                o_ref[rows(n), :] += jnp.dot(q_b, s_b.astype(BF16), preferred_element_type=F32) * qdb_ref[h]
            s_b = c_dec_b * s_b + kv_update(k_b, kdb_ref[h], v_b)

        if emit_state:
            snew_ref[0, h] = s_f
            snew_ref[1, h] = s_b

        gn = gn_ref[:, cols]
        for n in range(n_chunks):
            o = o_ref[rows(n), :]
            mu = jnp.mean(o, axis=-1, keepdims=True)
            cen = o - mu
            var = jnp.mean(cen * cen, axis=-1, keepdims=True)
            on = cen * lax.rsqrt(var + EPS) * gn
            y_ref[rows(n), cols] = (jax.nn.silu(g_ref[rows(n), cols].astype(F32)) * on).astype(y_ref.dtype)


def _retention(proj, log_decay, gn, seq_len, layer, state=None, state_out=None):
    n = proj.shape[0]
    batch = n // seq_len
    has_state = state is not None
    chunk = min(RET_CHUNK, seq_len)
    blk = lambda off: pl.BlockSpec((seq_len, D_RET), lambda b: (b, off // D_RET))
    state_spec = pl.BlockSpec((None, None, 2, H_RET, DK_RET, DV_RET), lambda b: (b, layer, 0, 0, 0, 0))
    in_specs = [
        pl.BlockSpec(memory_space=pltpu.SMEM),
        blk(OFF_QR), blk(OFF_KR), blk(OFF_VR), blk(OFF_GR),
        pl.BlockSpec((1, D_RET), lambda b: (0, 0)),
    ]
    args = [log_decay, proj, proj, proj, proj, gn.reshape(1, D_RET)]
    if has_state:
        in_specs.append(state_spec)
        args.append(state)
    out_specs = [pl.BlockSpec((seq_len, D_RET), lambda b: (b, 0))]
    out_shape = [jax.ShapeDtypeStruct((n, D_RET), BF16)]
    aliases = {}
    if state_out is not None:
        depth, buf = state_out
        out_specs.append(state_spec)
        out_shape.append(jax.ShapeDtypeStruct((batch, depth, 2, H_RET, DK_RET, DV_RET), F32))
        if buf is not None:
            aliases[len(args)] = 1
            in_specs.append(pl.BlockSpec(memory_space=pl.ANY))
            args.append(buf)
    kern = functools.partial(_ret_kernel, seq_len=seq_len, chunk=chunk, has_state=has_state,
                             emit_state=state_out is not None, carried=bool(aliases))
    return pl.pallas_call(
        kern,
        grid=(batch,),
        in_specs=in_specs,
        out_specs=out_specs,
        out_shape=out_shape,
        input_output_aliases=aliases,
        scratch_shapes=[
            pltpu.VMEM((seq_len, DV_RET), F32),
            pltpu.VMEM((H_RET, chunk, chunk), F32),
            pltpu.VMEM((H_RET, chunk, DK_RET), F32),
            pltpu.VMEM((H_RET, chunk, DK_RET), F32),
            pltpu.VMEM((H_RET, chunk, DK_RET), F32),
            pltpu.VMEM((H_RET, chunk, DK_RET), F32),
        ],
        compiler_params=_params(1),
        name="retention",
    )(*args)


def _softmax_parts(scores, sink):
    m = jnp.maximum(sink, functools.reduce(
        jnp.maximum, [jnp.max(s, axis=-1, keepdims=True) for s in scores]))
    ps = [jnp.exp(s - m) for s in scores]
    denom = jnp.exp(sink - m) + functools.reduce(
        lambda a, b: a + b, [jnp.sum(p, axis=-1, keepdims=True) for p in ps])
    return ps, denom


def _ctx_attn_kernel(sink_ref, q_ref, k_ref, v_ref, o_ref, *, seq_len):
    scale = DH_ATT ** -0.5
    for s0 in range(0, q_ref.shape[0], seq_len):
        rows = slice(s0, s0 + seq_len)
        for head in range(H_ATT):
            cols = slice(head * DH_ATT, (head + 1) * DH_ATT)
            kv_cols = slice(head // G_ATT * DH_ATT, (head // G_ATT + 1) * DH_ATT)
            s = lax.dot_general(q_ref[rows, cols], k_ref[rows, kv_cols], (((1,), (1,)), ((), ())),
                                preferred_element_type=F32) * scale
            (p,), denom = _softmax_parts([s], sink_ref[head])
            o = jnp.dot(p.astype(BF16), v_ref[rows, kv_cols], preferred_element_type=F32)
            o_ref[rows, cols] = (o / denom).astype(o_ref.dtype)


def _context_attention(proj, sink, seq_len):
    n = proj.shape[0]
    kvw = H_KV * DH_ATT
    rows = CTX_SEQS * seq_len
    return pl.pallas_call(
        functools.partial(_ctx_attn_kernel, seq_len=seq_len),
        grid=(n // rows,),
        in_specs=[
            pl.BlockSpec(memory_space=pltpu.SMEM),
            pl.BlockSpec((rows, D_ATT), lambda b: (b, OFF_QA // D_ATT)),
            pl.BlockSpec((rows, kvw), lambda b: (b, OFF_KA // kvw)),
            pl.BlockSpec((rows, kvw), lambda b: (b, OFF_VA // kvw)),
        ],
        out_specs=pl.BlockSpec((rows, D_ATT), lambda b: (b, 0)),
        out_shape=jax.ShapeDtypeStruct((n, D_ATT), BF16),
        compiler_params=_params(1),
        name="context_attention",
    )(sink, proj, proj, proj)


def _rope(x, cos, sin_signed):
    quarter = DH_ATT // 4
    lane = lax.broadcasted_iota(jnp.int32, x.shape, 1)
    first = (lane % (2 * quarter)) < quarter
    partner = jnp.where(first, pltpu.roll(x, DH_ATT - quarter, axis=1), pltpu.roll(x, quarter, axis=1))
    return x * cos + partner * sin_signed


def _lat_attn_kernel(sink_ref, q_ref, k_ref, v_ref, ck_ref, cv_ref, cos_ref, sin_ref, o_ref,
                     kr_ref, *, seq_len):
    step = pl.program_id(1)
    scale = DH_ATT ** -0.5
    span = BLOCK + 2 * WINDOW

    @pl.when(step == 0)
    def _():
        for kv_head in range(H_KV):
            kv_cols = slice(kv_head * DH_ATT, (kv_head + 1) * DH_ATT)
            kr_ref[:, kv_cols] = _rope(k_ref[:, kv_cols].astype(F32), cos_ref[...], sin_ref[...]).astype(BF16)

    row_head = lax.broadcasted_iota(jnp.int32, (G_ATT * BLOCK, 1), 0) // BLOCK
    nt = (((1,), (1,)), ((), ()))

    for local in range(LAT_QBLOCKS):
        qb = step * LAT_QBLOCKS + local
        out_rows = slice(local * BLOCK, (local + 1) * BLOCK)
        q_rows = pl.ds(pl.multiple_of(qb * BLOCK, BLOCK), BLOCK)
        cos_q = cos_ref[q_rows, :]
        sin_q = sin_ref[q_rows, :]
        start = pl.multiple_of(jnp.clip(qb * BLOCK - WINDOW, 0, seq_len - span), BLOCK)
        q_pos = qb * BLOCK + lax.broadcasted_iota(jnp.int32, (BLOCK, span), 0)
        k_pos = start + lax.broadcasted_iota(jnp.int32, (BLOCK, span), 1)
        band = jnp.abs(k_pos - q_pos) <= WINDOW
        for kv_head in range(H_KV):
            kv_cols = slice(kv_head * DH_ATT, (kv_head + 1) * DH_ATT)
            heads = range(kv_head * G_ATT, (kv_head + 1) * G_ATT)
            q4 = jnp.concatenate(
                [_rope(q_ref[out_rows, h * DH_ATT:(h + 1) * DH_ATT].astype(F32), cos_q, sin_q) for h in heads],
                axis=0).astype(BF16)
            k_win = kr_ref[pl.ds(start, span), kv_cols]
            v_win = v_ref[pl.ds(start, span), kv_cols]
            s_loc = lax.dot_general(q4, k_win, nt, preferred_element_type=F32) * scale
            s_loc = jnp.where(band[None], s_loc.reshape(G_ATT, BLOCK, span), NEG_INF).reshape(G_ATT * BLOCK, span)
            s_ctx = lax.dot_general(q4, ck_ref[kv_head].astype(BF16), nt, preferred_element_type=F32) * scale
            sink4 = jnp.full((G_ATT * BLOCK, 1), sink_ref[kv_head * G_ATT], F32)
            for g in range(1, G_ATT):
                sink4 = jnp.where(row_head == g, sink_ref[kv_head * G_ATT + g], sink4)
            (p_loc, p_ctx), denom = _softmax_parts([s_loc, s_ctx], sink4)
            o4 = (jnp.dot(p_loc.astype(BF16), v_win, preferred_element_type=F32)
                  + jnp.dot(p_ctx.astype(BF16), cv_ref[kv_head].astype(BF16), preferred_element_type=F32)) / denom
            for g, h in enumerate(heads):
                o_ref[out_rows, h * DH_ATT:(h + 1) * DH_ATT] = o4[g * BLOCK:(g + 1) * BLOCK, :].astype(o_ref.dtype)


def _latent_attention(proj, sink, cache_k, cache_v, layer, seq_len, cos, sin_signed):
    n = proj.shape[0]
    batch = n // seq_len
    nq = seq_len // (LAT_QBLOCKS * BLOCK)
    q_rows = LAT_QBLOCKS * BLOCK
    kvw = H_KV * DH_ATT
    past = cache_k.shape[3]
    cache_spec = pl.BlockSpec((None, None, H_KV, past, DH_ATT), lambda b, i: (b, layer, 0, 0, 0))
    return pl.pallas_call(
        functools.partial(_lat_attn_kernel, seq_len=seq_len),
        grid=(batch, nq),
        in_specs=[
            pl.BlockSpec(memory_space=pltpu.SMEM),
            pl.BlockSpec((q_rows, D_ATT), lambda b, i: (b * nq + i, OFF_QA // D_ATT)),
            pl.BlockSpec((seq_len, kvw), lambda b, i: (b, OFF_KA // kvw)),
            pl.BlockSpec((seq_len, kvw), lambda b, i: (b, OFF_VA // kvw)),
            cache_spec, cache_spec,
            pl.BlockSpec((seq_len, DH_ATT), lambda b, i: (0, 0)),
            pl.BlockSpec((seq_len, DH_ATT), lambda b, i: (0, 0)),
        ],
        out_specs=pl.BlockSpec((q_rows, D_ATT), lambda b, i: (b * nq + i, 0)),
        out_shape=jax.ShapeDtypeStruct((n, D_ATT), BF16),
        scratch_shapes=[pltpu.VMEM((seq_len, kvw), BF16)],
        compiler_params=_params(2),
        name="latent_attention",
    )(sink, proj, proj, proj, cache_k, cache_v, cos, sin_signed)


def _rope_tables(seq_len):
    rows = seq_len // GRID_W
    row = jnp.repeat(jnp.arange(rows), GRID_W)
    col = jnp.tile(jnp.arange(GRID_W), rows)
    quarter = DH_ATT // 4
    inv = ROPE_BASE ** (-jnp.arange(quarter, dtype=F32) / quarter)
    ang_r, ang_c = row[:, None] * inv, col[:, None] * inv
    cos = jnp.concatenate([jnp.cos(ang_r)] * 2 + [jnp.cos(ang_c)] * 2, axis=-1)
    sin_signed = jnp.concatenate([-jnp.sin(ang_r), jnp.sin(ang_r), -jnp.sin(ang_c), jnp.sin(ang_c)], axis=-1)
    return cos.astype(F32), sin_signed.astype(F32)


def _out_proj_body(r):
    w = _tile(r, "w")
    mix = (jnp.dot(r["y_r"][...], w[:D_RET], preferred_element_type=F32)
           + jnp.dot(r["y_a"][...], w[D_RET:], preferred_element_type=F32))
    r["out"][...] = r["x"][...] + r["gate"][...] * mix


def _out_project(y_r, y_a, x, mod4, w, layer, tiles, rounding, out=None):
    n, d = x.shape
    tm = TOKEN_TILE
    tn = OUT_COLS_ROUNDING if rounding else OUT_COLS
    w_in, w_out = _weight_tile("w", w, layer, (D_RET + D_ATT, tn), lambda i, j: (0, j), rounding)
    ins = [("y_r", pl.BlockSpec((tm, D_RET), tiles.rows(D_RET, D_RET)), y_r),
           ("y_a", pl.BlockSpec((tm, D_ATT), tiles.rows(D_ATT, D_ATT)), y_a),
           w_in,
           ("x", pl.BlockSpec((tm, tn), tiles.rows(tn, d)), x),
           ("gate", tiles.mod(mod4, layer, 2, tn, lambda j: j), mod4)]
    outs = [("out", pl.BlockSpec((tm, tn), tiles.rows(tn, d)), jax.ShapeDtypeStruct((n, d), F32))]
    if w_out:
        outs.append(w_out)
    return _call(_out_proj_body, "out_proj", (tiles.count, d // tn), ins, outs, [], [("out", out)])


def _ffn_body(r, *, seq_len):
    j = pl.program_id(1)
    x_ref, o_ref, h_ref = r["x"], r["out"], r["h"]
    rows = x_ref.shape[0]

    cr = CONV_ROWS
    n_chunks = rows // cr

    def step(first):
        w_gate, w_val, w_down = _tile(r, "wg"), _tile(r, "wv"), _tile(r, "wd")
        t = lax.broadcasted_iota(jnp.int32, (cr, w_gate.shape[1]), 0)
        ups = []
        for m in range(n_chunks):
            rs = slice(m * cr, (m + 1) * cr)
            if first:
                for c in range(m * cr // NORM_ROWS, (m + 1) * cr // NORM_ROWS):
                    ns = slice(c * NORM_ROWS, (c + 1) * NORM_ROWS)
                    h_ref[ns, :] = _modulated_norm(x_ref[ns, :], r["scale"], r["shift"])
            hm = h_ref[rs, :]
            ups.append(tuple(jnp.dot(hm, w, preferred_element_type=F32) for w in (w_gate, w_val)))

        for m in range(n_chunks):
            seq_pos = (t + (m * cr) % seq_len) % seq_len
            seq_start, seq_end = seq_pos == 0, seq_pos == seq_len - 1

            def conv_branch(b, cw_ref, cb_ref):
                up = ups[m][b]
                prev = pltpu.roll(up, 1, axis=0)
                nxt = pltpu.roll(up, cr - 1, axis=0)
                if (m * cr) % seq_len:
                    prev = jnp.where(t == 0, ups[m - 1][b][cr - 1:cr, :], prev)
                if ((m + 1) * cr) % seq_len:
                    nxt = jnp.where(t == cr - 1, ups[m + 1][b][0:1, :], nxt)
                prev = jnp.where(seq_start, 0.0, prev)
                nxt = jnp.where(seq_end, 0.0, nxt)
                return cb_ref[...] + prev * cw_ref[0:1, :] + up * cw_ref[1:2, :] + nxt * cw_ref[2:3, :]

            act = jax.nn.silu(conv_branch(0, r["cwg"], r["cbg"])) * conv_branch(1, r["cwv"], r["cbv"])
            down = jnp.dot(act.astype(BF16), w_down, preferred_element_type=F32)
            rs = slice(m * cr, (m + 1) * cr)
            if first:
                o_ref[rs, :] = down
            else:
                o_ref[rs, :] += down

    pl.when(j == 0)(lambda: step(True))
    pl.when(j > 0)(lambda: step(False))

    @pl.when(j == pl.num_programs(1) - 1)
    def _():
        if "gain" not in r:
            o_ref[...] = x_ref[...] + r["gate"][...] * o_ref[...]
            return
        gate, gain = r["gate"][...], r["gain"][...]

        def body(c, carry):
            rs = pl.ds(pl.multiple_of(c * NORM_ROWS, NORM_ROWS), NORM_ROWS)
            y = x_ref[rs, :] + gate * o_ref[rs, :]
            o_ref[rs, :] = y * lax.rsqrt(jnp.mean(y * y, axis=-1, keepdims=True) + EPS) * gain
            return carry

        lax.fori_loop(0, rows // NORM_ROWS, body, 0)


def _conv_ffn(x, mod4, w_gate, w_val, conv_w, conv_b, w_down, layer, tiles, rounding, seq_len,
              gain=None, out=None):
    n, d = x.shape
    d_ff = w_down.shape[-2]
    tm = TOKEN_TILE
    tf = FFN_COLS_ROUNDING if rounding else FFN_COLS
    nf = d_ff // tf
    val0 = nf if rounding else 0
    wg_in, wg_out = _weight_tile("wg", w_gate, layer, (d, tf), lambda i, j: (0, j), rounding)
    wv_in, wv_out = _weight_tile("wv", w_val, layer, (d, tf), lambda i, j: (0, val0 + j), rounding)
    wd_in, wd_out = _weight_tile("wd", w_down, layer, (tf, d), lambda i, j: (j, 0), rounding)
    if rounding:
        half = jax.ShapeDtypeStruct((d, d_ff), BF16)
        wg_out = (wg_out[0], wg_out[1], half)
        wv_out = (wv_out[0], pl.BlockSpec((d, tf), lambda i, j: (0, j)), half)
    conv_b3 = conv_b.reshape(conv_b.shape[0], 1, 2 * d_ff)
    x_spec = pl.BlockSpec((tm, d), tiles.rows(d, d), **({"pipeline_mode": pl.Buffered(1)} if rounding else {}))
    ins = [("x", x_spec, x),
           ("shift", tiles.mod(mod4, layer, 3, d), mod4),
           ("scale", tiles.mod(mod4, layer, 4, d), mod4),
           ("gate", tiles.mod(mod4, layer, 5, d), mod4),
           wg_in, wv_in,
           ("cwg", pl.BlockSpec((None, CONV_W, tf), lambda i, j: (layer, 0, j)), conv_w),
           ("cwv", pl.BlockSpec((None, CONV_W, tf), lambda i, j: (layer, 0, nf + j)), conv_w),
           ("cbg", pl.BlockSpec((None, 1, tf), lambda i, j: (layer, 0, j)), conv_b3),
           ("cbv", pl.BlockSpec((None, 1, tf), lambda i, j: (layer, 0, nf + j)), conv_b3),
           wd_in]
    if gain is not None:
        ins.append(("gain", pl.BlockSpec((1, d), lambda i, j: (0, 0)), gain.reshape(1, d)))
    outs = [("out", pl.BlockSpec((tm, d), tiles.rows(d, d)), jax.ShapeDtypeStruct((n, d), F32))]
    outs += [o for o in (wg_out, wv_out, wd_out) if o]
    return _call(_ffn_body, "conv_ffn", (tiles.count, nf), ins, outs,
                 [("h", pltpu.VMEM((tm, d), BF16))], [("out", out)], seq_len=seq_len)


def kernel(x_prompt, x_sample, cache_k, cache_v, state_ret, c, c_ctx, w_mod, b_mod, w_in, w_out,
           ret_log_decay, ret_gn, att_sink, w_up, conv_w, conv_b, w_down, final_gain):
    batch, seq, d = x_prompt.shape
    dec_batch, dec_seq, _ = x_sample.shape
    depth = w_in.shape[0]
    assert TOKEN_TILE % seq == 0 and dec_seq == TOKEN_TILE and 1 + dec_batch <= COND_ROWS
    assert w_in.shape[-1] == D_IN and d == D_RET + D_ATT

    cond = jnp.zeros((COND_ROWS, d), F32).at[0].set(c_ctx).at[1:1 + dec_batch].set(c)
    mod = _modulation(cond, w_mod, b_mod)
    mod4 = mod.reshape(depth, COND_ROWS, 1, N_MOD * d)
    cos, sin_signed = _rope_tables(dec_seq)

    xp = x_prompt.reshape(batch * seq, d)
    xs = x_sample.reshape(dec_batch * dec_seq, d)
    n_ctx_tiles = batch * seq // TOKEN_TILE
    ctx_first = _Tiles(0, 1, 0, 0)
    ctx_rest = _Tiles(1, n_ctx_tiles - 1, 0, 0)
    lat = _Tiles(0, dec_batch, 1, 1)

    new_k = new_v = new_s = None
    for l in range(depth):
        o = _project(xp, mod4, w_in, l, ctx_first, True, cache=(seq, depth, new_k, new_v))
        w_in_l = o["w_bf"]
        o = _project(xp, mod4, w_in_l, l, ctx_rest, False, proj=o["proj"], cache=(seq, depth, o["k"], o["v"]))
        proj, new_k, new_v = o["proj"], o["k"], o["v"]
        y_r, new_s = _retention(proj, ret_log_decay[l], ret_gn[l], seq, l, state_out=(depth, new_s))
        y_a = _context_attention(proj, att_sink[l], seq)
        o = _out_project(y_r, y_a, xp, mod4, w_out, l, ctx_first, True)
        w_out_l = o["w_bf"]
        x1 = _out_project(y_r, y_a, xp, mod4, w_out_l, l, ctx_rest, False, out=o["out"])["out"]
        gain = final_gain if l == depth - 1 else None
        o = _conv_ffn(x1, mod4, w_up, w_up, conv_w, conv_b, w_down, l, ctx_first, True, seq, gain=gain)
        w_gate_l, w_val_l, w_down_l = o["wg_bf"], o["wv_bf"], o["wd_bf"]
        xp = _conv_ffn(x1, mod4, w_gate_l, w_val_l, conv_w, conv_b, w_down_l, l, ctx_rest, False, seq,
                       gain=gain, out=o["out"])["out"]
        proj = _project(xs, mod4, w_in_l, l, lat, False)["proj"]
        (y_r,) = _retention(proj, ret_log_decay[l], ret_gn[l], dec_seq, l, state=state_ret)
        y_a = _latent_attention(proj, att_sink[l], cache_k, cache_v, l, dec_seq, cos, sin_signed)
        x1 = _out_project(y_r, y_a, xs, mod4, w_out_l, l, lat, False)["out"]
        xs = _conv_ffn(x1, mod4, w_gate_l, w_val_l, conv_w, conv_b, w_down_l, l, lat, False, dec_seq,
                       gain=gain)["out"]

    return (xp.reshape(batch, seq, d), xs.reshape(dec_batch, dec_seq, d), new_k, new_v, new_s)
```

```python
import functools

import jax
import jax.numpy as jnp
from jax import lax
from jax.experimental import pallas as pl
from jax.experimental.pallas import tpu as pltpu

F32 = jnp.float32
BF16 = jnp.bfloat16

GRID_W = 64
H_RET = 8
DK_RET = 128
DV_RET = 128
D_RET = H_RET * DV_RET
H_ATT = 8
H_KV = 2
G_ATT = H_ATT // H_KV
DH_ATT = 128
D_ATT = H_ATT * DH_ATT
WINDOW = 128
BLOCK = 128
CONV_W = 3
N_MOD = 6
ROPE_BASE = 10000.0
EPS = 1e-6
NEG_INF = -1e30

OFF_QR = 0
OFF_KR = OFF_QR + H_RET * DK_RET
OFF_VR = OFF_KR + H_RET * DK_RET
OFF_GR = OFF_VR + H_RET * DV_RET
OFF_QA = OFF_GR + H_RET * DV_RET
OFF_KA = OFF_QA + H_ATT * DH_ATT
OFF_VA = OFF_KA + H_KV * DH_ATT
D_IN = OFF_VA + H_KV * DH_ATT
D_KV = 2 * H_KV * DH_ATT

COND_ROWS = 8
TOKEN_TILE = 1024
NORM_ROWS = 256
CONV_ROWS = 512
RET_CHUNK = 256
CTX_SEQS = 4
LAT_QBLOCKS = 4
PROJ_COLS, PROJ_COLS_ROUNDING = 1408, 512
OUT_COLS, OUT_COLS_ROUNDING = 2048, 512
FFN_COLS, FFN_COLS_ROUNDING = 512, 256
V7X_VMEM_LIMIT = 58 * 1024 * 1024


def _params(n_axes):
    return pltpu.CompilerParams(dimension_semantics=("arbitrary",) * n_axes,
                                vmem_limit_bytes=V7X_VMEM_LIMIT)


def _call(body, name, grid, ins, outs, scratch, carried=(), **static):
    in_names = [n for n, _, _ in ins]
    in_specs = [s for _, s, _ in ins]
    args = [a for _, _, a in ins]
    out_names = [n for n, _, _ in outs]
    aliases = {}
    for out_name, arr in carried:
        if arr is not None:
            aliases[len(args)] = out_names.index(out_name)
            in_names.append("carried_" + out_name)
            in_specs.append(pl.BlockSpec(memory_space=pl.ANY))
            args.append(arr)
    names = tuple(in_names + out_names + [n for n, _ in scratch])

    def kern(*refs):
        body(dict(zip(names, refs)), **static)

    res = pl.pallas_call(
        kern,
        grid=grid,
        in_specs=in_specs,
        out_specs=[s for _, s, _ in outs],
        out_shape=[s for _, _, s in outs],
        input_output_aliases=aliases,
        scratch_shapes=[s for _, s in scratch],
        compiler_params=_params(len(grid)),
        name=name,
    )(*args)
    return dict(zip(out_names, res))


def _mod_kernel(cond_ref, w_ref, b_ref, o_ref):
    a = jax.nn.silu(cond_ref[...]).astype(BF16)
    o_ref[...] = jnp.dot(a, w_ref[...].astype(BF16), preferred_element_type=F32) + b_ref[...]


def _modulation(cond, w_mod, b_mod):
    depth, d, n = w_mod.shape
    tn = 1024
    return pl.pallas_call(
        _mod_kernel,
        grid=(depth, n // tn),
        in_specs=[
            pl.BlockSpec((COND_ROWS, d), lambda l, j: (0, 0)),
            pl.BlockSpec((None, d, tn), lambda l, j: (l, 0, j)),
            pl.BlockSpec((None, 1, tn), lambda l, j: (l, 0, j)),
        ],
        out_specs=pl.BlockSpec((None, COND_ROWS, tn), lambda l, j: (l, 0, j)),
        out_shape=jax.ShapeDtypeStruct((depth, COND_ROWS, n), F32),
        compiler_params=_params(2),
        name="modulation",
    )(cond, w_mod, b_mod.reshape(depth, 1, n))


class _Tiles:
    def __init__(self, first, count, row0, row_step):
        self.first, self.count, self.row0, self.row_step = first, count, row0, row_step

    def rows(self, width, full_width=None):
        if width == full_width:
            return lambda i, j: (self.first + i, 0)
        return lambda i, j: (self.first + i, j)

    def mod(self, mod4, layer, chunk, width, col_block=lambda j: 0):
        per_chunk = mod4.shape[-1] // N_MOD // width
        return pl.BlockSpec(
            (None, None, 1, width),
            lambda i, j: (layer, self.row0 + self.row_step * (self.first + i), 0, chunk * per_chunk + col_block(j)))


def _weight_tile(name, w, layer, block, index, rounding):
    if not rounding:
        return (name, pl.BlockSpec(block, index), w), None
    spec = pl.BlockSpec((None,) + block, lambda i, j: (layer,) + index(i, j))
    out = (name + "_bf", pl.BlockSpec(block, index), jax.ShapeDtypeStruct(w.shape[1:], BF16))
    return (name, spec, w), out


def _tile(r, name):
    w = r[name][...]
    if name + "_bf" in r:
        w = w.astype(BF16)
        r[name + "_bf"][...] = w
    return w


def _modulated_norm(x, scale_ref, shift_ref):
    inv = lax.rsqrt(jnp.mean(x * x, axis=-1, keepdims=True) + EPS)
    return ((x * inv) * (1.0 + scale_ref[...]) + shift_ref[...]).astype(BF16)


def _proj_body(r, *, kv_step):
    j = pl.program_id(1)
    h_ref, proj_ref = r["h"], r["proj"]

    @pl.when(j == 0)
    def _():
        w = _tile(r, "w")
        for c in range(h_ref.shape[0] // NORM_ROWS):
            rs = slice(c * NORM_ROWS, (c + 1) * NORM_ROWS)
            h = _modulated_norm(r["x"][rs, :], r["scale"], r["shift"])
            h_ref[rs, :] = h
            proj_ref[rs, :] = jnp.dot(h, w, preferred_element_type=F32).astype(BF16)

    @pl.when(j > 0)
    def _():
        res = jnp.dot(h_ref[...], _tile(r, "w"), preferred_element_type=F32)
        proj_ref[...] = res.astype(BF16)
        if "k" in r:
            seqs, _, seq_len, _ = r["k"].shape
            kv0 = res.shape[1] - D_KV

            @pl.when(j == kv_step)
            def _():
                for dst, off in ((r["k"], kv0), (r["v"], kv0 + H_KV * DH_ATT)):
                    for s in range(seqs):
                        for hh in range(H_KV):
                            dst[s, hh] = res[s * seq_len:(s + 1) * seq_len,
                                             off + hh * DH_ATT:off + (hh + 1) * DH_ATT]


def _project(x, mod4, w, layer, tiles, rounding, proj=None, cache=None):
    n, d = x.shape
    d_in = w.shape[-1]
    tm = TOKEN_TILE
    tn = PROJ_COLS_ROUNDING if rounding else PROJ_COLS
    assert tn >= D_KV and d_in % tn == 0
    w_in, w_out = _weight_tile("w", w, layer, (d, tn), lambda i, j: (0, j), rounding)
    ins = [("x", pl.BlockSpec((tm, d), tiles.rows(d, d)), x),
           ("shift", tiles.mod(mod4, layer, 0, d), mod4),
           ("scale", tiles.mod(mod4, layer, 1, d), mod4),
           w_in]
    outs = [("proj", pl.BlockSpec((tm, tn), tiles.rows(tn, d_in)), jax.ShapeDtypeStruct((n, d_in), BF16))]
    carried = [("proj", proj)]
    if cache is not None:
        seq_len, depth, k_buf, v_buf = cache
        seqs = tm // seq_len
        shape = jax.ShapeDtypeStruct((n // seq_len, depth, H_KV, seq_len, DH_ATT), F32)
        spec = pl.BlockSpec((seqs, None, H_KV, seq_len, DH_ATT), lambda i, j: (tiles.first + i, layer, 0, 0, 0))
        outs += [("k", spec, shape), ("v", spec, shape)]
        carried += [("k", k_buf), ("v", v_buf)]
    if w_out:
        outs.append(w_out)
    return _call(_proj_body, "in_proj", (tiles.count, d_in // tn), ins, outs,
                 [("h", pltpu.VMEM((tm, d), BF16))], carried, kv_step=d_in // tn - 1)


def _ret_kernel(*refs, seq_len, chunk, has_state, emit_state, carried):
    it = iter(refs)
    ld_ref = next(it)
    q_ref, k_ref, v_ref, g_ref, gn_ref = (next(it) for _ in range(5))
    s0_ref = next(it) if has_state else None
    if carried:
        next(it)
    y_ref = next(it)
    snew_ref = next(it) if emit_state else None
    o_ref, intra_ref, qdf_ref, kdf_ref, qdb_ref, kdb_ref = (next(it) for _ in range(6))

    n_chunks = seq_len // chunk
    k_scale = DK_RET ** -0.5

    @pl.when(pl.program_id(0) == 0)
    def _():
        row = lax.broadcasted_iota(jnp.int32, (chunk, chunk), 0).astype(F32)
        col = lax.broadcasted_iota(jnp.int32, (chunk, chunk), 1).astype(F32)
        diff = row - col
        pos = lax.broadcasted_iota(jnp.int32, (chunk, DK_RET), 0).astype(F32)
        for h in range(H_RET):
            lg_f = ld_ref[0, h]
            lg_b = ld_ref[1, h]
            intra_ref[h] = (jnp.where(diff >= 0, jnp.exp(lg_f * jnp.maximum(diff, 0.0)), 0.0)
                            + jnp.where(diff <= 0, jnp.exp(lg_b * jnp.maximum(-diff, 0.0)), 0.0)) * k_scale
            qdf_ref[h] = jnp.exp(lg_f * (pos + 1.0))
            kdf_ref[h] = jnp.exp(lg_f * (chunk - 1.0 - pos)) * k_scale
            qdb_ref[h] = jnp.exp(lg_b * (chunk - pos))
            kdb_ref[h] = jnp.exp(lg_b * pos) * k_scale

    def rows(n):
        return pl.ds(n * chunk, chunk)

    def kv_update(k_b, dec, v_b):
        kd = (k_b.astype(F32) * dec).T.astype(BF16)
        return jnp.dot(kd, v_b, preferred_element_type=F32)

    for h in range(H_RET):
        cols = slice(h * DK_RET, (h + 1) * DK_RET)
        c_dec_f = jnp.exp(jnp.full((DK_RET, DV_RET), ld_ref[0, h] * chunk, F32))
        c_dec_b = jnp.exp(jnp.full((DK_RET, DV_RET), ld_ref[1, h] * chunk, F32))

        s_f = s0_ref[0, h] if has_state else jnp.zeros((DK_RET, DV_RET), F32)
        for n in range(n_chunks):
            q_b = q_ref[rows(n), cols]
            k_b = k_ref[rows(n), cols]
            v_b = v_ref[rows(n), cols]
            sc = lax.dot_general(q_b, k_b, (((1,), (1,)), ((), ())),
                                 preferred_element_type=F32) * intra_ref[h]
            o = jnp.dot(sc.astype(BF16), v_b, preferred_element_type=F32)
            if has_state or n > 0:
                o = o + jnp.dot(q_b, s_f.astype(BF16), preferred_element_type=F32) * qdf_ref[h]
            o_ref[rows(n), :] = o
            s_f = c_dec_f * s_f + kv_update(k_b, kdf_ref[h], v_b)

        s_b = s0_ref[1, h] if has_state else jnp.zeros((DK_RET, DV_RET), F32)
        for n in reversed(range(n_chunks)):
            q_b = q_ref[rows(n), cols]
            k_b = k_ref[rows(n), cols]
            v_b = v_ref[rows(n), cols]
            if has_state or n < n_chunks - 1:
                o_ref[rows(n), :] += jnp.dot(q_b, s_b.astype(BF16), preferred_element_type=F32) * qdb_ref[h]
            s_b = c_dec_b * s_b + kv_update(k_b, kdb_ref[h], v_b)

        if emit_state:
            snew_ref[0, h] = s_f
            snew_ref[1, h] = s_b

        gn = gn_ref[:, cols]
        for n in range(n_chunks):
            o = o_ref[rows(n), :]
            mu = jnp.mean(o, axis=-1, keepdims=True)
            cen = o - mu
            var = jnp.mean(cen * cen, axis=-1, keepdims=True)
            on = cen * lax.rsqrt(var + EPS) * gn
            y_ref[rows(n), cols] = (jax.nn.silu(g_ref[rows(n), cols].astype(F32)) * on).astype(y_ref.dtype)


def _retention(proj, log_decay, gn, seq_len, layer, state=None, state_out=None):
    n = proj.shape[0]
    batch = n // seq_len
    has_state = state is not None
    chunk = min(RET_CHUNK, seq_len)
    blk = lambda off: pl.BlockSpec((seq_len, D_RET), lambda b: (b, off // D_RET))
    state_spec = pl.BlockSpec((None, None, 2, H_RET, DK_RET, DV_RET), lambda b: (b, layer, 0, 0, 0, 0))
    in_specs = [
        pl.BlockSpec(memory_space=pltpu.SMEM),
        blk(OFF_QR), blk(OFF_KR), blk(OFF_VR), blk(OFF_GR),
        pl.BlockSpec((1, D_RET), lambda b: (0, 0)),
    ]
    args = [log_decay, proj, proj, proj, proj, gn.reshape(1, D_RET)]
    if has_state:
        in_specs.append(state_spec)
        args.append(state)
    out_specs = [pl.BlockSpec((seq_len, D_RET), lambda b: (b, 0))]
    out_shape = [jax.ShapeDtypeStruct((n, D_RET), BF16)]
    aliases = {}
    if state_out is not None:
        depth, buf = state_out
        out_specs.append(state_spec)
        out_shape.append(jax.ShapeDtypeStruct((batch, depth, 2, H_RET, DK_RET, DV_RET), F32))
        if buf is not None:
            aliases[len(args)] = 1
            in_specs.append(pl.BlockSpec(memory_space=pl.ANY))
            args.append(buf)
    kern = functools.partial(_ret_kernel, seq_len=seq_len, chunk=chunk, has_state=has_state,
                             emit_state=state_out is not None, carried=bool(aliases))
    return pl.pallas_call(
        kern,
        grid=(batch,),
        in_specs=in_specs,
        out_specs=out_specs,
        out_shape=out_shape,
        input_output_aliases=aliases,
        scratch_shapes=[
            pltpu.VMEM((seq_len, DV_RET), F32),
            pltpu.VMEM((H_RET, chunk, chunk), F32),
            pltpu.VMEM((H_RET, chunk, DK_RET), F32),
            pltpu.VMEM((H_RET, chunk, DK_RET), F32),
            pltpu.VMEM((H_RET, chunk, DK_RET), F32),
            pltpu.VMEM((H_RET, chunk, DK_RET), F32),
        ],
        compiler_params=_params(1),
        name="retention",
    )(*args)


def _softmax_parts(scores, sink):
    m = jnp.maximum(sink, functools.reduce(
        jnp.maximum, [jnp.max(s, axis=-1, keepdims=True) for s in scores]))
    ps = [jnp.exp(s - m) for s in scores]
    denom = jnp.exp(sink - m) + functools.reduce(
        lambda a, b: a + b, [jnp.sum(p, axis=-1, keepdims=True) for p in ps])
    return ps, denom


def _ctx_attn_kernel(sink_ref, q_ref, k_ref, v_ref, o_ref, *, seq_len):
    scale = DH_ATT ** -0.5
    for s0 in range(0, q_ref.shape[0], seq_len):
        rows = slice(s0, s0 + seq_len)
        for head in range(H_ATT):
            cols = slice(head * DH_ATT, (head + 1) * DH_ATT)
            kv_cols = slice(head // G_ATT * DH_ATT, (head // G_ATT + 1) * DH_ATT)
            s = lax.dot_general(q_ref[rows, cols], k_ref[rows, kv_cols], (((1,), (1,)), ((), ())),
                                preferred_element_type=F32) * scale
            (p,), denom = _softmax_parts([s], sink_ref[head])
            o = jnp.dot(p.astype(BF16), v_ref[rows, kv_cols], preferred_element_type=F32)
            o_ref[rows, cols] = (o / denom).astype(o_ref.dtype)


def _context_attention(proj, sink, seq_len):
    n = proj.shape[0]
    kvw = H_KV * DH_ATT
    rows = CTX_SEQS * seq_len
    return pl.pallas_call(
        functools.partial(_ctx_attn_kernel, seq_len=seq_len),
        grid=(n // rows,),
        in_specs=[
            pl.BlockSpec(memory_space=pltpu.SMEM),
            pl.BlockSpec((rows, D_ATT), lambda b: (b, OFF_QA // D_ATT)),
            pl.BlockSpec((rows, kvw), lambda b: (b, OFF_KA // kvw)),
            pl.BlockSpec((rows, kvw), lambda b: (b, OFF_VA // kvw)),
        ],
        out_specs=pl.BlockSpec((rows, D_ATT), lambda b: (b, 0)),
        out_shape=jax.ShapeDtypeStruct((n, D_ATT), BF16),
        compiler_params=_params(1),
        name="context_attention",
    )(sink, proj, proj, proj)


def _rope(x, cos, sin_signed):
    quarter = DH_ATT // 4
    lane = lax.broadcasted_iota(jnp.int32, x.shape, 1)
    first = (lane % (2 * quarter)) < quarter
    partner = jnp.where(first, pltpu.roll(x, DH_ATT - quarter, axis=1), pltpu.roll(x, quarter, axis=1))
    return x * cos + partner * sin_signed


def _lat_attn_kernel(sink_ref, q_ref, k_ref, v_ref, ck_ref, cv_ref, cos_ref, sin_ref, o_ref,
                     kr_ref, *, seq_len):
    step = pl.program_id(1)
    scale = DH_ATT ** -0.5
    span = BLOCK + 2 * WINDOW

    @pl.when(step == 0)
    def _():
        for kv_head in range(H_KV):
            kv_cols = slice(kv_head * DH_ATT, (kv_head + 1) * DH_ATT)
            kr_ref[:, kv_cols] = _rope(k_ref[:, kv_cols].astype(F32), cos_ref[...], sin_ref[...]).astype(BF16)

    row_head = lax.broadcasted_iota(jnp.int32, (G_ATT * BLOCK, 1), 0) // BLOCK
    nt = (((1,), (1,)), ((), ()))

    for local in range(LAT_QBLOCKS):
        qb = step * LAT_QBLOCKS + local
        out_rows = slice(local * BLOCK, (local + 1) * BLOCK)
        q_rows = pl.ds(pl.multiple_of(qb * BLOCK, BLOCK), BLOCK)
        cos_q = cos_ref[q_rows, :]
        sin_q = sin_ref[q_rows, :]
        start = pl.multiple_of(jnp.clip(qb * BLOCK - WINDOW, 0, seq_len - span), BLOCK)
        q_pos = qb * BLOCK + lax.broadcasted_iota(jnp.int32, (BLOCK, span), 0)
        k_pos = start + lax.broadcasted_iota(jnp.int32, (BLOCK, span), 1)
        band = jnp.abs(k_pos - q_pos) <= WINDOW
        for kv_head in range(H_KV):
            kv_cols = slice(kv_head * DH_ATT, (kv_head + 1) * DH_ATT)
            heads = range(kv_head * G_ATT, (kv_head + 1) * G_ATT)
            q4 = jnp.concatenate(
                [_rope(q_ref[out_rows, h * DH_ATT:(h + 1) * DH_ATT].astype(F32), cos_q, sin_q) for h in heads],
                axis=0).astype(BF16)
            k_win = kr_ref[pl.ds(start, span), kv_cols]
            v_win = v_ref[pl.ds(start, span), kv_cols]
            s_loc = lax.dot_general(q4, k_win, nt, preferred_element_type=F32) * scale
            s_loc = jnp.where(band[None], s_loc.reshape(G_ATT, BLOCK, span), NEG_INF).reshape(G_ATT * BLOCK, span)
            s_ctx = lax.dot_general(q4, ck_ref[kv_head].astype(BF16), nt, preferred_element_type=F32) * scale
            sink4 = jnp.full((G_ATT * BLOCK, 1), sink_ref[kv_head * G_ATT], F32)
            for g in range(1, G_ATT):
                sink4 = jnp.where(row_head == g, sink_ref[kv_head * G_ATT + g], sink4)
            (p_loc, p_ctx), denom = _softmax_parts([s_loc, s_ctx], sink4)
            o4 = (jnp.dot(p_loc.astype(BF16), v_win, preferred_element_type=F32)
                  + jnp.dot(p_ctx.astype(BF16), cv_ref[kv_head].astype(BF16), preferred_element_type=F32)) / denom
            for g, h in enumerate(heads):
                o_ref[out_rows, h * DH_ATT:(h + 1) * DH_ATT] = o4[g * BLOCK:(g + 1) * BLOCK, :].astype(o_ref.dtype)


def _latent_attention(proj, sink, cache_k, cache_v, layer, seq_len, cos, sin_signed):
    n = proj.shape[0]
    batch = n // seq_len
    nq = seq_len // (LAT_QBLOCKS * BLOCK)
    q_rows = LAT_QBLOCKS * BLOCK
    kvw = H_KV * DH_ATT
    past = cache_k.shape[3]
    cache_spec = pl.BlockSpec((None, None, H_KV, past, DH_ATT), lambda b, i: (b, layer, 0, 0, 0))
    return pl.pallas_call(
        functools.partial(_lat_attn_kernel, seq_len=seq_len),
        grid=(batch, nq),
        in_specs=[
            pl.BlockSpec(memory_space=pltpu.SMEM),
            pl.BlockSpec((q_rows, D_ATT), lambda b, i: (b * nq + i, OFF_QA // D_ATT)),
            pl.BlockSpec((seq_len, kvw), lambda b, i: (b, OFF_KA // kvw)),
            pl.BlockSpec((seq_len, kvw), lambda b, i: (b, OFF_VA // kvw)),
            cache_spec, cache_spec,
            pl.BlockSpec((seq_len, DH_ATT), lambda b, i: (0, 0)),
            pl.BlockSpec((seq_len, DH_ATT), lambda b, i: (0, 0)),
        ],
        out_specs=pl.BlockSpec((q_rows, D_ATT), lambda b, i: (b * nq + i, 0)),
        out_shape=jax.ShapeDtypeStruct((n, D_ATT), BF16),
        scratch_shapes=[pltpu.VMEM((seq_len, kvw), BF16)],
        compiler_params=_params(2),
        name="latent_attention",
    )(sink, proj, proj, proj, cache_k, cache_v, cos, sin_signed)


def _rope_tables(seq_len):
    rows = seq_len // GRID_W
    row = jnp.repeat(jnp.arange(rows), GRID_W)
    col = jnp.tile(jnp.arange(GRID_W), rows)
    quarter = DH_ATT // 4
    inv = ROPE_BASE ** (-jnp.arange(quarter, dtype=F32) / quarter)
    ang_r, ang_c = row[:, None] * inv, col[:, None] * inv
    cos = jnp.concatenate([jnp.cos(ang_r)] * 2 + [jnp.cos(ang_c)] * 2, axis=-1)
    sin_signed = jnp.concatenate([-jnp.sin(ang_r), jnp.sin(ang_r), -jnp.sin(ang_c), jnp.sin(ang_c)], axis=-1)
    return cos.astype(F32), sin_signed.astype(F32)


def _out_proj_body(r):
    w_ref = r["w"]
    if "w_bf" in r:
        r["w_bf"][...] = w_ref[...].astype(BF16)
        w_ref = r["w_bf"]
    for c in range(r["x"].shape[0] // NORM_ROWS):
        rs = slice(c * NORM_ROWS, (c + 1) * NORM_ROWS)
        mix = (jnp.dot(r["y_r"][rs, :], w_ref[:D_RET, :], preferred_element_type=F32)
               + jnp.dot(r["y_a"][rs, :], w_ref[D_RET:, :], preferred_element_type=F32))
        r["out"][rs, :] = r["x"][rs, :] + r["gate"][...] * mix


def _out_project(y_r, y_a, x, mod4, w, layer, tiles, rounding, out=None):
    n, d = x.shape
    tm = TOKEN_TILE
    tn = OUT_COLS_ROUNDING if rounding else OUT_COLS
    w_in, w_out = _weight_tile("w", w, layer, (D_RET + D_ATT, tn), lambda i, j: (0, j), rounding)
    if tn == d:
        w_in = (w_in[0], pl.BlockSpec((D_RET + D_ATT, tn), lambda i, j: (0, 0), pipeline_mode=pl.Buffered(1)), w)
    ins = [("y_r", pl.BlockSpec((tm, D_RET), tiles.rows(D_RET, D_RET)), y_r),
           ("y_a", pl.BlockSpec((tm, D_ATT), tiles.rows(D_ATT, D_ATT)), y_a),
           w_in,
           ("x", pl.BlockSpec((tm, tn), tiles.rows(tn, d)), x),
           ("gate", tiles.mod(mod4, layer, 2, tn, lambda j: j), mod4)]
    outs = [("out", pl.BlockSpec((tm, tn), tiles.rows(tn, d)), jax.ShapeDtypeStruct((n, d), F32))]
    if w_out:
        outs.append(w_out)
    return _call(_out_proj_body, "out_proj", (tiles.count, d // tn), ins, outs, [], [("out", out)])


def _ffn_body(r, *, seq_len):
    j = pl.program_id(1)
    x_ref, o_ref, h_ref = r["x"], r["out"], r["h"]
    rows = x_ref.shape[0]

    cr = CONV_ROWS
    n_chunks = rows // cr

    def step(first):
        w_gate, w_val, w_down = _tile(r, "wg"), _tile(r, "wv"), _tile(r, "wd")
        t = lax.broadcasted_iota(jnp.int32, (cr, w_gate.shape[1]), 0)
        ups = []
        for m in range(n_chunks):
            rs = slice(m * cr, (m + 1) * cr)
            if first:
                for c in range(m * cr // NORM_ROWS, (m + 1) * cr // NORM_ROWS):
                    ns = slice(c * NORM_ROWS, (c + 1) * NORM_ROWS)
                    h_ref[ns, :] = _modulated_norm(x_ref[ns, :], r["scale"], r["shift"])
            hm = h_ref[rs, :]
            ups.append(tuple(jnp.dot(hm, w, preferred_element_type=F32) for w in (w_gate, w_val)))

        for m in range(n_chunks):
            seq_pos = (t + (m * cr) % seq_len) % seq_len
            seq_start, seq_end = seq_pos == 0, seq_pos == seq_len - 1

            def conv_branch(b, cw_ref, cb_ref):
                up = ups[m][b]
                prev = pltpu.roll(up, 1, axis=0)
                nxt = pltpu.roll(up, cr - 1, axis=0)
                if (m * cr) % seq_len:
                    prev = jnp.where(t == 0, ups[m - 1][b][cr - 1:cr, :], prev)
                if ((m + 1) * cr) % seq_len:
                    nxt = jnp.where(t == cr - 1, ups[m + 1][b][0:1, :], nxt)
                prev = jnp.where(seq_start, 0.0, prev)
                nxt = jnp.where(seq_end, 0.0, nxt)
                return cb_ref[...] + prev * cw_ref[0:1, :] + up * cw_ref[1:2, :] + nxt * cw_ref[2:3, :]

            act = jax.nn.silu(conv_branch(0, r["cwg"], r["cbg"])) * conv_branch(1, r["cwv"], r["cbv"])
            down = jnp.dot(act.astype(BF16), w_down, preferred_element_type=F32)
            rs = slice(m * cr, (m + 1) * cr)
            if first:
                o_ref[rs, :] = down
            else:
                o_ref[rs, :] += down

    pl.when(j == 0)(lambda: step(True))
    pl.when(j > 0)(lambda: step(False))

    @pl.when(j == pl.num_programs(1) - 1)
    def _():
        if "gain" not in r:
            o_ref[...] = x_ref[...] + r["gate"][...] * o_ref[...]
            return
        gate, gain = r["gate"][...], r["gain"][...]

        def body(c, carry):
            rs = pl.ds(pl.multiple_of(c * NORM_ROWS, NORM_ROWS), NORM_ROWS)
            y = x_ref[rs, :] + gate * o_ref[rs, :]
            o_ref[rs, :] = y * lax.rsqrt(jnp.mean(y * y, axis=-1, keepdims=True) + EPS) * gain
            return carry

        lax.fori_loop(0, rows // NORM_ROWS, body, 0)


def _conv_ffn(x, mod4, w_gate, w_val, conv_w, conv_b, w_down, layer, tiles, rounding, seq_len,
              gain=None, out=None):
    n, d = x.shape
    d_ff = w_down.shape[-2]
    tm = TOKEN_TILE
    tf = FFN_COLS_ROUNDING if rounding else FFN_COLS
    nf = d_ff // tf
    val0 = nf if rounding else 0
    wg_in, wg_out = _weight_tile("wg", w_gate, layer, (d, tf), lambda i, j: (0, j), rounding)
    wv_in, wv_out = _weight_tile("wv", w_val, layer, (d, tf), lambda i, j: (0, val0 + j), rounding)
    wd_in, wd_out = _weight_tile("wd", w_down, layer, (tf, d), lambda i, j: (j, 0), rounding)
    if rounding:
        half = jax.ShapeDtypeStruct((d, d_ff), BF16)
        wg_out = (wg_out[0], wg_out[1], half)
        wv_out = (wv_out[0], pl.BlockSpec((d, tf), lambda i, j: (0, j)), half)
    conv_b3 = conv_b.reshape(conv_b.shape[0], 1, 2 * d_ff)
    x_spec = pl.BlockSpec((tm, d), tiles.rows(d, d), **({"pipeline_mode": pl.Buffered(1)} if rounding else {}))
    ins = [("x", x_spec, x),
           ("shift", tiles.mod(mod4, layer, 3, d), mod4),
           ("scale", tiles.mod(mod4, layer, 4, d), mod4),
           ("gate", tiles.mod(mod4, layer, 5, d), mod4),
           wg_in, wv_in,
           ("cwg", pl.BlockSpec((None, CONV_W, tf), lambda i, j: (layer, 0, j)), conv_w),
           ("cwv", pl.BlockSpec((None, CONV_W, tf), lambda i, j: (layer, 0, nf + j)), conv_w),
           ("cbg", pl.BlockSpec((None, 1, tf), lambda i, j: (layer, 0, j)), conv_b3),
           ("cbv", pl.BlockSpec((None, 1, tf), lambda i, j: (layer, 0, nf + j)), conv_b3),
           wd_in]
    if gain is not None:
        ins.append(("gain", pl.BlockSpec((1, d), lambda i, j: (0, 0)), gain.reshape(1, d)))
    outs = [("out", pl.BlockSpec((tm, d), tiles.rows(d, d)), jax.ShapeDtypeStruct((n, d), F32))]
    outs += [o for o in (wg_out, wv_out, wd_out) if o]
    return _call(_ffn_body, "conv_ffn", (tiles.count, nf), ins, outs,
                 [("h", pltpu.VMEM((tm, d), BF16))], [("out", out)], seq_len=seq_len)


def kernel(x_prompt, x_sample, cache_k, cache_v, state_ret, c, c_ctx, w_mod, b_mod, w_in, w_out,
           ret_log_decay, ret_gn, att_sink, w_up, conv_w, conv_b, w_down, final_gain):
    batch, seq, d = x_prompt.shape
    dec_batch, dec_seq, _ = x_sample.shape
    depth = w_in.shape[0]
    assert TOKEN_TILE % seq == 0 and dec_seq == TOKEN_TILE and 1 + dec_batch <= COND_ROWS
    assert w_in.shape[-1] == D_IN and d == D_RET + D_ATT

    cond = jnp.zeros((COND_ROWS, d), F32).at[0].set(c_ctx).at[1:1 + dec_batch].set(c)
    mod = _modulation(cond, w_mod, b_mod)
    mod4 = mod.reshape(depth, COND_ROWS, 1, N_MOD * d)
    cos, sin_signed = _rope_tables(dec_seq)

    xp = x_prompt.reshape(batch * seq, d)
    xs = x_sample.reshape(dec_batch * dec_seq, d)
    n_ctx_tiles = batch * seq // TOKEN_TILE
    ctx_first = _Tiles(0, 1, 0, 0)
    ctx_rest = _Tiles(1, n_ctx_tiles - 1, 0, 0)
    lat = _Tiles(0, dec_batch, 1, 1)

    new_k = new_v = new_s = None
    for l in range(depth):
        o = _project(xp, mod4, w_in, l, ctx_first, True, cache=(seq, depth, new_k, new_v))
        w_in_l = o["w_bf"]
        o = _project(xp, mod4, w_in_l, l, ctx_rest, False, proj=o["proj"], cache=(seq, depth, o["k"], o["v"]))
        proj, new_k, new_v = o["proj"], o["k"], o["v"]
        y_r, new_s = _retention(proj, ret_log_decay[l], ret_gn[l], seq, l, state_out=(depth, new_s))
        y_a = _context_attention(proj, att_sink[l], seq)
        o = _out_project(y_r, y_a, xp, mod4, w_out, l, ctx_first, True)
        w_out_l = o["w_bf"]
        x1 = _out_project(y_r, y_a, xp, mod4, w_out_l, l, ctx_rest, False, out=o["out"])["out"]
        gain = final_gain if l == depth - 1 else None
        o = _conv_ffn(x1, mod4, w_up, w_up, conv_w, conv_b, w_down, l, ctx_first, True, seq, gain=gain)
        w_gate_l, w_val_l, w_down_l = o["wg_bf"], o["wv_bf"], o["wd_bf"]
        xp = _conv_ffn(x1, mod4, w_gate_l, w_val_l, conv_w, conv_b, w_down_l, l, ctx_rest, False, seq,
                       gain=gain, out=o["out"])["out"]
        proj = _project(xs, mod4, w_in_l, l, lat, False)["proj"]
        (y_r,) = _retention(proj, ret_log_decay[l], ret_gn[l], dec_seq, l, state=state_ret)
        y_a = _latent_attention(proj, att_sink[l], cache_k, cache_v, l, dec_seq, cos, sin_signed)
        x1 = _out_project(y_r, y_a, xs, mod4, w_out_l, l, lat, False)["out"]
        xs = _conv_ffn(x1, mod4, w_gate_l, w_val_l, conv_w, conv_b, w_down_l, l, lat, False, dec_seq,
                       gain=gain)["out"]

    return (xp.reshape(batch, seq, d), xs.reshape(dec_batch, dec_seq, d), new_k, new_v, new_s)
```

```python
import functools

import jax
import jax.numpy as jnp
from jax import lax
from jax.experimental import pallas as pl
from jax.experimental.pallas import tpu as pltpu

F32 = jnp.float32
BF16 = jnp.bfloat16

GRID_W = 64
H_RET = 8
DK_RET = 128
DV_RET = 128
D_RET = H_RET * DV_RET
H_ATT = 8
H_KV = 2
G_ATT = H_ATT // H_KV
DH_ATT = 128
D_ATT = H_ATT * DH_ATT
WINDOW = 128
BLOCK = 128
CONV_W = 3
N_MOD = 6
ROPE_BASE = 10000.0
EPS = 1e-6
NEG_INF = -1e30

OFF_QR = 0
OFF_KR = OFF_QR + H_RET * DK_RET
OFF_VR = OFF_KR + H_RET * DK_RET
OFF_GR = OFF_VR + H_RET * DV_RET
OFF_QA = OFF_GR + H_RET * DV_RET
OFF_KA = OFF_QA + H_ATT * DH_ATT
OFF_VA = OFF_KA + H_KV * DH_ATT
D_IN = OFF_VA + H_KV * DH_ATT
D_KV = 2 * H_KV * DH_ATT

COND_ROWS = 8
TOKEN_TILE = 1024
NORM_ROWS = 256
CONV_ROWS = 512
RET_CHUNK = 256
CTX_SEQS = 4
LAT_QBLOCKS = 4
PROJ_COLS, PROJ_COLS_ROUNDING = 1408, 512
OUT_COLS, OUT_COLS_ROUNDING = 2048, 512
FFN_COLS, FFN_COLS_ROUNDING = 512, 256
V7X_VMEM_LIMIT = 58 * 1024 * 1024


def _params(n_axes):
    return pltpu.CompilerParams(dimension_semantics=("arbitrary",) * n_axes,
                                vmem_limit_bytes=V7X_VMEM_LIMIT)


def _call(body, name, grid, ins, outs, scratch, carried=(), **static):
    in_names = [n for n, _, _ in ins]
    in_specs = [s for _, s, _ in ins]
    args = [a for _, _, a in ins]
    out_names = [n for n, _, _ in outs]
    aliases = {}
    for out_name, arr in carried:
        if arr is not None:
            aliases[len(args)] = out_names.index(out_name)
            in_names.append("carried_" + out_name)
            in_specs.append(pl.BlockSpec(memory_space=pl.ANY))
            args.append(arr)
    names = tuple(in_names + out_names + [n for n, _ in scratch])

    def kern(*refs):
        body(dict(zip(names, refs)), **static)

    res = pl.pallas_call(
        kern,
        grid=grid,
        in_specs=in_specs,
        out_specs=[s for _, s, _ in outs],
        out_shape=[s for _, _, s in outs],
        input_output_aliases=aliases,
        scratch_shapes=[s for _, s in scratch],
        compiler_params=_params(len(grid)),
        name=name,
    )(*args)
    return dict(zip(out_names, res))


def _mod_kernel(cond_ref, w_ref, b_ref, o_ref):
    a = jax.nn.silu(cond_ref[...]).astype(BF16)
    o_ref[...] = jnp.dot(a, w_ref[...].astype(BF16), preferred_element_type=F32) + b_ref[...]


def _modulation(cond, w_mod, b_mod):
    depth, d, n = w_mod.shape
    tn = 1024
    return pl.pallas_call(
        _mod_kernel,
        grid=(depth, n // tn),
        in_specs=[
            pl.BlockSpec((COND_ROWS, d), lambda l, j: (0, 0)),
            pl.BlockSpec((None, d, tn), lambda l, j: (l, 0, j)),
            pl.BlockSpec((None, 1, tn), lambda l, j: (l, 0, j)),
        ],
        out_specs=pl.BlockSpec((None, COND_ROWS, tn), lambda l, j: (l, 0, j)),
        out_shape=jax.ShapeDtypeStruct((depth, COND_ROWS, n), F32),
        compiler_params=_params(2),
        name="modulation",
    )(cond, w_mod, b_mod.reshape(depth, 1, n))


class _Tiles:
    def __init__(self, first, count, row0, row_step):
        self.first, self.count, self.row0, self.row_step = first, count, row0, row_step

    def rows(self, width, full_width=None):
        if width == full_width:
            return lambda i, j: (self.first + i, 0)
        return lambda i, j: (self.first + i, j)

    def mod(self, mod4, layer, chunk, width, col_block=lambda j: 0):
        per_chunk = mod4.shape[-1] // N_MOD // width
        return pl.BlockSpec(
            (None, None, 1, width),
            lambda i, j: (layer, self.row0 + self.row_step * (self.first + i), 0, chunk * per_chunk + col_block(j)))


def _weight_tile(name, w, layer, block, index, rounding):
    if not rounding:
        return (name, pl.BlockSpec(block, index), w), None
    spec = pl.BlockSpec((None,) + block, lambda i, j: (layer,) + index(i, j))
    out = (name + "_bf", pl.BlockSpec(block, index), jax.ShapeDtypeStruct(w.shape[1:], BF16))
    return (name, spec, w), out


def _tile(r, name):
    w = r[name][...]
    if name + "_bf" in r:
        w = w.astype(BF16)
        r[name + "_bf"][...] = w
    return w


def _modulated_norm(x, scale_ref, shift_ref):
    inv = lax.rsqrt(jnp.mean(x * x, axis=-1, keepdims=True) + EPS)
    return ((x * inv) * (1.0 + scale_ref[...]) + shift_ref[...]).astype(BF16)


def _proj_body(r, *, kv_step):
    j = pl.program_id(1)
    h_ref, proj_ref = r["h"], r["proj"]

    @pl.when(j == 0)
    def _():
        w = _tile(r, "w")
        for c in range(h_ref.shape[0] // NORM_ROWS):
            rs = slice(c * NORM_ROWS, (c + 1) * NORM_ROWS)
            h = _modulated_norm(r["x"][rs, :], r["scale"], r["shift"])
            h_ref[rs, :] = h
            proj_ref[rs, :] = jnp.dot(h, w, preferred_element_type=F32).astype(BF16)

    @pl.when(j > 0)
    def _():
        res = jnp.dot(h_ref[...], _tile(r, "w"), preferred_element_type=F32)
        proj_ref[...] = res.astype(BF16)
        if "k" in r:
            seqs, _, seq_len, _ = r["k"].shape
            kv0 = res.shape[1] - D_KV

            @pl.when(j == kv_step)
            def _():
                for dst, off in ((r["k"], kv0), (r["v"], kv0 + H_KV * DH_ATT)):
                    for s in range(seqs):
                        for hh in range(H_KV):
                            dst[s, hh] = res[s * seq_len:(s + 1) * seq_len,
                                             off + hh * DH_ATT:off + (hh + 1) * DH_ATT]


def _project(x, mod4, w, layer, tiles, rounding, proj=None, cache=None):
    n, d = x.shape
    d_in = w.shape[-1]
    tm = TOKEN_TILE
    tn = PROJ_COLS_ROUNDING if rounding else PROJ_COLS
    assert tn >= D_KV and d_in % tn == 0
    w_in, w_out = _weight_tile("w", w, layer, (d, tn), lambda i, j: (0, j), rounding)
    ins = [("x", pl.BlockSpec((tm, d), tiles.rows(d, d)), x),
           ("shift", tiles.mod(mod4, layer, 0, d), mod4),
           ("scale", tiles.mod(mod4, layer, 1, d), mod4),
           w_in]
    outs = [("proj", pl.BlockSpec((tm, tn), tiles.rows(tn, d_in)), jax.ShapeDtypeStruct((n, d_in), BF16))]
    carried = [("proj", proj)]
    if cache is not None:
        seq_len, depth, k_buf, v_buf = cache
        seqs = tm // seq_len
        shape = jax.ShapeDtypeStruct((n // seq_len, depth, H_KV, seq_len, DH_ATT), F32)
        spec = pl.BlockSpec((seqs, None, H_KV, seq_len, DH_ATT), lambda i, j: (tiles.first + i, layer, 0, 0, 0))
        outs += [("k", spec, shape), ("v", spec, shape)]
        carried += [("k", k_buf), ("v", v_buf)]
    if w_out:
        outs.append(w_out)
    return _call(_proj_body, "in_proj", (tiles.count, d_in // tn), ins, outs,
                 [("h", pltpu.VMEM((tm, d), BF16))], carried, kv_step=d_in // tn - 1)


def _ret_kernel(*refs, seq_len, chunk, has_state, emit_state, carried):
    it = iter(refs)
    ld_ref = next(it)
    q_ref, k_ref, v_ref, g_ref, gn_ref = (next(it) for _ in range(5))
    s0_ref = next(it) if has_state else None
    if carried:
        next(it)
    y_ref = next(it)
    snew_ref = next(it) if emit_state else None
    o_ref, intra_ref, qdf_ref, kdf_ref, qdb_ref, kdb_ref = (next(it) for _ in range(6))

    n_chunks = seq_len // chunk
    k_scale = DK_RET ** -0.5

    @pl.when(pl.program_id(0) == 0)
    def _():
        row = lax.broadcasted_iota(jnp.int32, (chunk, chunk), 0).astype(F32)
        col = lax.broadcasted_iota(jnp.int32, (chunk, chunk), 1).astype(F32)
        diff = row - col
        pos = lax.broadcasted_iota(jnp.int32, (chunk, DK_RET), 0).astype(F32)
        for h in range(H_RET):
            lg_f = ld_ref[0, h]
            lg_b = ld_ref[1, h]
            intra_ref[h] = (jnp.where(diff >= 0, jnp.exp(lg_f * jnp.maximum(diff, 0.0)), 0.0)
                            + jnp.where(diff <= 0, jnp.exp(lg_b * jnp.maximum(-diff, 0.0)), 0.0)) * k_scale
            qdf_ref[h] = jnp.exp(lg_f * (pos + 1.0))
            kdf_ref[h] = jnp.exp(lg_f * (chunk - 1.0 - pos)) * k_scale
            qdb_ref[h] = jnp.exp(lg_b * (chunk - pos))
            kdb_ref[h] = jnp.exp(lg_b * pos) * k_scale

    def rows(n):
        return pl.ds(n * chunk, chunk)

    def kv_update(k_b, dec, v_b):
        kd = (k_b.astype(F32) * dec).T.astype(BF16)
        return jnp.dot(kd, v_b, preferred_element_type=F32)

    for h in range(H_RET):
        cols = slice(h * DK_RET, (h + 1) * DK_RET)
        c_dec_f = jnp.exp(jnp.full((DK_RET, DV_RET), ld_ref[0, h] * chunk, F32))
        c_dec_b = jnp.exp(jnp.full((DK_RET, DV_RET), ld_ref[1, h] * chunk, F32))

        s_f = s0_ref[0, h] if has_state else jnp.zeros((DK_RET, DV_RET), F32)
        for n in range(n_chunks):
            q_b = q_ref[rows(n), cols]
            k_b = k_ref[rows(n), cols]
            v_b = v_ref[rows(n), cols]
            sc = lax.dot_general(q_b, k_b, (((1,), (1,)), ((), ())),
                                 preferred_element_type=F32) * intra_ref[h]
            o = jnp.dot(sc.astype(BF16), v_b, preferred_element_type=F32)
            if has_state or n > 0:
                o = o + jnp.dot(q_b, s_f.astype(BF16), preferred_element_type=F32) * qdf_ref[h]
            o_ref[rows(n), :] = o
            s_f = c_dec_f * s_f + kv_update(k_b, kdf_ref[h], v_b)

        s_b = s0_ref[1, h] if has_state else jnp.zeros((DK_RET, DV_RET), F32)
        for n in reversed(range(n_chunks)):
            q_b = q_ref[rows(n), cols]
            k_b = k_ref[rows(n), cols]
            v_b = v_ref[rows(n), cols]
            if has_state or n < n_chunks - 1:
                o_ref[rows(n), :] += jnp.dot(q_b, s_b.astype(BF16), preferred_element_type=F32) * qdb_ref[h]
            s_b = c_dec_b * s_b + kv_update(k_b, kdb_ref[h], v_b)

        if emit_state:
            snew_ref[0, h] = s_f
            snew_ref[1, h] = s_b

        gn = gn_ref[:, cols]
        for n in range(n_chunks):
            o = o_ref[rows(n), :]
            mu = jnp.mean(o, axis=-1, keepdims=True)
            cen = o - mu
            var = jnp.mean(cen * cen, axis=-1, keepdims=True)
            on = cen * lax.rsqrt(var + EPS) * gn
            y_ref[rows(n), cols] = (jax.nn.silu(g_ref[rows(n), cols].astype(F32)) * on).astype(y_ref.dtype)


def _retention(proj, log_decay, gn, seq_len, layer, state=None, state_out=None):
    n = proj.shape[0]
    batch = n // seq_len
    has_state = state is not None
    chunk = min(RET_CHUNK, seq_len)
    blk = lambda off: pl.BlockSpec((seq_len, D_RET), lambda b: (b, off // D_RET))
    state_spec = pl.BlockSpec((None, None, 2, H_RET, DK_RET, DV_RET), lambda b: (b, layer, 0, 0, 0, 0))
    in_specs = [
        pl.BlockSpec(memory_space=pltpu.SMEM),
        blk(OFF_QR), blk(OFF_KR), blk(OFF_VR), blk(OFF_GR),
        pl.BlockSpec((1, D_RET), lambda b: (0, 0)),
    ]
    args = [log_decay, proj, proj, proj, proj, gn.reshape(1, D_RET)]
    if has_state:
        in_specs.append(state_spec)
        args.append(state)
    out_specs = [pl.BlockSpec((seq_len, D_RET), lambda b: (b, 0))]
    out_shape = [jax.ShapeDtypeStruct((n, D_RET), BF16)]
    aliases = {}
    if state_out is not None:
        depth, buf = state_out
        out_specs.append(state_spec)
        out_shape.append(jax.ShapeDtypeStruct((batch, depth, 2, H_RET, DK_RET, DV_RET), F32))
        if buf is not None:
            aliases[len(args)] = 1
            in_specs.append(pl.BlockSpec(memory_space=pl.ANY))
            args.append(buf)
    kern = functools.partial(_ret_kernel, seq_len=seq_len, chunk=chunk, has_state=has_state,
                             emit_state=state_out is not None, carried=bool(aliases))
    return pl.pallas_call(
        kern,
        grid=(batch,),
        in_specs=in_specs,
        out_specs=out_specs,
        out_shape=out_shape,
        input_output_aliases=aliases,
        scratch_shapes=[
            pltpu.VMEM((seq_len, DV_RET), F32),
            pltpu.VMEM((H_RET, chunk, chunk), F32),
            pltpu.VMEM((H_RET, chunk, DK_RET), F32),
            pltpu.VMEM((H_RET, chunk, DK_RET), F32),
            pltpu.VMEM((H_RET, chunk, DK_RET), F32),
            pltpu.VMEM((H_RET, chunk, DK_RET), F32),
        ],
        compiler_params=_params(1),
        name="retention",
    )(*args)


def _softmax_parts(scores, sink):
    m = jnp.maximum(sink, functools.reduce(
        jnp.maximum, [jnp.max(s, axis=-1, keepdims=True) for s in scores]))
    ps = [jnp.exp(s - m) for s in scores]
    denom = jnp.exp(sink - m) + functools.reduce(
        lambda a, b: a + b, [jnp.sum(p, axis=-1, keepdims=True) for p in ps])
    return ps, denom


def _ctx_attn_kernel(sink_ref, q_ref, k_ref, v_ref, o_ref, *, seq_len):
    scale = DH_ATT ** -0.5
    for s0 in range(0, q_ref.shape[0], seq_len):
        rows = slice(s0, s0 + seq_len)
        for head in range(H_ATT):
            cols = slice(head * DH_ATT, (head + 1) * DH_ATT)
            kv_cols = slice(head // G_ATT * DH_ATT, (head // G_ATT + 1) * DH_ATT)
            s = lax.dot_general(q_ref[rows, cols], k_ref[rows, kv_cols], (((1,), (1,)), ((), ())),
                                preferred_element_type=F32) * scale
            (p,), denom = _softmax_parts([s], sink_ref[head])
            o = jnp.dot(p.astype(BF16), v_ref[rows, kv_cols], preferred_element_type=F32)
            o_ref[rows, cols] = (o / denom).astype(o_ref.dtype)


def _context_attention(proj, sink, seq_len):
    n = proj.shape[0]
    kvw = H_KV * DH_ATT
    rows = CTX_SEQS * seq_len
    return pl.pallas_call(
        functools.partial(_ctx_attn_kernel, seq_len=seq_len),
        grid=(n // rows,),
        in_specs=[
            pl.BlockSpec(memory_space=pltpu.SMEM),
            pl.BlockSpec((rows, D_ATT), lambda b: (b, OFF_QA // D_ATT)),
            pl.BlockSpec((rows, kvw), lambda b: (b, OFF_KA // kvw)),
            pl.BlockSpec((rows, kvw), lambda b: (b, OFF_VA // kvw)),
        ],
        out_specs=pl.BlockSpec((rows, D_ATT), lambda b: (b, 0)),
        out_shape=jax.ShapeDtypeStruct((n, D_ATT), BF16),
        compiler_params=_params(1),
        name="context_attention",
    )(sink, proj, proj, proj)


def _rope(x, cos, sin_signed):
    quarter = DH_ATT // 4
    lane = lax.broadcasted_iota(jnp.int32, x.shape, 1)
    first = (lane % (2 * quarter)) < quarter
    partner = jnp.where(first, pltpu.roll(x, DH_ATT - quarter, axis=1), pltpu.roll(x, quarter, axis=1))
    return x * cos + partner * sin_signed


def _lat_attn_kernel(sink_ref, q_ref, k_ref, v_ref, ck_ref, cv_ref, cos_ref, sin_ref, o_ref,
                     kr_ref, *, seq_len):
    step = pl.program_id(1)
    scale = DH_ATT ** -0.5
    span = BLOCK + 2 * WINDOW

    @pl.when(step == 0)
    def _():
        for kv_head in range(H_KV):
            kv_cols = slice(kv_head * DH_ATT, (kv_head + 1) * DH_ATT)
            kr_ref[:, kv_cols] = _rope(k_ref[:, kv_cols].astype(F32), cos_ref[...], sin_ref[...]).astype(BF16)

    row_head = lax.broadcasted_iota(jnp.int32, (G_ATT * BLOCK, 1), 0) // BLOCK
    nt = (((1,), (1,)), ((), ()))

    for local in range(LAT_QBLOCKS):
        qb = step * LAT_QBLOCKS + local
        out_rows = slice(local * BLOCK, (local + 1) * BLOCK)
        q_rows = pl.ds(pl.multiple_of(qb * BLOCK, BLOCK), BLOCK)
        cos_q = cos_ref[q_rows, :]
        sin_q = sin_ref[q_rows, :]
        start = pl.multiple_of(jnp.clip(qb * BLOCK - WINDOW, 0, seq_len - span), BLOCK)
        q_pos = qb * BLOCK + lax.broadcasted_iota(jnp.int32, (BLOCK, span), 0)
        k_pos = start + lax.broadcasted_iota(jnp.int32, (BLOCK, span), 1)
        band = jnp.abs(k_pos - q_pos) <= WINDOW
        for kv_head in range(H_KV):
            kv_cols = slice(kv_head * DH_ATT, (kv_head + 1) * DH_ATT)
            heads = range(kv_head * G_ATT, (kv_head + 1) * G_ATT)
            q4 = jnp.concatenate(
                [_rope(q_ref[out_rows, h * DH_ATT:(h + 1) * DH_ATT].astype(F32), cos_q, sin_q) for h in heads],
                axis=0).astype(BF16)
            k_win = kr_ref[pl.ds(start, span), kv_cols]
            v_win = v_ref[pl.ds(start, span), kv_cols]
            s_loc = lax.dot_general(q4, k_win, nt, preferred_element_type=F32) * scale
            s_loc = jnp.where(band[None], s_loc.reshape(G_ATT, BLOCK, span), NEG_INF).reshape(G_ATT * BLOCK, span)
            s_ctx = lax.dot_general(q4, ck_ref[kv_head].astype(BF16), nt, preferred_element_type=F32) * scale
            sink4 = jnp.full((G_ATT * BLOCK, 1), sink_ref[kv_head * G_ATT], F32)
            for g in range(1, G_ATT):
                sink4 = jnp.where(row_head == g, sink_ref[kv_head * G_ATT + g], sink4)
            (p_loc, p_ctx), denom = _softmax_parts([s_loc, s_ctx], sink4)
            o4 = (jnp.dot(p_loc.astype(BF16), v_win, preferred_element_type=F32)
                  + jnp.dot(p_ctx.astype(BF16), cv_ref[kv_head].astype(BF16), preferred_element_type=F32)) / denom
            for g, h in enumerate(heads):
                o_ref[out_rows, h * DH_ATT:(h + 1) * DH_ATT] = o4[g * BLOCK:(g + 1) * BLOCK, :].astype(o_ref.dtype)


def _latent_attention(proj, sink, cache_k, cache_v, layer, seq_len, cos, sin_signed):
    n = proj.shape[0]
    batch = n // seq_len
    nq = seq_len // (LAT_QBLOCKS * BLOCK)
    q_rows = LAT_QBLOCKS * BLOCK
    kvw = H_KV * DH_ATT
    past = cache_k.shape[3]
    cache_spec = pl.BlockSpec((None, None, H_KV, past, DH_ATT), lambda b, i: (b, layer, 0, 0, 0))
    return pl.pallas_call(
        functools.partial(_lat_attn_kernel, seq_len=seq_len),
        grid=(batch, nq),
        in_specs=[
            pl.BlockSpec(memory_space=pltpu.SMEM),
            pl.BlockSpec((q_rows, D_ATT), lambda b, i: (b * nq + i, OFF_QA // D_ATT)),
            pl.BlockSpec((seq_len, kvw), lambda b, i: (b, OFF_KA // kvw)),
            pl.BlockSpec((seq_len, kvw), lambda b, i: (b, OFF_VA // kvw)),
            cache_spec, cache_spec,
            pl.BlockSpec((seq_len, DH_ATT), lambda b, i: (0, 0)),
            pl.BlockSpec((seq_len, DH_ATT), lambda b, i: (0, 0)),
        ],
        out_specs=pl.BlockSpec((q_rows, D_ATT), lambda b, i: (b * nq + i, 0)),
        out_shape=jax.ShapeDtypeStruct((n, D_ATT), BF16),
        scratch_shapes=[pltpu.VMEM((seq_len, kvw), BF16)],
        compiler_params=_params(2),
        name="latent_attention",
    )(sink, proj, proj, proj, cache_k, cache_v, cos, sin_signed)


def _rope_tables(seq_len):
    rows = seq_len // GRID_W
    row = jnp.repeat(jnp.arange(rows), GRID_W)
    col = jnp.tile(jnp.arange(GRID_W), rows)
    quarter = DH_ATT // 4
    inv = ROPE_BASE ** (-jnp.arange(quarter, dtype=F32) / quarter)
    ang_r, ang_c = row[:, None] * inv, col[:, None] * inv
    cos = jnp.concatenate([jnp.cos(ang_r)] * 2 + [jnp.cos(ang_c)] * 2, axis=-1)
    sin_signed = jnp.concatenate([-jnp.sin(ang_r), jnp.sin(ang_r), -jnp.sin(ang_c), jnp.sin(ang_c)], axis=-1)
    return cos.astype(F32), sin_signed.astype(F32)


def _out_proj_body(r):
    w_ref = r["w"]
    if "w_bf" in r:
        r["w_bf"][...] = w_ref[...].astype(BF16)
        w_ref = r["w_bf"]
    for c in range(r["x"].shape[0] // NORM_ROWS):
        rs = slice(c * NORM_ROWS, (c + 1) * NORM_ROWS)
        mix = (jnp.dot(r["y_r"][rs, :], w_ref[:D_RET, :], preferred_element_type=F32)
               + jnp.dot(r["y_a"][rs, :], w_ref[D_RET:, :], preferred_element_type=F32))
        r["out"][rs, :] = r["x"][rs, :] + r["gate"][...] * mix


def _out_project(y_r, y_a, x, mod4, w, layer, tiles, rounding, out=None):
    n, d = x.shape
    tm = TOKEN_TILE
    tn = OUT_COLS_ROUNDING if rounding else OUT_COLS
    w_in, w_out = _weight_tile("w", w, layer, (D_RET + D_ATT, tn), lambda i, j: (0, j), rounding)
    if tn == d:
        w_in = (w_in[0], pl.BlockSpec((D_RET + D_ATT, tn), lambda i, j: (0, 0), pipeline_mode=pl.Buffered(1)), w)
    ins = [("y_r", pl.BlockSpec((tm, D_RET), tiles.rows(D_RET, D_RET)), y_r),
           ("y_a", pl.BlockSpec((tm, D_ATT), tiles.rows(D_ATT, D_ATT)), y_a),
           w_in,
           ("x", pl.BlockSpec((tm, tn), tiles.rows(tn, d)), x),
           ("gate", tiles.mod(mod4, layer, 2, tn, lambda j: j), mod4)]
    outs = [("out", pl.BlockSpec((tm, tn), tiles.rows(tn, d)), jax.ShapeDtypeStruct((n, d), F32))]
    if w_out:
        outs.append(w_out)
    return _call(_out_proj_body, "out_proj", (tiles.count, d // tn), ins, outs, [], [("out", out)])


def _ffn_body(r, *, seq_len):
    j = pl.program_id(1)
    x_ref, o_ref, h_ref = r["x"], r["out"], r["h"]
    rows = x_ref.shape[0]

    cr = CONV_ROWS
    n_chunks = rows // cr

    def step(first, last):
        w_gate, w_val, w_down = _tile(r, "wg"), _tile(r, "wv"), _tile(r, "wd")
        t = lax.broadcasted_iota(jnp.int32, (cr, w_gate.shape[1]), 0)
        ups = []
        for m in range(n_chunks):
            rs = slice(m * cr, (m + 1) * cr)
            if first:
                for c in range(m * cr // NORM_ROWS, (m + 1) * cr // NORM_ROWS):
                    ns = slice(c * NORM_ROWS, (c + 1) * NORM_ROWS)
                    h_ref[ns, :] = _modulated_norm(x_ref[ns, :], r["scale"], r["shift"])
            hm = h_ref[rs, :]
            ups.append(tuple(jnp.dot(hm, w, preferred_element_type=F32) for w in (w_gate, w_val)))

        for m in range(n_chunks):
            seq_pos = (t + (m * cr) % seq_len) % seq_len
            seq_start, seq_end = seq_pos == 0, seq_pos == seq_len - 1

            def conv_branch(b, cw_ref, cb_ref):
                up = ups[m][b]
                prev = pltpu.roll(up, 1, axis=0)
                nxt = pltpu.roll(up, cr - 1, axis=0)
                if (m * cr) % seq_len:
                    prev = jnp.where(t == 0, ups[m - 1][b][cr - 1:cr, :], prev)
                if ((m + 1) * cr) % seq_len:
                    nxt = jnp.where(t == cr - 1, ups[m + 1][b][0:1, :], nxt)
                prev = jnp.where(seq_start, 0.0, prev)
                nxt = jnp.where(seq_end, 0.0, nxt)
                return cb_ref[...] + prev * cw_ref[0:1, :] + up * cw_ref[1:2, :] + nxt * cw_ref[2:3, :]

            act = jax.nn.silu(conv_branch(0, r["cwg"], r["cbg"])) * conv_branch(1, r["cwv"], r["cbv"])
            down = jnp.dot(act.astype(BF16), w_down, preferred_element_type=F32)
            rs = slice(m * cr, (m + 1) * cr)
            if first:
                o_ref[rs, :] = down
            elif last:
                o_ref[rs, :] = x_ref[rs, :] + r["gate"][...] * (o_ref[rs, :] + down)
            else:
                o_ref[rs, :] += down

    n_steps = pl.num_programs(1)
    if "gain" not in r:
        pl.when(j == 0)(lambda: step(True, False))
        pl.when((j > 0) & (j < n_steps - 1))(lambda: step(False, False))
        pl.when(j == n_steps - 1)(lambda: step(False, True))
        return
    pl.when(j == 0)(lambda: step(True, False))
    pl.when(j > 0)(lambda: step(False, False))

    @pl.when(j == n_steps - 1)
    def _():
        gate, gain = r["gate"][...], r["gain"][...]

        def body(c, carry):
            rs = pl.ds(pl.multiple_of(c * NORM_ROWS, NORM_ROWS), NORM_ROWS)
            y = x_ref[rs, :] + gate * o_ref[rs, :]
            o_ref[rs, :] = y * lax.rsqrt(jnp.mean(y * y, axis=-1, keepdims=True) + EPS) * gain
            return carry

        lax.fori_loop(0, rows // NORM_ROWS, body, 0)


def _conv_ffn(x, mod4, w_gate, w_val, conv_w, conv_b, w_down, layer, tiles, rounding, seq_len,
              gain=None, out=None):
    n, d = x.shape
    d_ff = w_down.shape[-2]
    tm = TOKEN_TILE
    tf = FFN_COLS_ROUNDING if rounding else FFN_COLS
    nf = d_ff // tf
    val0 = nf if rounding else 0
    wg_in, wg_out = _weight_tile("wg", w_gate, layer, (d, tf), lambda i, j: (0, j), rounding)
    wv_in, wv_out = _weight_tile("wv", w_val, layer, (d, tf), lambda i, j: (0, val0 + j), rounding)
    wd_in, wd_out = _weight_tile("wd", w_down, layer, (tf, d), lambda i, j: (j, 0), rounding)
    if rounding:
        half = jax.ShapeDtypeStruct((d, d_ff), BF16)
        wg_out = (wg_out[0], wg_out[1], half)
        wv_out = (wv_out[0], pl.BlockSpec((d, tf), lambda i, j: (0, j)), half)
    conv_b3 = conv_b.reshape(conv_b.shape[0], 1, 2 * d_ff)
    x_spec = pl.BlockSpec((tm, d), tiles.rows(d, d), **({"pipeline_mode": pl.Buffered(1)} if rounding else {}))
    ins = [("x", x_spec, x),
           ("shift", tiles.mod(mod4, layer, 3, d), mod4),
           ("scale", tiles.mod(mod4, layer, 4, d), mod4),
           ("gate", tiles.mod(mod4, layer, 5, d), mod4),
           wg_in, wv_in,
           ("cwg", pl.BlockSpec((None, CONV_W, tf), lambda i, j: (layer, 0, j)), conv_w),
           ("cwv", pl.BlockSpec((None, CONV_W, tf), lambda i, j: (layer, 0, nf + j)), conv_w),
           ("cbg", pl.BlockSpec((None, 1, tf), lambda i, j: (layer, 0, j)), conv_b3),
           ("cbv", pl.BlockSpec((None, 1, tf), lambda i, j: (layer, 0, nf + j)), conv_b3),
           wd_in]
    if gain is not None:
        ins.append(("gain", pl.BlockSpec((1, d), lambda i, j: (0, 0)), gain.reshape(1, d)))
    outs = [("out", pl.BlockSpec((tm, d), tiles.rows(d, d)), jax.ShapeDtypeStruct((n, d), F32))]
    outs += [o for o in (wg_out, wv_out, wd_out) if o]
    return _call(_ffn_body, "conv_ffn", (tiles.count, nf), ins, outs,
                 [("h", pltpu.VMEM((tm, d), BF16))], [("out", out)], seq_len=seq_len)


def kernel(x_prompt, x_sample, cache_k, cache_v, state_ret, c, c_ctx, w_mod, b_mod, w_in, w_out,
           ret_log_decay, ret_gn, att_sink, w_up, conv_w, conv_b, w_down, final_gain):
    batch, seq, d = x_prompt.shape
    dec_batch, dec_seq, _ = x_sample.shape
    depth = w_in.shape[0]
    assert TOKEN_TILE % seq == 0 and dec_seq == TOKEN_TILE and 1 + dec_batch <= COND_ROWS
    assert w_in.shape[-1] == D_IN and d == D_RET + D_ATT

    cond = jnp.zeros((COND_ROWS, d), F32).at[0].set(c_ctx).at[1:1 + dec_batch].set(c)
    mod = _modulation(cond, w_mod, b_mod)
    mod4 = mod.reshape(depth, COND_ROWS, 1, N_MOD * d)
    cos, sin_signed = _rope_tables(dec_seq)

    xp = x_prompt.reshape(batch * seq, d)
    xs = x_sample.reshape(dec_batch * dec_seq, d)
    n_ctx_tiles = batch * seq // TOKEN_TILE
    ctx_first = _Tiles(0, 1, 0, 0)
    ctx_rest = _Tiles(1, n_ctx_tiles - 1, 0, 0)
    lat = _Tiles(0, dec_batch, 1, 1)

    new_k = new_v = new_s = None
    for l in range(depth):
        o = _project(xp, mod4, w_in, l, ctx_first, True, cache=(seq, depth, new_k, new_v))
        w_in_l = o["w_bf"]
        o = _project(xp, mod4, w_in_l, l, ctx_rest, False, proj=o["proj"], cache=(seq, depth, o["k"], o["v"]))
        proj, new_k, new_v = o["proj"], o["k"], o["v"]
        y_r, new_s = _retention(proj, ret_log_decay[l], ret_gn[l], seq, l, state_out=(depth, new_s))
        y_a = _context_attention(proj, att_sink[l], seq)
        o = _out_project(y_r, y_a, xp, mod4, w_out, l, ctx_first, True)
        w_out_l = o["w_bf"]
        x1 = _out_project(y_r, y_a, xp, mod4, w_out_l, l, ctx_rest, False, out=o["out"])["out"]
        gain = final_gain if l == depth - 1 else None
        o = _conv_ffn(x1, mod4, w_up, w_up, conv_w, conv_b, w_down, l, ctx_first, True, seq, gain=gain)
        w_gate_l, w_val_l, w_down_l = o["wg_bf"], o["wv_bf"], o["wd_bf"]
        xp = _conv_ffn(x1, mod4, w_gate_l, w_val_l, conv_w, conv_b, w_down_l, l, ctx_rest, False, seq,
                       gain=gain, out=o["out"])["out"]
        proj = _project(xs, mod4, w_in_l, l, lat, False)["proj"]
        (y_r,) = _retention(proj, ret_log_decay[l], ret_gn[l], dec_seq, l, state=state_ret)
        y_a = _latent_attention(proj, att_sink[l], cache_k, cache_v, l, dec_seq, cos, sin_signed)
        x1 = _out_project(y_r, y_a, xs, mod4, w_out_l, l, lat, False)["out"]
        xs = _conv_ffn(x1, mod4, w_gate_l, w_val_l, conv_w, conv_b, w_down_l, l, lat, False, dec_seq,
                       gain=gain)["out"]

    return (xp.reshape(batch, seq, d), xs.reshape(dec_batch, dec_seq, d), new_k, new_v, new_s)
```
